```python
import math
import jax
import jax.numpy as jnp
from jax import lax
import numpy as np

D_MODEL = 1024
BATCH = 16
SEQ = 4096
DEPTH = 1
DEC_BATCH = 1
DEC_SEQ = 16384
PAST_LEN = 128

EPS = 1e-6
NEG_INF = -1e30
ROPE_THETA = 500000.0
MEM_LEN = 256

DSA_GROUPS = ((128, 1), (512, 4), (2048, 16))
DSA_N_GROUPS = 3
DSA_HEADS = 4
DSA_HEAD_DIM = 64
DSA_ROT_DIM = DSA_HEAD_DIM // 4
DSA_BLOCK = 64
DSA_IN_COLS = 3 * DSA_N_GROUPS * DSA_HEADS * DSA_HEAD_DIM
DSA_OUT = DSA_HEADS * DSA_HEAD_DIM

MLA_HEADS = 8
MLA_Q_LORA = 768
MLA_KV_LORA = 256
MLA_NOPE = 64
MLA_ROPE = 32
MLA_V = 64
MLA_QK = MLA_NOPE + MLA_ROPE
MLA_QBLOCK = 128
MLA_IN_COLS = MLA_Q_LORA + MLA_KV_LORA + MLA_ROPE
MLA_OUT = MLA_HEADS * MLA_V

IN_COLS = DSA_IN_COLS + MLA_IN_COLS

MEM_HEADS = 4
MEM_HEAD_DIM = D_MODEL // MEM_HEADS

PEER_HEADS = 8
PEER_NKEYS = 128
PEER_N = PEER_NKEYS * PEER_NKEYS
PEER_DKEY = 256
PEER_HALF = PEER_DKEY // 2
PEER_TOPK = 16
PEER_CHUNK = 128

kernel_name = 'hybrid_dilated_mla_peer_encoder'


def rms_norm(x, g):
    xf = x.astype(jnp.float32)
    y = xf * lax.rsqrt(jnp.mean(xf * xf, axis=-1, keepdims=True) + EPS)
    return (y * g.astype(jnp.float32)).astype(x.dtype)


def rope_tables(seq_len, dim):
    inv_freq = ROPE_THETA ** (-jnp.arange(0, dim, 2, dtype=jnp.float32) / dim)
    ang = jnp.arange(seq_len, dtype=jnp.float32)[:, None] * inv_freq[None, :]
    return jnp.cos(ang), jnp.sin(ang)


def apply_rope(x, cos, sin):
    half = x.shape[-1] // 2
    bshape = (cos.shape[0],) + (1,) * (x.ndim - 3) + (half,)
    c = cos.reshape(bshape).astype(x.dtype)
    s = sin.reshape(bshape).astype(x.dtype)
    x1, x2 = x[..., :half], x[..., half:]
    return jnp.concatenate([x1 * c - x2 * s, x1 * s + x2 * c], axis=-1)


def partial_rope(x, cos, sin, rot_dim):
    return jnp.concatenate([apply_rope(x[..., :rot_dim], cos, sin), x[..., rot_dim:]], axis=-1)


def to_residue(t, d):
    b, s = t.shape[:2]
    l = s // d
    return jnp.swapaxes(t.reshape((b, l, d) + t.shape[2:]), 1, 2).reshape((b * d, l) + t.shape[2:])


def from_residue(t, batch, d):
    l = t.shape[1]
    return jnp.swapaxes(t.reshape((batch, d, l) + t.shape[2:]), 1, 2).reshape((batch, l * d) + t.shape[2:])


def dilated_window_group(q, k, v, window, dilation):
    batch, seq, n_heads, head_dim = q.shape
    half = window // (2 * dilation)
    qr, kr, vr = to_residue(q, dilation), to_residue(k, dilation), to_residue(v, dilation)
    n, l = qr.shape[:2]
    blk = DSA_BLOCK
    nb = -(-l // blk)
    lp = nb * blk
    kw_len = blk + 2 * half
    qb = jnp.pad(qr, ((0, 0), (0, lp - l), (0, 0), (0, 0))).reshape(n, nb, blk, n_heads, head_dim)
    kpad = ((0, 0), (half, lp - l + half), (0, 0), (0, 0))
    idx = jnp.arange(nb)[:, None] * blk + jnp.arange(kw_len)[None, :]
    kw = jnp.pad(kr, kpad)[:, idx]
    vw = jnp.pad(vr, kpad)[:, idx]
    s = jnp.einsum('nbqhd,nbkhd->nbhqk', qb, kw).astype(jnp.float32) * (head_dim ** -0.5)
    key_pos = idx - half
    q_pos = jnp.arange(nb)[:, None] * blk + jnp.arange(blk)[None, :]
    rel = key_pos[:, None, :] - q_pos[:, :, None]
    ok = (jnp.abs(rel) <= half) & (key_pos[:, None, :] >= 0) & (key_pos[:, None, :] < l)
    s = jnp.where(ok[None, :, None], s, NEG_INF)
    m = jnp.max(s, axis=-1, keepdims=True)
    p = jnp.exp(s - m)
    den = jnp.sum(p, axis=-1, keepdims=True)
    o = jnp.einsum('nbhqk,nbkhd->nbqhd', (p / den).astype(v.dtype), vw)
    o = o.reshape(n, lp, n_heads, head_dim)[:, :l]
    lse = jnp.swapaxes((m + jnp.log(den))[..., 0], 2, 3).reshape(n, lp, n_heads)[:, :l]
    return from_residue(o, batch, dilation), from_residue(lse, batch, dilation)


def dilated_attention(za, cos, sin):
    b, s = za.shape[:2]
    q = partial_rope(za[:, :, 0], cos, sin, DSA_ROT_DIM)
    k = partial_rope(za[:, :, 1], cos, sin, DSA_ROT_DIM)
    v = za[:, :, 2]
    outs, lses = [], []
    for g, (window, dilation) in enumerate(DSA_GROUPS):
        o, lse = dilated_window_group(q[:, :, g], k[:, :, g], v[:, :, g], window, dilation)
        outs.append(o)
        lses.append(lse)
    alpha = jax.nn.softmax(jnp.stack(lses, axis=0), axis=0).astype(v.dtype)
    o = jnp.einsum('gbsh,gbshd->bshd', alpha, jnp.stack(outs, axis=0))
    return o.reshape(b, s, DSA_OUT)


def mla_attention(z, g_qa, g_kva, w_uq, w_ukv, cos, sin):
    b, s = z.shape[:2]
    c_q = z[..., :MLA_Q_LORA]
    c_kv = z[..., MLA_Q_LORA:MLA_Q_LORA + MLA_KV_LORA]
    k_pe = z[..., MLA_Q_LORA + MLA_KV_LORA:]
    q = (rms_norm(c_q, g_qa) @ w_uq).reshape(b, s, MLA_HEADS, MLA_QK)
    q = jnp.concatenate([q[..., :MLA_NOPE], apply_rope(q[..., MLA_NOPE:], cos, sin)], axis=-1)
    kv = (rms_norm(c_kv, g_kva) @ w_ukv).reshape(b, s, MLA_HEADS, MLA_NOPE + MLA_V)
    k_pe = apply_rope(k_pe[:, :, None, :], cos, sin)
    k = jnp.concatenate([kv[..., :MLA_NOPE], jnp.broadcast_to(k_pe, (b, s, MLA_HEADS, MLA_ROPE))], axis=-1)
    v = kv[..., MLA_NOPE:]
    nq = s // MLA_QBLOCK
    qb = jnp.swapaxes(q.reshape(b, nq, MLA_QBLOCK, MLA_HEADS, MLA_QK), 0, 1)
    scale = MLA_QK ** -0.5

    def one_block(qi):
        sc = jnp.einsum('bqhd,bkhd->bhqk', qi, k).astype(jnp.float32) * scale
        p = jax.nn.softmax(sc, axis=-1).astype(v.dtype)
        return jnp.einsum('bhqk,bkhd->bqhd', p, v)

    o = lax.map(one_block, qb)
    return jnp.swapaxes(o, 0, 1).reshape(b, s, MLA_OUT)


def memory_cross_attention(h, m, w_mq, w_mkv, w_mo):
    b, s, d = h.shape
    n_mem = m.shape[1]
    q = (h @ w_mq).reshape(b, s, MEM_HEADS, MEM_HEAD_DIM)
    kv = (m @ w_mkv).reshape(b, n_mem, 2, MEM_HEADS, MEM_HEAD_DIM)
    sc = jnp.einsum('bshd,bmhd->bhsm', q, kv[:, :, 0]).astype(jnp.float32) * (MEM_HEAD_DIM ** -0.5)
    p = jax.nn.softmax(sc, axis=-1).astype(h.dtype)
    o = jnp.einsum('bhsm,bmhd->bshd', p, kv[:, :, 1]).reshape(b, s, d)
    return o @ w_mo


def peer_layer(h, w_pq, sub_keys, peer_u, peer_v):
    b, s, d = h.shape
    ht = h.reshape((b * s) // PEER_CHUNK, PEER_CHUNK, d)

    def chunk(xc):
        q = (xc @ w_pq).reshape(PEER_CHUNK, PEER_HEADS, 2, PEER_HALF)
        sc = jnp.einsum('thpd,hpnd->thpn', q, sub_keys).astype(jnp.float32)
        s1, i1 = lax.top_k(sc[:, :, 0], PEER_TOPK)
        s2, i2 = lax.top_k(sc[:, :, 1], PEER_TOPK)
        cand = (s1[..., :, None] + s2[..., None, :]).reshape(PEER_CHUNK, PEER_HEADS, PEER_TOPK * PEER_TOPK)
        cidx = (i1[..., :, None] * PEER_NKEYS + i2[..., None, :]).reshape(PEER_CHUNK, PEER_HEADS, PEER_TOPK * PEER_TOPK)
        top_s, pos = lax.top_k(cand, PEER_TOPK)
        eidx = jnp.take_along_axis(cidx, pos, axis=-1)
        g = jax.nn.softmax(top_s, axis=-1)
        ue = peer_u[eidx]
        a = jax.nn.gelu(jnp.einsum('td,thkd->thk', xc, ue).astype(jnp.float32), approximate=False)
        ve = peer_v[eidx]
        return jnp.einsum('thk,thkd->td', (g * a).astype(xc.dtype), ve)

    return lax.map(chunk, ht).reshape(b, s, d)


def run_trunk(x, mem, g_mix, w_in, g_qa, g_kva, w_uq, w_ukv, w_gate, b_gate, w_pa, w_pb, w_out,
              g_xq, g_mkv, w_mq, w_mkv, w_mo, g_ffn, w_pq, sub_keys, peer_u, peer_v, g_final):
    b, s, d = x.shape
    cos_a, sin_a = rope_tables(s, DSA_ROT_DIM)
    cos_b, sin_b = rope_tables(s, MLA_ROPE)
    for l in range(DEPTH):
        h = rms_norm(x, g_mix[l])
        z = h @ w_in[l]
        za = z[..., :DSA_IN_COLS].reshape(b, s, 3, DSA_N_GROUPS, DSA_HEADS, DSA_HEAD_DIM)
        ya = dilated_attention(za, cos_a, sin_a)
        yb = mla_attention(z[..., DSA_IN_COLS:], g_qa[l], g_kva[l], w_uq[l], w_ukv[l], cos_b, sin_b)
        gates = jax.nn.sigmoid((h @ w_gate[l] + b_gate[l]).astype(jnp.float32)).astype(x.dtype)
        merged = gates[..., :d] * (ya @ w_pa[l]) + gates[..., d:] * (yb @ w_pb[l])
        x = x + merged @ w_out[l]
        x = x + memory_cross_attention(rms_norm(x, g_xq[l]), rms_norm(mem, g_mkv[l]), w_mq[l], w_mkv[l], w_mo[l])
        x = x + peer_layer(rms_norm(x, g_ffn[l]), w_pq[l], sub_keys[l], peer_u[l], peer_v[l])
    return rms_norm(x, g_final)


def setup_inputs(seed: int = 0) -> dict:
    key = jax.random.key(seed)
    ks = jax.random.split(key, 32)

    def nrm(k, shape, scale):
        return jax.random.normal(k, shape, dtype=jnp.float32) * scale

    def gain(k, shape):
        return 1.0 + 0.02 * jax.random.normal(k, shape, dtype=jnp.float32)

    dm = D_MODEL
    return {
        'x_prompt': nrm(ks[0], (BATCH, SEQ, dm), 1.0),
        'x_sample': nrm(ks[1], (DEC_BATCH, DEC_SEQ, dm), 1.0),
        'mem_prompt': nrm(ks[2], (BATCH, MEM_LEN, dm), 1.0),
        'mem_sample': nrm(ks[3], (DEC_BATCH, MEM_LEN, dm), 1.0),
        'g_mix': gain(ks[4], (DEPTH, dm)),
        'w_in': nrm(ks[5], (DEPTH, dm, IN_COLS), dm ** -0.5),
        'g_qa': gain(ks[6], (DEPTH, MLA_Q_LORA)),
        'g_kva': gain(ks[7], (DEPTH, MLA_KV_LORA)),
        'w_uq': nrm(ks[8], (DEPTH, MLA_Q_LORA, MLA_HEADS * MLA_QK), MLA_Q_LORA ** -0.5),
        'w_ukv': nrm(ks[9], (DEPTH, MLA_KV_LORA, MLA_HEADS * (MLA_NOPE + MLA_V)), MLA_KV_LORA ** -0.5),
        'w_gate': nrm(ks[10], (DEPTH, dm, 2 * dm), dm ** -0.5),
        'b_gate': nrm(ks[11], (DEPTH, 2 * dm), 0.01),
        'w_pa': nrm(ks[12], (DEPTH, DSA_OUT, dm), DSA_OUT ** -0.5),
        'w_pb': nrm(ks[13], (DEPTH, MLA_OUT, dm), MLA_OUT ** -0.5),
        'w_out': nrm(ks[14], (DEPTH, dm, dm), dm ** -0.5),
        'g_xq': gain(ks[15], (DEPTH, dm)),
        'g_mkv': gain(ks[16], (DEPTH, dm)),
        'w_mq': nrm(ks[17], (DEPTH, dm, dm), dm ** -0.5),
        'w_mkv': nrm(ks[18], (DEPTH, dm, 2 * dm), dm ** -0.5),
        'w_mo': nrm(ks[19], (DEPTH, dm, dm), dm ** -0.5),
        'g_ffn': gain(ks[20], (DEPTH, dm)),
        'w_pq': nrm(ks[21], (DEPTH, dm, PEER_HEADS * PEER_DKEY), dm ** -0.5),
        'sub_keys': nrm(ks[22], (DEPTH, PEER_HEADS, 2, PEER_NKEYS, PEER_HALF), PEER_HALF ** -0.5),
        'peer_u': nrm(ks[23], (DEPTH, PEER_N, dm), dm ** -0.5),
        'peer_v': nrm(ks[24], (DEPTH, PEER_N, dm), 0.5),
        'g_final': gain(ks[25], (dm,)),
    }


def reference(x_prompt, x_sample, mem_prompt, mem_sample, g_mix, w_in, g_qa, g_kva, w_uq, w_ukv,
              w_gate, b_gate, w_pa, w_pb, w_out, g_xq, g_mkv, w_mq, w_mkv, w_mo, g_ffn, w_pq,
              sub_keys, peer_u, peer_v, g_final):
    y_prompt = run_trunk(x_prompt, mem_prompt, g_mix, w_in, g_qa, g_kva, w_uq, w_ukv, w_gate, b_gate,
                         w_pa, w_pb, w_out, g_xq, g_mkv, w_mq, w_mkv, w_mo, g_ffn, w_pq, sub_keys,
                         peer_u, peer_v, g_final)
    y_sample = run_trunk(x_sample, mem_sample, g_mix, w_in, g_qa, g_kva, w_uq, w_ukv, w_gate, b_gate,
                         w_pa, w_pb, w_out, g_xq, g_mkv, w_mq, w_mkv, w_mo, g_ffn, w_pq, sub_keys,
                         peer_u, peer_v, g_final)
    return (y_prompt, y_sample)
```

```python
import functools

import jax
import jax.numpy as jnp
from jax import lax
from jax.experimental import pallas as pl
from jax.experimental.pallas import tpu as pltpu

F32 = jnp.float32
BF16 = jnp.bfloat16

D_MODEL = 1024
EPS = 1e-6
NEG_INF = -1e30
ROPE_THETA = 500000.0

DSA_DILATIONS = (1, 4, 16)
DSA_HALF = 64
DSA_HEADS = 4
DSA_HEAD_DIM = 64
DSA_ROT = 16
DSA_GROUP_COLS = DSA_HEADS * DSA_HEAD_DIM
DSA_QK_COLS = 2 * 3 * DSA_GROUP_COLS
DSA_V_COLS = 3 * DSA_GROUP_COLS

MLA_HEADS = 8
MLA_Q_LORA = 768
MLA_KV_LORA = 256
MLA_NOPE = 64
MLA_ROPE = 32
MLA_V = 64
MLA_QK = MLA_NOPE + MLA_ROPE
MLA_SLAB = 128

MEM_HEADS = 4
MEM_HEAD_DIM = 256

PEER_HEADS = 8
PEER_NKEYS = 128
PEER_HALF = 128
PEER_TOPK = 16

LANES = 128
VMEM_LIMIT_BYTES = 56 * 1024 * 1024


def _params(sem):
    return pltpu.CompilerParams(dimension_semantics=sem, vmem_limit_bytes=VMEM_LIMIT_BYTES)


def _const_spec(shape):
    nd = len(shape)
    return pl.BlockSpec(shape, lambda *_: (0,) * nd)


def _rms(x, g):
    y = x * lax.rsqrt(jnp.mean(x * x, axis=-1, keepdims=True) + EPS)
    return y * g


def _dot(a, b):
    return jnp.dot(a, b, preferred_element_type=F32)


def _dot_nt(a, b):
    return lax.dot_general(a, b, (((1,), (1,)), ((), ())), preferred_element_type=F32)


def _rope_slab(x, cos_t, sin_t, take_up, half):
    up = pltpu.roll(x, LANES - half, 1)
    dn = pltpu.roll(x, half, 1)
    return x * cos_t + jnp.where(take_up, up, dn) * sin_t


def _inproj_kernel(x_ref, g_ref, wqk_ref, wv_ref, wc_ref, wkpe_ref, wg_ref, bg_ref,
                   ca_ref, sa_ref, cb_ref, sb_ref,
                   qk_ref, v_ref, cq_ref, ckv_ref, kpe_ref, gate_ref):
    h = _rms(x_ref[...], g_ref[...]).astype(BF16)
    rows = h.shape[0]
    lane = lax.broadcasted_iota(jnp.int32, (rows, LANES), 1)
    up_a = (lane % DSA_HEAD_DIM) < (DSA_ROT // 2)
    ca = ca_ref[...]
    sa = sa_ref[...]
    z = _dot(h, wqk_ref[...])
    for s in range(DSA_QK_COLS // LANES):
        sl = slice(s * LANES, (s + 1) * LANES)
        qk_ref[:, sl] = _rope_slab(z[:, sl], ca, sa, up_a, DSA_ROT // 2).astype(BF16)
    v_ref[...] = _dot(h, wv_ref[...]).astype(BF16)
    c = _dot(h, wc_ref[...])
    cq_ref[...] = c[:, :MLA_Q_LORA]
    ckv_ref[...] = c[:, MLA_Q_LORA:]
    up_b = (lane >= MLA_NOPE) & (lane < MLA_NOPE + MLA_ROPE // 2)
    kpe_ref[...] = _rope_slab(_dot(h, wkpe_ref[...]), cb_ref[...], sb_ref[...], up_b, MLA_ROPE // 2)
    gate_ref[...] = jax.nn.sigmoid(_dot(h, wg_ref[...]) + bg_ref[...])


def _inproj(x2d, seq, w, tabs, tm):
    t = x2d.shape[0]
    nseq = seq // tm
    row = lambda i: (i, 0)
    tab = lambda i: (i % nseq, 0)
    out_shape = (
        jax.ShapeDtypeStruct((t, DSA_QK_COLS), BF16),
        jax.ShapeDtypeStruct((t, DSA_V_COLS), BF16),
        jax.ShapeDtypeStruct((t, MLA_Q_LORA), F32),
        jax.ShapeDtypeStruct((t, MLA_KV_LORA), F32),
        jax.ShapeDtypeStruct((t, MLA_SLAB), F32),
        jax.ShapeDtypeStruct((t, 2 * D_MODEL), F32),
    )
    return pl.pallas_call(
        _inproj_kernel,
        grid=(t // tm,),
        in_specs=[
            pl.BlockSpec((tm, D_MODEL), row),
            _const_spec((1, D_MODEL)),
            _const_spec(w['wqk'].shape), _const_spec(w['wv'].shape), _const_spec(w['wc'].shape),
            _const_spec(w['wkpe'].shape), _const_spec(w['wgate'].shape), _const_spec((1, 2 * D_MODEL)),
            pl.BlockSpec((tm, LANES), tab), pl.BlockSpec((tm, LANES), tab),
            pl.BlockSpec((tm, LANES), tab), pl.BlockSpec((tm, LANES), tab),
        ],
        out_specs=[pl.BlockSpec((tm, s.shape[1]), row) for s in out_shape],
        out_shape=out_shape,
        compiler_params=_params(("parallel",)),
        name="inproj",
    )(x2d, w['g_mix'], w['wqk'], w['wv'], w['wc'], w['wkpe'], w['wgate'], w['b_gate'],
      tabs['ca'], tabs['sa'], tabs['cb'], tabs['sb'])


def _mla_proj_kernel(cq_ref, ckv_ref, kpe_ref, gqa_ref, gkva_ref, wq_ref, wk_ref, wv_ref,
                     cb_ref, sb_ref, q_ref, k_ref, v_ref):
    qn = _rms(cq_ref[...], gqa_ref[...]).astype(BF16)
    rows = qn.shape[0]
    lane = lax.broadcasted_iota(jnp.int32, (rows, LANES), 1)
    up_b = (lane >= MLA_NOPE) & (lane < MLA_NOPE + MLA_ROPE // 2)
    cb = cb_ref[...]
    sb = sb_ref[...]
    q = _dot(qn, wq_ref[...])
    kvn = _rms(ckv_ref[...], gkva_ref[...]).astype(BF16)
    k = _dot(kvn, wk_ref[...])
    kpe = kpe_ref[...]
    scale = MLA_QK ** -0.5
    for s in range(MLA_HEADS):
        sl = slice(s * LANES, (s + 1) * LANES)
        q_ref[:, sl] = (_rope_slab(q[:, sl], cb, sb, up_b, MLA_ROPE // 2) * scale).astype(BF16)
        k_ref[:, sl] = (k[:, sl] + kpe).astype(BF16)
    v_ref[...] = _dot(kvn, wv_ref[...]).astype(BF16)


def _mla_proj(cq, ckv, kpe, seq, w, tabs, tm):
    t = cq.shape[0]
    nseq = seq // tm
    row = lambda i: (i, 0)
    tab = lambda i: (i % nseq, 0)
    wide = MLA_HEADS * MLA_SLAB
    out_shape = tuple(jax.ShapeDtypeStruct((t, wide), BF16) for _ in range(3))
    return pl.pallas_call(
        _mla_proj_kernel,
        grid=(t // tm,),
        in_specs=[
            pl.BlockSpec((tm, MLA_Q_LORA), row), pl.BlockSpec((tm, MLA_KV_LORA), row),
            pl.BlockSpec((tm, MLA_SLAB), row),
            _const_spec((1, MLA_Q_LORA)), _const_spec((1, MLA_KV_LORA)),
            _const_spec(w['wq_mla'].shape), _const_spec(w['wk_mla'].shape), _const_spec(w['wv_mla'].shape),
            pl.BlockSpec((tm, LANES), tab), pl.BlockSpec((tm, LANES), tab),
        ],
        out_specs=[pl.BlockSpec((tm, wide), row) for _ in range(3)],
        out_shape=out_shape,
        compiler_params=_params(("parallel",)),
        name="mla_proj",
    )(cq, ckv, kpe, w['g_qa'], w['g_kva'], w['wq_mla'], w['wk_mla'], w['wv_mla'], tabs['cb'], tabs['sb'])


def _mla_attn_kernel(q_ref, k_ref, v_ref, o_ref, m_ref, l_ref, acc_ref):
    j = pl.program_id(3)

    @pl.when(j == 0)
    def _():
        m_ref[...] = jnp.full(m_ref.shape, -jnp.inf, F32)
        l_ref[...] = jnp.zeros(l_ref.shape, F32)
        acc_ref[...] = jnp.zeros(acc_ref.shape, F32)

    s = _dot_nt(q_ref[...], k_ref[...])
    m_prev = m_ref[...]
    m_new = jnp.maximum(m_prev, jnp.max(s, axis=1, keepdims=True))
    alpha = jnp.exp(m_prev - m_new)
    p = jnp.exp(s - m_new[:, :1])
    l_ref[...] = alpha * l_ref[...] + jnp.sum(p, axis=1, keepdims=True)
    acc_ref[...] = alpha * acc_ref[...] + _dot(p.astype(BF16), v_ref[...])
    m_ref[...] = m_new

    @pl.when(j == pl.num_programs(3) - 1)
    def _():
        o_ref[...] = (acc_ref[...] / l_ref[...]).astype(o_ref.dtype)


def _mla_attn(q, k, v, tq, tk):
    b, s, wide = q.shape
    return pl.pallas_call(
        _mla_attn_kernel,
        grid=(b, MLA_HEADS, s // tq, s // tk),
        in_specs=[
            pl.BlockSpec((None, tq, MLA_SLAB), lambda bi, h, i, j: (bi, i, h)),
            pl.BlockSpec((None, tk, MLA_SLAB), lambda bi, h, i, j: (bi, j, h)),
            pl.BlockSpec((None, tk, MLA_SLAB), lambda bi, h, i, j: (bi, j, h)),
        ],
        out_specs=pl.BlockSpec((None, tq, MLA_SLAB), lambda bi, h, i, j: (bi, i, h)),
        out_shape=jax.ShapeDtypeStruct((b, s, wide), BF16),
        scratch_shapes=[pltpu.VMEM((tq, LANES), F32), pltpu.VMEM((tq, LANES), F32),
                        pltpu.VMEM((tq, MLA_SLAB), F32)],
        compiler_params=_params(("parallel", "parallel", "parallel", "arbitrary")),
        name="mla_attn",
    )(q, k, v)


def _dsa_kernel(q_ref, kp_ref, kc_ref, kn_ref, vp_ref, vc_ref, vn_ref, o_ref, lse_ref, *, cls_len):
    j = pl.program_id(2)
    tq = q_ref.shape[0]
    nk = tq + 2 * DSA_HALF
    q = q_ref[...]
    kk = jnp.concatenate([kp_ref[...], kc_ref[...], kn_ref[...]], axis=0)
    vv = jnp.concatenate([vp_ref[...], vc_ref[...], vn_ref[...]], axis=0)
    qpos = j * tq + lax.broadcasted_iota(jnp.int32, (tq, nk), 0)
    kpos = j * tq - DSA_HALF + lax.broadcasted_iota(jnp.int32, (tq, nk), 1)
    ok = (jnp.abs(kpos - qpos) <= DSA_HALF) & (kpos >= 0) & (kpos < cls_len)
    lane = lax.broadcasted_iota(jnp.int32, (tq, DSA_GROUP_COLS), 1)
    o = jnp.zeros((tq, DSA_GROUP_COLS), F32)
    lse = jnp.zeros((tq, DSA_GROUP_COLS), F32)
    for h in range(DSA_HEADS):
        in_head = (lane // DSA_HEAD_DIM) == h
        s = _dot_nt(jnp.where(in_head, q, jnp.zeros_like(q)), kk)
        s = jnp.where(ok, s, NEG_INF)
        m = jnp.max(s, axis=1, keepdims=True)
        p = jnp.exp(s - m)
        den = jnp.sum(p, axis=1, keepdims=True)
        oh = _dot((p / den).astype(BF16), vv)
        o = jnp.where(in_head, oh, o)
        lse = jnp.where(in_head, m + jnp.log(den), lse)
    o_ref[...] = o
    lse_ref[...] = lse


def _dsa_group(qk, v, g, dil, tq):
    b, s, _ = qk.shape
    cls_len = s // dil
    tq = min(tq, cls_len)
    qk_v = qk.reshape(b, cls_len, dil * DSA_QK_COLS)
    v_v = v.reshape(b, cls_len, dil * DSA_V_COLS)
    per = tq // DSA_HALF
    last = cls_len // DSA_HALF - 1
    qk_blocks = DSA_QK_COLS // DSA_GROUP_COLS
    v_blocks = DSA_V_COLS // DSA_GROUP_COLS
    qcol = lambda r: r * qk_blocks + g
    kcol = lambda r: r * qk_blocks + 3 + g
    vcol = lambda r: r * v_blocks + g
    prev = lambda j: jnp.maximum(j * per - 1, 0)
    nxt = lambda j: jnp.minimum((j + 1) * per, last)
    halo = (None, DSA_HALF, DSA_GROUP_COLS)
    cur = (None, tq, DSA_GROUP_COLS)
    out_sd = jax.ShapeDtypeStruct((b, cls_len, dil * DSA_GROUP_COLS), F32)
    o, lse = pl.pallas_call(
        functools.partial(_dsa_kernel, cls_len=cls_len),
        grid=(b, dil, cls_len // tq),
        in_specs=[
            pl.BlockSpec(cur, lambda bi, r, j: (bi, j, qcol(r))),
            pl.BlockSpec(halo, lambda bi, r, j: (bi, prev(j), kcol(r))),
            pl.BlockSpec(cur, lambda bi, r, j: (bi, j, kcol(r))),
            pl.BlockSpec(halo, lambda bi, r, j: (bi, nxt(j), kcol(r))),
            pl.BlockSpec(halo, lambda bi, r, j: (bi, prev(j), vcol(r))),
            pl.BlockSpec(cur, lambda bi, r, j: (bi, j, vcol(r))),
            pl.BlockSpec(halo, lambda bi, r, j: (bi, nxt(j), vcol(r))),
        ],
        out_specs=[pl.BlockSpec(cur, lambda bi, r, j: (bi, j, r)),
                   pl.BlockSpec(cur, lambda bi, r, j: (bi, j, r))],
        out_shape=(out_sd, out_sd),
        compiler_params=_params(("parallel", "parallel", "parallel")),
        name=f"dsa_g{g}",
    )(qk_v, qk_v, qk_v, qk_v, v_v, v_v, v_v)
    return o.reshape(b * s, DSA_GROUP_COLS), lse.reshape(b * s, DSA_GROUP_COLS)


def _merge_kernel(o0_ref, l0_ref, o1_ref, l1_ref, o2_ref, l2_ref, yb_ref, gate_ref, x_ref,
                  wpa_ref, wpb_ref, wout_ref, out_ref):
    l0, l1, l2 = l0_ref[...], l1_ref[...], l2_ref[...]
    m = jnp.maximum(jnp.maximum(l0, l1), l2)
    e0, e1, e2 = jnp.exp(l0 - m), jnp.exp(l1 - m), jnp.exp(l2 - m)
    den = e0 + e1 + e2
    ya = (e0 / den) * o0_ref[...] + (e1 / den) * o1_ref[...] + (e2 / den) * o2_ref[...]
    gate = gate_ref[...]
    pa = _dot(ya.astype(BF16), wpa_ref[...])
    pb = _dot(yb_ref[...], wpb_ref[...])
    merged = gate[:, :D_MODEL] * pa + gate[:, D_MODEL:] * pb
    out_ref[...] = x_ref[...] + _dot(merged.astype(BF16), wout_ref[...])


def _merge(dsa_out, yb, gates, x2d, w, tm):
    t = x2d.shape[0]
    row = lambda i: (i, 0)
    grp = pl.BlockSpec((tm, DSA_GROUP_COLS), row)
    flat = [a for pair in dsa_out for a in pair]
    return pl.pallas_call(
        _merge_kernel,
        grid=(t // tm,),
        in_specs=[grp] * 6 + [
            pl.BlockSpec((tm, MLA_HEADS * MLA_SLAB), row),
            pl.BlockSpec((tm, 2 * D_MODEL), row),
            pl.BlockSpec((tm, D_MODEL), row),
            _const_spec(w['wpa'].shape), _const_spec(w['wpb'].shape), _const_spec(w['wout'].shape),
        ],
        out_specs=pl.BlockSpec((tm, D_MODEL), row),
        out_shape=jax.ShapeDtypeStruct((t, D_MODEL), F32),
        compiler_params=_params(("parallel",)),
        name="merge",
    )(*flat, yb, gates, x2d, w['wpa'], w['wpb'], w['wout'])


def _norm_mm_kernel(x_ref, g_ref, w_ref, o_ref):
    o_ref[...] = _dot(_rms(x_ref[...], g_ref[...]).astype(BF16), w_ref[...]).astype(o_ref.dtype)


def _norm_mm(x2d, g, wmat, tm):
    t, d = x2d.shape
    n = wmat.shape[1]
    return pl.pallas_call(
        _norm_mm_kernel,
        grid=(t // tm,),
        in_specs=[pl.BlockSpec((tm, d), lambda i: (i, 0)), _const_spec((1, d)), _const_spec(wmat.shape)],
        out_specs=pl.BlockSpec((tm, n), lambda i: (i, 0)),
        out_shape=jax.ShapeDtypeStruct((t, n), BF16),
        compiler_params=_params(("parallel",)),
        name="mem_kv",
    )(x2d, g, wmat)


def _xattn_kernel(x_ref, g_ref, kv_ref, wq_ref, wo_ref, out_ref):
    x = x_ref[...]
    hq = _rms(x, g_ref[...]).astype(BF16)
    q = (_dot(hq, wq_ref[...]) * (MEM_HEAD_DIM ** -0.5)).astype(BF16)
    kv = kv_ref[...]
    outs = []
    for h in range(MEM_HEADS):
        sl = slice(h * MEM_HEAD_DIM, (h + 1) * MEM_HEAD_DIM)
        vsl = slice(D_MODEL + h * MEM_HEAD_DIM, D_MODEL + (h + 1) * MEM_HEAD_DIM)
        s = _dot_nt(q[:, sl], kv[:, sl])
        m = jnp.max(s, axis=1, keepdims=True)
        p = jnp.exp(s - m)
        den = jnp.sum(p, axis=1, keepdims=True)
        outs.append(_dot((p / den).astype(BF16), kv[:, vsl]).astype(BF16))
    o = jnp.concatenate(outs, axis=1)
    out_ref[...] = x + _dot(o, wo_ref[...])


def _xattn(x3d, kv3d, w, tm):
    b, s, d = x3d.shape
    n_mem = kv3d.shape[1]
    return pl.pallas_call(
        _xattn_kernel,
        grid=(b, s // tm),
        in_specs=[
            pl.BlockSpec((None, tm, d), lambda bi, i: (bi, i, 0)),
            _const_spec((1, d)),
            pl.BlockSpec((None, n_mem, 2 * d), lambda bi, i: (bi, 0, 0)),
            _const_spec(w['wmq'].shape), _const_spec(w['wmo'].shape),
        ],
        out_specs=pl.BlockSpec((None, tm, d), lambda bi, i: (bi, i, 0)),
        out_shape=jax.ShapeDtypeStruct((b, s, d), F32),
        compiler_params=_params(("parallel", "parallel")),
        name="xattn",
    )(x3d, w['g_xq'], kv3d, w['wmq'], w['wmo'])


def _top_values(sc, count):
    n = sc.shape[0]
    row = lax.broadcasted_iota(jnp.int32, sc.shape, 0)
    vals = []
    for _ in range(count):
        m = jnp.max(sc, axis=0, keepdims=True)
        first = jnp.min(jnp.where(sc == m, row, n), axis=0, keepdims=True)
        vals.append(m)
        sc = jnp.where(row == first, -jnp.inf, sc)
    return jnp.concatenate(vals, axis=0)


def _peer_score_kernel(x_ref, g_ref, wpq_ref, keys_ref, hq_ref, s1_ref, e1_ref, s2_ref, e2_ref, tau_ref):
    hq = _rms(x_ref[...], g_ref[...]).astype(BF16)

    @pl.when(pl.program_id(1) == 0)
    def _():
        hq_ref[...] = hq

    q = _dot(hq, wpq_ref[...]).astype(BF16)
    s1 = _dot_nt(keys_ref[0], q[:, :PEER_HALF])
    s2 = _dot_nt(keys_ref[1], q[:, PEER_HALF:])
    v1 = _top_values(s1, PEER_TOPK)
    v2 = _top_values(s2, PEER_TOPK)
    cand = [v1[0:1] + v2]
    cand += [v1[i:i + 1] + v2[0:8] for i in range(1, 8)]
    cand += [v1[8:16] + v2[0:1]]
    top = _top_values(jnp.concatenate(cand, axis=0), PEER_TOPK)
    z = jnp.sum(jnp.exp(top - top[0:1]), axis=0, keepdims=True)
    s1_ref[...] = s1
    s2_ref[...] = s2
    e1_ref[...] = jnp.exp(s1 - v1[0:1]) / z
    e2_ref[...] = jnp.exp(s2 - v2[0:1])
    tau_ref[...] = top[PEER_TOPK - 1:PEER_TOPK]


def _peer_scores(x2d, w, tt):
    t = x2d.shape[0]
    stat = jax.ShapeDtypeStruct((PEER_HEADS, PEER_NKEYS, t), F32)
    stat_spec = pl.BlockSpec((None, PEER_NKEYS, tt), lambda i, h: (h, 0, i))
    return pl.pallas_call(
        _peer_score_kernel,
        grid=(t // tt, PEER_HEADS),
        in_specs=[
            pl.BlockSpec((tt, D_MODEL), lambda i, h: (i, 0)),
            _const_spec((1, D_MODEL)),
            pl.BlockSpec((D_MODEL, 2 * PEER_HALF), lambda i, h: (0, h)),
            pl.BlockSpec((None, 2, PEER_NKEYS, PEER_HALF), lambda i, h: (h, 0, 0, 0)),
        ],
        out_specs=[
            pl.BlockSpec((tt, D_MODEL), lambda i, h: (i, 0)),
            stat_spec, stat_spec, stat_spec, stat_spec,
            pl.BlockSpec((None, 1, tt), lambda i, h: (h, 0, i)),
        ],
        out_shape=(jax.ShapeDtypeStruct((t, D_MODEL), BF16), stat, stat, stat, stat,
                   jax.ShapeDtypeStruct((PEER_HEADS, 1, t), F32)),
        compiler_params=_params(("parallel", "arbitrary")),
        name="peer_scores",
    )(x2d, w['g_ffn'], w['wpq'], w['keys'])


def _peer_mix_kernel(hq_ref, u_ref, vt_ref, s1_ref, e1_ref, s2_ref, e2_ref, tau_ref, x_ref, gf_ref,
                     out_ref, acc_ref, act_ref, wg_ref, *, blocks):
    j = pl.program_id(1)

    @pl.when(j == 0)
    def _():
        acc_ref[...] = jnp.zeros(acc_ref.shape, F32)

    act_ref[...] = _dot_nt(u_ref[...], hq_ref[...])
    tt = hq_ref.shape[0]

    def one_block(i, carry):
        a = j * blocks + i
        wsum = jnp.zeros((PEER_NKEYS, tt), F32)
        for h in range(PEER_HEADS):
            pair = s1_ref[h, pl.ds(a, 1), :] + s2_ref[h]
            gate = e1_ref[h, pl.ds(a, 1), :] * e2_ref[h]
            wsum = wsum + jnp.where(pair >= tau_ref[h], gate, 0.0)
        off = pl.multiple_of(i * PEER_NKEYS, PEER_NKEYS)
        pre = act_ref[pl.ds(off, PEER_NKEYS), :]
        act = 0.5 * pre * (1.0 + lax.erf(pre * (0.5 ** 0.5)))
        wg_ref[pl.ds(off, PEER_NKEYS), :] = (wsum * act).astype(BF16)
        return carry

    lax.fori_loop(0, blocks, one_block, 0)
    acc_ref[...] += _dot(vt_ref[...], wg_ref[...])

    @pl.when(j == pl.num_programs(1) - 1)
    def _():
        out_ref[...] = _rms(x_ref[...] + acc_ref[...].T, gf_ref[...])


def _peer_mix(hq, stats, x2d, w, tt, blocks):
    t = x2d.shape[0]
    s1, e1, s2, e2, tau = stats
    ne = blocks * PEER_NKEYS
    stat_spec = pl.BlockSpec((PEER_HEADS, PEER_NKEYS, tt), lambda i, j: (0, 0, i))
    return pl.pallas_call(
        functools.partial(_peer_mix_kernel, blocks=blocks),
        grid=(t // tt, PEER_NKEYS // blocks),
        in_specs=[
            pl.BlockSpec((tt, D_MODEL), lambda i, j: (i, 0)),
            pl.BlockSpec((ne, D_MODEL), lambda i, j: (j, 0)),
            pl.BlockSpec((D_MODEL, ne), lambda i, j: (0, j)),
            stat_spec, stat_spec, stat_spec, stat_spec,
            pl.BlockSpec((PEER_HEADS, 1, tt), lambda i, j: (0, 0, i)),
            pl.BlockSpec((tt, D_MODEL), lambda i, j: (i, 0)),
            _const_spec((1, D_MODEL)),
        ],
        out_specs=pl.BlockSpec((tt, D_MODEL), lambda i, j: (i, 0)),
        out_shape=jax.ShapeDtypeStruct((t, D_MODEL), F32),
        scratch_shapes=[pltpu.VMEM((D_MODEL, tt), F32), pltpu.VMEM((ne, tt), F32),
                        pltpu.VMEM((ne, tt), BF16)],
        compiler_params=_params(("parallel", "arbitrary")),
        name="peer_mix",
    )(hq, w['peer_u'], w['peer_vt'], s1, e1, s2, e2, tau, x2d, w['g_final'])


def _prep_weights(g_mix, w_in, g_qa, g_kva, w_uq, w_ukv, w_gate, b_gate, w_pa, w_pb, w_out,
                  g_xq, g_mkv, w_mq, w_mkv, w_mo, g_ffn, w_pq, sub_keys, peer_u, peer_v, g_final):
    w_in = w_in[0]
    q_cols = DSA_QK_COLS // 2
    wqk = jnp.concatenate([w_in[:, :q_cols] * (DSA_HEAD_DIM ** -0.5), w_in[:, q_cols:DSA_QK_COLS]], axis=1)
    c0 = DSA_QK_COLS + DSA_V_COLS
    c1 = c0 + MLA_Q_LORA + MLA_KV_LORA
    wkpe = jnp.zeros((D_MODEL, MLA_SLAB), F32).at[:, MLA_NOPE:MLA_QK].set(w_in[:, c1:])
    pad_head = lambda a: jnp.pad(a, ((0, 0), (0, 0), (0, MLA_SLAB - a.shape[2])))
    wq = pad_head(w_uq[0].reshape(MLA_Q_LORA, MLA_HEADS, MLA_QK)).reshape(MLA_Q_LORA, -1)
    ukv = w_ukv[0].reshape(MLA_KV_LORA, MLA_HEADS, MLA_NOPE + MLA_V)
    wk = pad_head(ukv[:, :, :MLA_NOPE]).reshape(MLA_KV_LORA, -1)
    wv = pad_head(ukv[:, :, MLA_NOPE:]).reshape(MLA_KV_LORA, -1)
    wpb = jnp.pad(w_pb[0].reshape(MLA_HEADS, MLA_V, D_MODEL), ((0, 0), (0, MLA_SLAB - MLA_V), (0, 0)))
    bf = lambda a: a.astype(BF16)
    return {
        'g_mix': g_mix, 'wqk': bf(wqk), 'wv': bf(w_in[:, DSA_QK_COLS:c0]), 'wc': bf(w_in[:, c0:c1]),
        'wkpe': bf(wkpe), 'wgate': bf(w_gate[0]), 'b_gate': b_gate,
        'g_qa': g_qa, 'g_kva': g_kva, 'wq_mla': bf(wq), 'wk_mla': bf(wk), 'wv_mla': bf(wv),
        'wpa': bf(w_pa[0]), 'wpb': bf(wpb.reshape(-1, D_MODEL)), 'wout': bf(w_out[0]),
        'g_xq': g_xq, 'g_mkv': g_mkv, 'wmq': bf(w_mq[0]), 'wmkv': bf(w_mkv[0]), 'wmo': bf(w_mo[0]),
        'g_ffn': g_ffn, 'wpq': bf(w_pq[0]),
        'keys': bf(sub_keys[0]),
        'peer_u': bf(peer_u[0]), 'peer_vt': bf(peer_v[0]).T,
        'g_final': g_final.reshape(1, D_MODEL),
    }


def _rope_tables(seq):
    def table(rot, lead, period):
        inv = ROPE_THETA ** (-jnp.arange(0, rot, 2, dtype=F32) / rot)
        ang = jnp.arange(seq, dtype=F32)[:, None] * inv[None, :]
        cos, sin = jnp.cos(ang), jnp.sin(ang)
        ones = lambda n: jnp.ones((seq, n), F32)
        zeros = lambda n: jnp.zeros((seq, n), F32)
        tail = period - lead - rot
        c = jnp.concatenate([ones(lead), cos, cos, ones(tail)], axis=1)
        s = jnp.concatenate([zeros(lead), -sin, sin, zeros(tail)], axis=1)
        reps = LANES // period
        return jnp.tile(c, (1, reps)), jnp.tile(s, (1, reps))
    ca, sa = table(DSA_ROT, 0, DSA_HEAD_DIM)
    cb, sb = table(MLA_ROPE, MLA_NOPE, MLA_SLAB)
    return {'ca': ca, 'sa': sa, 'cb': cb, 'sb': sb}


def _trunk(x, mem, w):
    b, s, d = x.shape
    t = b * s
    tabs = _rope_tables(s)
    x2d = x.reshape(t, d)
    tm = 256
    qk, v, cq, ckv, kpe, gates = _inproj(x2d, s, w, tabs, tm)
    dsa_out = [_dsa_group(qk.reshape(b, s, -1), v.reshape(b, s, -1), g, dil, 256)
               for g, dil in enumerate(DSA_DILATIONS)]
    q_b, k_b, v_b = _mla_proj(cq, ckv, kpe, s, w, tabs, tm)
    wide = MLA_HEADS * MLA_SLAB
    ta = min(512, s)
    yb = _mla_attn(q_b.reshape(b, s, wide), k_b.reshape(b, s, wide), v_b.reshape(b, s, wide), ta, ta)
    x1 = _merge(dsa_out, yb.reshape(t, wide), gates, x2d, w, tm)
    n_mem = mem.shape[1]
    kv = _norm_mm(mem.reshape(b * n_mem, d), w['g_mkv'], w['wmkv'], n_mem)
    x2 = _xattn(x1.reshape(b, s, d), kv.reshape(b, n_mem, 2 * d), w, tm)
    x2 = x2.reshape(t, d)
    tt = 128
    hq, *stats = _peer_scores(x2, w, tt)
    y = _peer_mix(hq, stats, x2, w, tt, 8)
    return y.reshape(b, s, d)


def kernel(x_prompt, x_sample, mem_prompt, mem_sample, g_mix, w_in, g_qa, g_kva, w_uq, w_ukv, w_gate,
           b_gate, w_pa, w_pb, w_out, g_xq, g_mkv, w_mq, w_mkv, w_mo, g_ffn, w_pq, sub_keys, peer_u,
           peer_v, g_final):
    w = _prep_weights(g_mix, w_in, g_qa, g_kva, w_uq, w_ukv, w_gate, b_gate, w_pa, w_pb, w_out,
                      g_xq, g_mkv, w_mq, w_mkv, w_mo, g_ffn, w_pq, sub_keys, peer_u, peer_v, g_final)
    return (_trunk(x_prompt, mem_prompt, w), _trunk(x_sample, mem_sample, w))
```

```python
import functools

import jax
import jax.numpy as jnp
from jax import lax
from jax.experimental import pallas as pl
from jax.experimental.pallas import tpu as pltpu

F32 = jnp.float32
BF16 = jnp.bfloat16

D_MODEL = 1024
EPS = 1e-6
NEG_INF = -1e30
ROPE_THETA = 500000.0

DSA_DILATIONS = (1, 4, 16)
DSA_HALF = 64
DSA_HEADS = 4
DSA_HEAD_DIM = 64
DSA_ROT = 16
DSA_GROUP_COLS = DSA_HEADS * DSA_HEAD_DIM
DSA_QK_COLS = 2 * 3 * DSA_GROUP_COLS
DSA_V_COLS = 3 * DSA_GROUP_COLS

MLA_HEADS = 8
MLA_Q_LORA = 768
MLA_KV_LORA = 256
MLA_NOPE = 64
MLA_ROPE = 32
MLA_V = 64
MLA_QK = MLA_NOPE + MLA_ROPE
MLA_SLAB = 128
MLA_ONE_LANE = MLA_V
MLA_HEADS_PER_STEP = 1
MLA_KEY_CHUNK = 512
LOG2_E = 1.4426950408889634

MEM_HEADS = 4
MEM_HEAD_DIM = 256

PEER_HEADS = 8
PEER_NKEYS = 128
PEER_HALF = 128
PEER_TOPK = 16

LANES = 128
VMEM_LIMIT_BYTES = 56 * 1024 * 1024


def _params(sem):
    return pltpu.CompilerParams(dimension_semantics=sem, vmem_limit_bytes=VMEM_LIMIT_BYTES)


def _const_spec(shape):
    nd = len(shape)
    return pl.BlockSpec(shape, lambda *_: (0,) * nd)


def _rms(x, g):
    y = x * lax.rsqrt(jnp.mean(x * x, axis=-1, keepdims=True) + EPS)
    return y * g


def _dot(a, b):
    return jnp.dot(a, b, preferred_element_type=F32)


def _dot_nt(a, b):
    return lax.dot_general(a, b, (((1,), (1,)), ((), ())), preferred_element_type=F32)


def _rope_slab(x, cos_t, sin_t, take_up, half):
    up = pltpu.roll(x, LANES - half, 1)
    dn = pltpu.roll(x, half, 1)
    return x * cos_t + jnp.where(take_up, up, dn) * sin_t


def _inproj_kernel(x_ref, g_ref, wqk_ref, wv_ref, wc_ref, wkpe_ref, wg_ref, bg_ref,
                   ca_ref, sa_ref, cb_ref, sb_ref,
                   qk_ref, v_ref, cq_ref, ckv_ref, kpe_ref, gate_ref):
    h = _rms(x_ref[...], g_ref[...]).astype(BF16)
    rows = h.shape[0]
    lane = lax.broadcasted_iota(jnp.int32, (rows, LANES), 1)
    up_a = (lane % DSA_HEAD_DIM) < (DSA_ROT // 2)
    ca = ca_ref[...]
    sa = sa_ref[...]
    z = _dot(h, wqk_ref[...])
    for s in range(DSA_QK_COLS // LANES):
        sl = slice(s * LANES, (s + 1) * LANES)
        qk_ref[:, sl] = _rope_slab(z[:, sl], ca, sa, up_a, DSA_ROT // 2).astype(BF16)
    v_ref[...] = _dot(h, wv_ref[...]).astype(BF16)
    c = _dot(h, wc_ref[...])
    cq_ref[...] = c[:, :MLA_Q_LORA]
    ckv_ref[...] = c[:, MLA_Q_LORA:]
    up_b = (lane >= MLA_NOPE) & (lane < MLA_NOPE + MLA_ROPE // 2)
    kpe_ref[...] = _rope_slab(_dot(h, wkpe_ref[...]), cb_ref[...], sb_ref[...], up_b, MLA_ROPE // 2)
    gate_ref[...] = jax.nn.sigmoid(_dot(h, wg_ref[...]) + bg_ref[...])


def _inproj(x2d, seq, w, tabs, tm):
    t = x2d.shape[0]
    nseq = seq // tm
    row = lambda i: (i, 0)
    tab = lambda i: (i % nseq, 0)
    out_shape = (
        jax.ShapeDtypeStruct((t, DSA_QK_COLS), BF16),
        jax.ShapeDtypeStruct((t, DSA_V_COLS), BF16),
        jax.ShapeDtypeStruct((t, MLA_Q_LORA), F32),
        jax.ShapeDtypeStruct((t, MLA_KV_LORA), F32),
        jax.ShapeDtypeStruct((t, MLA_SLAB), F32),
        jax.ShapeDtypeStruct((t, 2 * D_MODEL), F32),
    )
    return pl.pallas_call(
        _inproj_kernel,
        grid=(t // tm,),
        in_specs=[
            pl.BlockSpec((tm, D_MODEL), row),
            _const_spec((1, D_MODEL)),
            _const_spec(w['wqk'].shape), _const_spec(w['wv'].shape), _const_spec(w['wc'].shape),
            _const_spec(w['wkpe'].shape), _const_spec(w['wgate'].shape), _const_spec((1, 2 * D_MODEL)),
            pl.BlockSpec((tm, LANES), tab), pl.BlockSpec((tm, LANES), tab),
            pl.BlockSpec((tm, LANES), tab), pl.BlockSpec((tm, LANES), tab),
        ],
        out_specs=[pl.BlockSpec((tm, s.shape[1]), row) for s in out_shape],
        out_shape=out_shape,
        compiler_params=_params(("parallel",)),
        name="inproj",
    )(x2d, w['g_mix'], w['wqk'], w['wv'], w['wc'], w['wkpe'], w['wgate'], w['b_gate'],
      tabs['ca'], tabs['sa'], tabs['cb'], tabs['sb'])


def _mla_proj_kernel(cq_ref, ckv_ref, kpe_ref, gqa_ref, gkva_ref, wq_ref, wk_ref, wv_ref,
                     cb_ref, sb_ref, q_ref, k_ref, v_ref):
    qn = _rms(cq_ref[...], gqa_ref[...]).astype(BF16)
    rows = qn.shape[0]
    lane = lax.broadcasted_iota(jnp.int32, (rows, LANES), 1)
    up_b = (lane >= MLA_NOPE) & (lane < MLA_NOPE + MLA_ROPE // 2)
    cb = cb_ref[...]
    sb = sb_ref[...]
    q = _dot(qn, wq_ref[...])
    kvn = _rms(ckv_ref[...], gkva_ref[...]).astype(BF16)
    k = _dot(kvn, wk_ref[...])
    kpe = kpe_ref[...]
    scale = MLA_QK ** -0.5 * LOG2_E
    for s in range(MLA_HEADS):
        sl = slice(s * LANES, (s + 1) * LANES)
        q_ref[:, sl] = (_rope_slab(q[:, sl], cb, sb, up_b, MLA_ROPE // 2) * scale).astype(BF16)
        k_ref[:, sl] = (k[:, sl] + kpe).astype(BF16)
    wide_lane = lax.broadcasted_iota(jnp.int32, (1, MLA_HEADS * MLA_SLAB), 1)
    one_lane = jnp.where(wide_lane % MLA_SLAB == MLA_ONE_LANE, 1.0, 0.0)
    v_ref[...] = (_dot(kvn, wv_ref[...]) + one_lane).astype(BF16)


def _mla_proj(cq, ckv, kpe, seq, w, tabs, tm):
    t = cq.shape[0]
    nseq = seq // tm
    row = lambda i: (i, 0)
    tab = lambda i: (i % nseq, 0)
    wide = MLA_HEADS * MLA_SLAB
    out_shape = tuple(jax.ShapeDtypeStruct((t, wide), BF16) for _ in range(3))
    return pl.pallas_call(
        _mla_proj_kernel,
        grid=(t // tm,),
        in_specs=[
            pl.BlockSpec((tm, MLA_Q_LORA), row), pl.BlockSpec((tm, MLA_KV_LORA), row),
            pl.BlockSpec((tm, MLA_SLAB), row),
            _const_spec((1, MLA_Q_LORA)), _const_spec((1, MLA_KV_LORA)),
            _const_spec(w['wq_mla'].shape), _const_spec(w['wk_mla'].shape), _const_spec(w['wv_mla'].shape),
            pl.BlockSpec((tm, LANES), tab), pl.BlockSpec((tm, LANES), tab),
        ],
        out_specs=[pl.BlockSpec((tm, wide), row) for _ in range(3)],
        out_shape=out_shape,
        compiler_params=_params(("parallel",)),
        name="mla_proj",
    )(cq, ckv, kpe, w['g_qa'], w['g_kva'], w['wq_mla'], w['wk_mla'], w['wv_mla'], tabs['cb'], tabs['sb'])


def _mla_attn_kernel(q_ref, k_ref, v_ref, o_ref, m_ref, acc_ref):
    j = pl.program_id(3)

    @pl.when(j == 0)
    def _():
        m_ref[...] = jnp.full(m_ref.shape, -jnp.inf, F32)
        acc_ref[...] = jnp.zeros(acc_ref.shape, F32)

    tk = k_ref.shape[0]
    for h in range(MLA_HEADS_PER_STEP):
        sl = slice(h * MLA_SLAB, (h + 1) * MLA_SLAB)
        q = q_ref[:, sl]
        m_run = m_ref[:, sl]
        acc = acc_ref[:, sl]
        for c in range(tk // MLA_KEY_CHUNK):
            rows = slice(c * MLA_KEY_CHUNK, (c + 1) * MLA_KEY_CHUNK)
            s = _dot_nt(q, k_ref[rows, sl])
            cols = [s[:, i * LANES:(i + 1) * LANES] for i in range(MLA_KEY_CHUNK // LANES)]
            m_loc = functools.reduce(jnp.maximum, cols)
            m_new = jnp.maximum(m_run, jnp.max(m_loc, axis=1, keepdims=True))
            p = jnp.concatenate([jnp.exp2(col - m_new) for col in cols], axis=1).astype(BF16)
            acc = jnp.exp2(m_run - m_new) * acc + _dot(p, v_ref[rows, sl])
            m_run = m_new
        acc_ref[:, sl] = acc
        m_ref[:, sl] = m_run

    @pl.when(j == pl.num_programs(3) - 1)
    def _():
        for h in range(MLA_HEADS_PER_STEP):
            sl = slice(h * MLA_SLAB, (h + 1) * MLA_SLAB)
            acc = acc_ref[:, sl]
            den = acc[:, MLA_ONE_LANE:MLA_ONE_LANE + 1]
            o_ref[:, sl] = (acc / den).astype(o_ref.dtype)


def _mla_attn(q, k, v, tq, tk):
    b, s, wide = q.shape
    cols = MLA_HEADS_PER_STEP * MLA_SLAB
    return pl.pallas_call(
        _mla_attn_kernel,
        grid=(b, MLA_HEADS // MLA_HEADS_PER_STEP, s // tq, s // tk),
        in_specs=[
            pl.BlockSpec((None, tq, cols), lambda bi, h, i, j: (bi, i, h)),
            pl.BlockSpec((None, tk, cols), lambda bi, h, i, j: (bi, j, h)),
            pl.BlockSpec((None, tk, cols), lambda bi, h, i, j: (bi, j, h)),
        ],
        out_specs=pl.BlockSpec((None, tq, cols), lambda bi, h, i, j: (bi, i, h)),
        out_shape=jax.ShapeDtypeStruct((b, s, wide), BF16),
        scratch_shapes=[pltpu.VMEM((tq, cols), F32), pltpu.VMEM((tq, cols), F32)],
        compiler_params=_params(("parallel", "parallel", "parallel", "arbitrary")),
        name="mla_attn",
    )(q, k, v)


def _dsa_kernel(q_ref, kp_ref, kc_ref, kn_ref, vp_ref, vc_ref, vn_ref, o_ref, lse_ref, *, cls_len):
    j = pl.program_id(2)
    tq = q_ref.shape[0]
    nk = tq + 2 * DSA_HALF
    q = q_ref[...]
    kk = jnp.concatenate([kp_ref[...], kc_ref[...], kn_ref[...]], axis=0)
    vv = jnp.concatenate([vp_ref[...], vc_ref[...], vn_ref[...]], axis=0)
    qpos = j * tq + lax.broadcasted_iota(jnp.int32, (tq, nk), 0)
    kpos = j * tq - DSA_HALF + lax.broadcasted_iota(jnp.int32, (tq, nk), 1)
    ok = (jnp.abs(kpos - qpos) <= DSA_HALF) & (kpos >= 0) & (kpos < cls_len)
    lane = lax.broadcasted_iota(jnp.int32, (tq, DSA_GROUP_COLS), 1)
    o = jnp.zeros((tq, DSA_GROUP_COLS), F32)
    lse = jnp.zeros((tq, DSA_GROUP_COLS), F32)
    for h in range(DSA_HEADS):
        in_head = (lane // DSA_HEAD_DIM) == h
        s = _dot_nt(jnp.where(in_head, q, jnp.zeros_like(q)), kk)
        s = jnp.where(ok, s, NEG_INF)
        m = jnp.max(s, axis=1, keepdims=True)
        p = jnp.exp(s - m)
        den = jnp.sum(p, axis=1, keepdims=True)
        oh = _dot((p / den).astype(BF16), vv)
        o = jnp.where(in_head, oh, o)
        lse = jnp.where(in_head, m + jnp.log(den), lse)
    o_ref[...] = o
    lse_ref[...] = lse


def _dsa_group(qk, v, g, dil, tq):
    b, s, _ = qk.shape
    cls_len = s // dil
    tq = min(tq, cls_len)
    qk_v = qk.reshape(b, cls_len, dil * DSA_QK_COLS)
    v_v = v.reshape(b, cls_len, dil * DSA_V_COLS)
    per = tq // DSA_HALF
    last = cls_len // DSA_HALF - 1
    qk_blocks = DSA_QK_COLS // DSA_GROUP_COLS
    v_blocks = DSA_V_COLS // DSA_GROUP_COLS
    qcol = lambda r: r * qk_blocks + g
    kcol = lambda r: r * qk_blocks + 3 + g
    vcol = lambda r: r * v_blocks + g
    prev = lambda j: jnp.maximum(j * per - 1, 0)
    nxt = lambda j: jnp.minimum((j + 1) * per, last)
    halo = (None, DSA_HALF, DSA_GROUP_COLS)
    cur = (None, tq, DSA_GROUP_COLS)
    out_sd = jax.ShapeDtypeStruct((b, cls_len, dil * DSA_GROUP_COLS), F32)
    o, lse = pl.pallas_call(
        functools.partial(_dsa_kernel, cls_len=cls_len),
        grid=(b, dil, cls_len // tq),
        in_specs=[
            pl.BlockSpec(cur, lambda bi, r, j: (bi, j, qcol(r))),
            pl.BlockSpec(halo, lambda bi, r, j: (bi, prev(j), kcol(r))),
            pl.BlockSpec(cur, lambda bi, r, j: (bi, j, kcol(r))),
            pl.BlockSpec(halo, lambda bi, r, j: (bi, nxt(j), kcol(r))),
            pl.BlockSpec(halo, lambda bi, r, j: (bi, prev(j), vcol(r))),
            pl.BlockSpec(cur, lambda bi, r, j: (bi, j, vcol(r))),
            pl.BlockSpec(halo, lambda bi, r, j: (bi, nxt(j), vcol(r))),
        ],
        out_specs=[pl.BlockSpec(cur, lambda bi, r, j: (bi, j, r)),
                   pl.BlockSpec(cur, lambda bi, r, j: (bi, j, r))],
        out_shape=(out_sd, out_sd),
        compiler_params=_params(("parallel", "parallel", "parallel")),
        name=f"dsa_g{g}",
    )(qk_v, qk_v, qk_v, qk_v, v_v, v_v, v_v)
    return o.reshape(b * s, DSA_GROUP_COLS), lse.reshape(b * s, DSA_GROUP_COLS)


def _merge_kernel(o0_ref, l0_ref, o1_ref, l1_ref, o2_ref, l2_ref, yb_ref, gate_ref, x_ref,
                  wpa_ref, wpb_ref, wout_ref, out_ref):
    l0, l1, l2 = l0_ref[...], l1_ref[...], l2_ref[...]
    m = jnp.maximum(jnp.maximum(l0, l1), l2)
    e0, e1, e2 = jnp.exp(l0 - m), jnp.exp(l1 - m), jnp.exp(l2 - m)
    den = e0 + e1 + e2
    ya = (e0 / den) * o0_ref[...] + (e1 / den) * o1_ref[...] + (e2 / den) * o2_ref[...]
    gate = gate_ref[...]
    pa = _dot(ya.astype(BF16), wpa_ref[...])
    pb = _dot(yb_ref[...], wpb_ref[...])
    merged = gate[:, :D_MODEL] * pa + gate[:, D_MODEL:] * pb
    out_ref[...] = x_ref[...] + _dot(merged.astype(BF16), wout_ref[...])


def _merge(dsa_out, yb, gates, x2d, w, tm):
    t = x2d.shape[0]
    row = lambda i: (i, 0)
    grp = pl.BlockSpec((tm, DSA_GROUP_COLS), row)
    flat = [a for pair in dsa_out for a in pair]
    return pl.pallas_call(
        _merge_kernel,
        grid=(t // tm,),
        in_specs=[grp] * 6 + [
            pl.BlockSpec((tm, MLA_HEADS * MLA_SLAB), row),
            pl.BlockSpec((tm, 2 * D_MODEL), row),
            pl.BlockSpec((tm, D_MODEL), row),
            _const_spec(w['wpa'].shape), _const_spec(w['wpb'].shape), _const_spec(w['wout'].shape),
        ],
        out_specs=pl.BlockSpec((tm, D_MODEL), row),
        out_shape=jax.ShapeDtypeStruct((t, D_MODEL), F32),
        compiler_params=_params(("parallel",)),
        name="merge",
    )(*flat, yb, gates, x2d, w['wpa'], w['wpb'], w['wout'])


def _norm_mm_kernel(x_ref, g_ref, w_ref, o_ref):
    o_ref[...] = _dot(_rms(x_ref[...], g_ref[...]).astype(BF16), w_ref[...]).astype(o_ref.dtype)


def _norm_mm(x2d, g, wmat, tm):
    t, d = x2d.shape
    n = wmat.shape[1]
    return pl.pallas_call(
        _norm_mm_kernel,
        grid=(t // tm,),
        in_specs=[pl.BlockSpec((tm, d), lambda i: (i, 0)), _const_spec((1, d)), _const_spec(wmat.shape)],
        out_specs=pl.BlockSpec((tm, n), lambda i: (i, 0)),
        out_shape=jax.ShapeDtypeStruct((t, n), BF16),
        compiler_params=_params(("parallel",)),
        name="mem_kv",
    )(x2d, g, wmat)


def _xattn_kernel(x_ref, g_ref, kv_ref, wq_ref, wo_ref, out_ref):
    x = x_ref[...]
    hq = _rms(x, g_ref[...]).astype(BF16)
    q = (_dot(hq, wq_ref[...]) * (MEM_HEAD_DIM ** -0.5)).astype(BF16)
    kv = kv_ref[...]
    outs = []
    for h in range(MEM_HEADS):
        sl = slice(h * MEM_HEAD_DIM, (h + 1) * MEM_HEAD_DIM)
        vsl = slice(D_MODEL + h * MEM_HEAD_DIM, D_MODEL + (h + 1) * MEM_HEAD_DIM)
        s = _dot_nt(q[:, sl], kv[:, sl])
        m = jnp.max(s, axis=1, keepdims=True)
        p = jnp.exp(s - m)
        den = jnp.sum(p, axis=1, keepdims=True)
        outs.append(_dot((p / den).astype(BF16), kv[:, vsl]).astype(BF16))
    o = jnp.concatenate(outs, axis=1)
    out_ref[...] = x + _dot(o, wo_ref[...])


def _xattn(x3d, kv3d, w, tm):
    b, s, d = x3d.shape
    n_mem = kv3d.shape[1]
    return pl.pallas_call(
        _xattn_kernel,
        grid=(b, s // tm),
        in_specs=[
            pl.BlockSpec((None, tm, d), lambda bi, i: (bi, i, 0)),
            _const_spec((1, d)),
            pl.BlockSpec((None, n_mem, 2 * d), lambda bi, i: (bi, 0, 0)),
            _const_spec(w['wmq'].shape), _const_spec(w['wmo'].shape),
        ],
        out_specs=pl.BlockSpec((None, tm, d), lambda bi, i: (bi, i, 0)),
        out_shape=jax.ShapeDtypeStruct((b, s, d), F32),
        compiler_params=_params(("parallel", "parallel")),
        name="xattn",
    )(x3d, w['g_xq'], kv3d, w['wmq'], w['wmo'])


def _top_values(sc, count, with_rank=False):
    n = sc.shape[0]
    row = lax.broadcasted_iota(jnp.int32, sc.shape, 0)
    rank = jnp.full(sc.shape, float(n - 1), F32)
    vals = []
    for k in range(count):
        m = jnp.max(sc, axis=0, keepdims=True)
        first = jnp.min(jnp.where(sc == m, row, n), axis=0, keepdims=True)
        hit = row == first
        vals.append(m)
        if with_rank:
            rank = jnp.where(hit, float(k), rank)
        sc = jnp.where(hit, -jnp.inf, sc)
    vals = jnp.concatenate(vals, axis=0)
    return (vals, rank) if with_rank else vals


def _pack_bf16_rows(x):
    return pltpu.bitcast(x.astype(BF16), jnp.int32)


def _unpack_bf16_rows(words):
    return pltpu.bitcast(words, BF16)


def _bf16_pair_word(x):
    bits = lax.bitcast_convert_type(x.astype(BF16).astype(F32), jnp.uint32)
    return lax.bitcast_convert_type(bits | (bits >> 16), jnp.int32)


def _peer_score_kernel(x_ref, g_ref, wpq_ref, keys_ref, hqt_ref, r2_ref, cnt_ref, e1_ref, e2_ref):
    hq32 = _rms(x_ref[...], g_ref[...])

    @pl.when(pl.program_id(1) == 0)
    def _():
        hqt_ref[...] = _pack_bf16_rows(hq32.T)

    q = _dot(hq32.astype(BF16), wpq_ref[...]).astype(BF16)
    s1 = _dot_nt(keys_ref[0], q[:, :PEER_HALF])
    s2 = _dot_nt(keys_ref[1], q[:, PEER_HALF:])
    v1 = _top_values(s1, PEER_TOPK)
    v2, r2 = _top_values(s2, PEER_TOPK, with_rank=True)
    cand = [v1[0:1] + v2]
    cand += [v1[i:i + 1] + v2[0:8] for i in range(1, 8)]
    cand += [v1[8:16] + v2[0:1]]
    top = _top_values(jnp.concatenate(cand, axis=0), PEER_TOPK)
    z = jnp.sum(jnp.exp(top - top[0:1]), axis=0, keepdims=True)
    tau = top[PEER_TOPK - 1:PEER_TOPK]
    cnt = jnp.zeros(s1.shape, F32)
    for r in range(PEER_TOPK):
        cnt = cnt + jnp.where(s1 + v2[r:r + 1] >= tau, 1.0, 0.0)
    r2_ref[...] = _pack_bf16_rows(r2)
    cnt_ref[...] = _bf16_pair_word(cnt)
    e1_ref[...] = _bf16_pair_word(jnp.exp(s1 - v1[0:1]) / z)
    e2_ref[...] = _pack_bf16_rows(jnp.exp(s2 - v2[0:1]))


def _peer_scores(x2d, w, tt):
    t = x2d.shape[0]
    words = lambda rows: jax.ShapeDtypeStruct((PEER_HEADS, rows, t), jnp.int32)
    spec = lambda rows: pl.BlockSpec((None, rows, tt), lambda i, h: (h, 0, i))
    half = PEER_NKEYS // 2
    return pl.pallas_call(
        _peer_score_kernel,
        grid=(t // tt, PEER_HEADS),
        in_specs=[
            pl.BlockSpec((tt, D_MODEL), lambda i, h: (i, 0)),
            _const_spec((1, D_MODEL)),
            pl.BlockSpec((D_MODEL, 2 * PEER_HALF), lambda i, h: (0, h)),
            pl.BlockSpec((None, 2, PEER_NKEYS, PEER_HALF), lambda i, h: (h, 0, 0, 0)),
        ],
        out_specs=[pl.BlockSpec((D_MODEL // 2, tt), lambda i, h: (0, i)),
                   spec(half), spec(PEER_NKEYS), spec(PEER_NKEYS), spec(half)],
        out_shape=(jax.ShapeDtypeStruct((D_MODEL // 2, t), jnp.int32),
                   words(half), words(PEER_NKEYS), words(PEER_NKEYS), words(half)),
        compiler_params=_params(("parallel", "arbitrary")),
        name="peer_scores",
    )(x2d, w['g_ffn'], w['wpq'], w['keys'])


def _peer_mix_kernel(hqt_ref, u_ref, vt_ref, r2_ref, cnt_ref, e1_ref, e2_ref, x_ref, gf_ref,
                     out_ref, acc_ref, wg_ref, *, blocks):
    j = pl.program_id(1)
    last = pl.num_programs(1) - 1
    tt = hqt_ref.shape[1]
    pair_rows = 2 * PEER_NKEYS

    @pl.when(j == 0)
    def _():
        acc_ref[...] = jnp.zeros(acc_ref.shape, F32)
        wg_ref[...] = jnp.zeros(wg_ref.shape, jnp.int32)

    def mix_previous_block():
        acc_ref[...] += _dot(_unpack_bf16_rows(vt_ref[...]), _unpack_bf16_rows(wg_ref[...]))

    def row_bcast(words):
        return _unpack_bf16_rows(jnp.broadcast_to(words, (PEER_NKEYS // 2, LANES)))

    @pl.when(j < last)
    def _():
        mix_previous_block()
        hqt = _unpack_bf16_rows(hqt_ref[...])
        half_rows = PEER_NKEYS // 2
        for c in range(blocks // 2):
            u_pair = _unpack_bf16_rows(u_ref[c * PEER_NKEYS:(c + 1) * PEER_NKEYS, :])
            pre2 = _dot(u_pair, hqt)
            for half in range(2):
                i = 2 * c + half
                a = j * blocks + i
                cnt_rows = [cnt_ref[h, pl.ds(a, 1), :] for h in range(PEER_HEADS)]
                e1_rows = [e1_ref[h, pl.ds(a, 1), :] for h in range(PEER_HEADS)]
                for col in range(tt // LANES):
                    cs = slice(col * LANES, (col + 1) * LANES)
                    wsum = jnp.zeros((PEER_NKEYS, LANES), BF16)
                    for h in range(PEER_HEADS):
                        chosen = _unpack_bf16_rows(r2_ref[h, :, cs]) < row_bcast(cnt_rows[h][:, cs])
                        picked = jnp.where(chosen, _unpack_bf16_rows(e2_ref[h, :, cs]), jnp.zeros((), BF16))
                        wsum = wsum + picked * row_bcast(e1_rows[h][:, cs])
                    pre = pre2[half * PEER_NKEYS:(half + 1) * PEER_NKEYS, cs]
                    act = 0.5 * pre * (1.0 + lax.erf(pre * (0.5 ** 0.5)))
                    wg_ref[i * half_rows:(i + 1) * half_rows, cs] = pltpu.bitcast(wsum * act.astype(BF16), jnp.int32)

    @pl.when(j == last)
    def _():
        mix_previous_block()
        out_ref[...] = _rms(x_ref[...] + acc_ref[...].T, gf_ref[...])


def _peer_mix(hqt, stats, x2d, w, tt, blocks):
    t = x2d.shape[0]
    ne = blocks * PEER_NKEYS
    nblk = PEER_NKEYS // blocks
    spec = lambda rows: pl.BlockSpec((PEER_HEADS, rows, tt), lambda i, j: (0, 0, i))
    half = PEER_NKEYS // 2
    return pl.pallas_call(
        functools.partial(_peer_mix_kernel, blocks=blocks),
        grid=(t // tt, nblk + 1),
        in_specs=[
            pl.BlockSpec((D_MODEL // 2, tt), lambda i, j: (0, i)),
            pl.BlockSpec((ne // 2, D_MODEL), lambda i, j: (jnp.minimum(j, nblk - 1), 0)),
            pl.BlockSpec((D_MODEL // 2, ne), lambda i, j: (0, jnp.maximum(j - 1, 0))),
            spec(half), spec(PEER_NKEYS), spec(PEER_NKEYS), spec(half),
            pl.BlockSpec((tt, D_MODEL), lambda i, j: (i, 0)),
            _const_spec((1, D_MODEL)),
        ],
        out_specs=pl.BlockSpec((tt, D_MODEL), lambda i, j: (i, 0)),
        out_shape=jax.ShapeDtypeStruct((t, D_MODEL), F32),
        scratch_shapes=[pltpu.VMEM((D_MODEL, tt), F32), pltpu.VMEM((ne // 2, tt), jnp.int32)],
        compiler_params=_params(("parallel", "arbitrary")),
        name="peer_mix",
    )(hqt, w['peer_u'], w['peer_vt'], *stats, x2d, w['g_final'])


def _pack_rows_host(a):
    m2, n = a.shape
    return lax.bitcast_convert_type(jnp.swapaxes(a.reshape(m2 // 2, 2, n), 1, 2), jnp.int32)


def _prep_weights(g_mix, w_in, g_qa, g_kva, w_uq, w_ukv, w_gate, b_gate, w_pa, w_pb, w_out,
                  g_xq, g_mkv, w_mq, w_mkv, w_mo, g_ffn, w_pq, sub_keys, peer_u, peer_v, g_final):
    w_in = w_in[0]
    q_cols = DSA_QK_COLS // 2
    wqk = jnp.concatenate([w_in[:, :q_cols] * (DSA_HEAD_DIM ** -0.5), w_in[:, q_cols:DSA_QK_COLS]], axis=1)
    c0 = DSA_QK_COLS + DSA_V_COLS
    c1 = c0 + MLA_Q_LORA + MLA_KV_LORA
    wkpe = jnp.zeros((D_MODEL, MLA_SLAB), F32).at[:, MLA_NOPE:MLA_QK].set(w_in[:, c1:])
    pad_head = lambda a: jnp.pad(a, ((0, 0), (0, 0), (0, MLA_SLAB - a.shape[2])))
    wq = pad_head(w_uq[0].reshape(MLA_Q_LORA, MLA_HEADS, MLA_QK)).reshape(MLA_Q_LORA, -1)
    ukv = w_ukv[0].reshape(MLA_KV_LORA, MLA_HEADS, MLA_NOPE + MLA_V)
    wk = pad_head(ukv[:, :, :MLA_NOPE]).reshape(MLA_KV_LORA, -1)
    wv = pad_head(ukv[:, :, MLA_NOPE:]).reshape(MLA_KV_LORA, -1)
    wpb = jnp.pad(w_pb[0].reshape(MLA_HEADS, MLA_V, D_MODEL), ((0, 0), (0, MLA_SLAB - MLA_V), (0, 0)))
    bf = lambda a: a.astype(BF16)
    return {
        'g_mix': g_mix, 'wqk': bf(wqk), 'wv': bf(w_in[:, DSA_QK_COLS:c0]), 'wc': bf(w_in[:, c0:c1]),
        'wkpe': bf(wkpe), 'wgate': bf(w_gate[0]), 'b_gate': b_gate,
        'g_qa': g_qa, 'g_kva': g_kva, 'wq_mla': bf(wq), 'wk_mla': bf(wk), 'wv_mla': bf(wv),
        'wpa': bf(w_pa[0]), 'wpb': bf(wpb.reshape(-1, D_MODEL)), 'wout': bf(w_out[0]),
        'g_xq': g_xq, 'g_mkv': g_mkv, 'wmq': bf(w_mq[0]), 'wmkv': bf(w_mkv[0]), 'wmo': bf(w_mo[0]),
        'g_ffn': g_ffn, 'wpq': bf(w_pq[0]),
        'keys': bf(sub_keys[0]),
        'peer_u': _pack_rows_host(bf(peer_u[0])), 'peer_vt': _pack_rows_host(bf(peer_v[0]).T),
        'g_final': g_final.reshape(1, D_MODEL),
    }


def _rope_tables(seq):
    def table(rot, lead, period):
        inv = ROPE_THETA ** (-jnp.arange(0, rot, 2, dtype=F32) / rot)
        ang = jnp.arange(seq, dtype=F32)[:, None] * inv[None, :]
        cos, sin = jnp.cos(ang), jnp.sin(ang)
        ones = lambda n: jnp.ones((seq, n), F32)
        zeros = lambda n: jnp.zeros((seq, n), F32)
        tail = period - lead - rot
        c = jnp.concatenate([ones(lead), cos, cos, ones(tail)], axis=1)
        s = jnp.concatenate([zeros(lead), -sin, sin, zeros(tail)], axis=1)
        reps = LANES // period
        return jnp.tile(c, (1, reps)), jnp.tile(s, (1, reps))
    ca, sa = table(DSA_ROT, 0, DSA_HEAD_DIM)
    cb, sb = table(MLA_ROPE, MLA_NOPE, MLA_SLAB)
    return {'ca': ca, 'sa': sa, 'cb': cb, 'sb': sb}


def _trunk(x, mem, w):
    b, s, d = x.shape
    t = b * s
    tabs = _rope_tables(s)
    x2d = x.reshape(t, d)
    tm = 256
    qk, v, cq, ckv, kpe, gates = _inproj(x2d, s, w, tabs, tm)
    dsa_out = [_dsa_group(qk.reshape(b, s, -1), v.reshape(b, s, -1), g, dil, 256)
               for g, dil in enumerate(DSA_DILATIONS)]
    q_b, k_b, v_b = _mla_proj(cq, ckv, kpe, s, w, tabs, tm)
    wide = MLA_HEADS * MLA_SLAB
    yb = _mla_attn(q_b.reshape(b, s, wide), k_b.reshape(b, s, wide), v_b.reshape(b, s, wide),
                   min(512, s), min(2048, s))
    x1 = _merge(dsa_out, yb.reshape(t, wide), gates, x2d, w, tm)
    n_mem = mem.shape[1]
    kv = _norm_mm(mem.reshape(b * n_mem, d), w['g_mkv'], w['wmkv'], n_mem)
    x2 = _xattn(x1.reshape(b, s, d), kv.reshape(b, n_mem, 2 * d), w, tm)
    x2 = x2.reshape(t, d)
    hqt, *stats = _peer_scores(x2, w, 128)
    y = _peer_mix(hqt, stats, x2, w, 256, 8)
    return y.reshape(b, s, d)


def kernel(x_prompt, x_sample, mem_prompt, mem_sample, g_mix, w_in, g_qa, g_kva, w_uq, w_ukv, w_gate,
           b_gate, w_pa, w_pb, w_out, g_xq, g_mkv, w_mq, w_mkv, w_mo, g_ffn, w_pq, sub_keys, peer_u,
           peer_v, g_final):
    w = _prep_weights(g_mix, w_in, g_qa, g_kva, w_uq, w_ukv, w_gate, b_gate, w_pa, w_pb, w_out,
                      g_xq, g_mkv, w_mq, w_mkv, w_mo, g_ffn, w_pq, sub_keys, peer_u, peer_v, g_final)
    return (_trunk(x_prompt, mem_prompt, w), _trunk(x_sample, mem_sample, w))
```

```python
import functools

import jax
import jax.numpy as jnp
from jax import lax
from jax.experimental import pallas as pl
from jax.experimental.pallas import tpu as pltpu

F32 = jnp.float32
BF16 = jnp.bfloat16

D_MODEL = 1024
EPS = 1e-6
NEG_INF = -1e30
ROPE_THETA = 500000.0

DSA_DILATIONS = (1, 4, 16)
DSA_HALF = 64
DSA_HEADS = 4
DSA_HEAD_DIM = 64
DSA_ROT = 16
DSA_GROUP_COLS = DSA_HEADS * DSA_HEAD_DIM
DSA_QK_COLS = 2 * 3 * DSA_GROUP_COLS
DSA_V_COLS = 3 * DSA_GROUP_COLS

MLA_HEADS = 8
MLA_Q_LORA = 768
MLA_KV_LORA = 256
MLA_NOPE = 64
MLA_ROPE = 32
MLA_V = 64
MLA_QK = MLA_NOPE + MLA_ROPE
MLA_SLAB = 128
MLA_ONE_LANE = MLA_V
MLA_HEADS_PER_STEP = 1
MLA_KEY_CHUNK = 512
LOG2_E = 1.4426950408889634

MEM_HEADS = 4
MEM_HEAD_DIM = 256

PEER_HEADS = 8
PEER_NKEYS = 128
PEER_HALF = 128
PEER_TOPK = 16
PEER_TOKENS = 256

LANES = 128
VMEM_LIMIT_BYTES = 56 * 1024 * 1024


def _params(sem):
    return pltpu.CompilerParams(dimension_semantics=sem, vmem_limit_bytes=VMEM_LIMIT_BYTES)


def _const_spec(shape):
    nd = len(shape)
    return pl.BlockSpec(shape, lambda *_: (0,) * nd)


def _rms(x, g):
    y = x * lax.rsqrt(jnp.mean(x * x, axis=-1, keepdims=True) + EPS)
    return y * g


def _dot(a, b):
    return jnp.dot(a, b, preferred_element_type=F32)


def _dot_nt(a, b):
    return lax.dot_general(a, b, (((1,), (1,)), ((), ())), preferred_element_type=F32)


def _rope_slab(x, cos_t, sin_t, take_up, half):
    up = pltpu.roll(x, LANES - half, 1)
    dn = pltpu.roll(x, half, 1)
    return x * cos_t + jnp.where(take_up, up, dn) * sin_t


def _inproj_kernel(x_ref, g_ref, wqk_ref, wv_ref, wc_ref, wkpe_ref, wg_ref, bg_ref,
                   ca_ref, sa_ref, cb_ref, sb_ref, *rest):
    dsa_refs = rest[:9]
    cq_ref, ckv_ref, kpe_ref, gate_ref, split_ref = rest[9:]
    h = _rms(x_ref[...], g_ref[...]).astype(BF16)
    rows = h.shape[0]
    lane = lax.broadcasted_iota(jnp.int32, (rows, LANES), 1)
    up_a = (lane % DSA_HEAD_DIM) < (DSA_ROT // 2)
    ca = ca_ref[...]
    sa = sa_ref[...]
    zqk = _dot(h, wqk_ref[...])
    zv = _dot(h, wv_ref[...])
    slot = 0
    for g, dil in enumerate(DSA_DILATIONS):
        for part in range(3):
            if part < 2:
                first = (part * DSA_QK_COLS // 2 + g * DSA_GROUP_COLS) // LANES
                val = jnp.concatenate(
                    [_rope_slab(zqk[:, s * LANES:(s + 1) * LANES], ca, sa, up_a, DSA_ROT // 2)
                     for s in range(first, first + DSA_GROUP_COLS // LANES)], axis=1)
            else:
                val = zv[:, g * DSA_GROUP_COLS:(g + 1) * DSA_GROUP_COLS]
            out_ref = dsa_refs[3 * g + part]
            if dil == 1:
                out_ref[0] = val.astype(BF16)
            else:
                for c in range(DSA_GROUP_COLS // LANES):
                    cs = slice(c * LANES, (c + 1) * LANES)
                    split_ref[slot, c] = val[:, cs]
                    for r in range(dil):
                        out_ref[r, :, cs] = split_ref[slot, c, pl.ds(r, rows // dil, stride=dil), :].astype(BF16)
                slot += 1
    c = _dot(h, wc_ref[...])
    cq_ref[...] = c[:, :MLA_Q_LORA]
    ckv_ref[...] = c[:, MLA_Q_LORA:]
    up_b = (lane >= MLA_NOPE) & (lane < MLA_NOPE + MLA_ROPE // 2)
    kpe_ref[...] = _rope_slab(_dot(h, wkpe_ref[...]), cb_ref[...], sb_ref[...], up_b, MLA_ROPE // 2)
    gate_ref[...] = jax.nn.sigmoid(_dot(h, wg_ref[...]) + bg_ref[...])


def _inproj(x2d, batch, seq, w, tabs, tm):
    t = x2d.shape[0]
    nseq = seq // tm
    row = lambda i: (i, 0)
    tab = lambda i: (i % nseq, 0)
    dsa_shapes, dsa_specs = [], []
    for dil in DSA_DILATIONS:
        for _ in range(3):
            dsa_shapes.append(jax.ShapeDtypeStruct((batch, dil, seq // dil, DSA_GROUP_COLS), BF16))
            dsa_specs.append(pl.BlockSpec((None, dil, tm // dil, DSA_GROUP_COLS),
                                          lambda i: (i // nseq, 0, i % nseq, 0)))
    flat_shapes = (
        jax.ShapeDtypeStruct((t, MLA_Q_LORA), F32),
        jax.ShapeDtypeStruct((t, MLA_KV_LORA), F32),
        jax.ShapeDtypeStruct((t, MLA_SLAB), F32),
        jax.ShapeDtypeStruct((t, 2 * D_MODEL), F32),
    )
    split_slots = 3 * sum(1 for dil in DSA_DILATIONS if dil > 1)
    return pl.pallas_call(
        _inproj_kernel,
        grid=(t // tm,),
        in_specs=[
            pl.BlockSpec((tm, D_MODEL), row),
            _const_spec((1, D_MODEL)),
            _const_spec(w['wqk'].shape), _const_spec(w['wv'].shape), _const_spec(w['wc'].shape),
            _const_spec(w['wkpe'].shape), _const_spec(w['wgate'].shape), _const_spec((1, 2 * D_MODEL)),
            pl.BlockSpec((tm, LANES), tab), pl.BlockSpec((tm, LANES), tab),
            pl.BlockSpec((tm, LANES), tab), pl.BlockSpec((tm, LANES), tab),
        ],
        out_specs=dsa_specs + [pl.BlockSpec((tm, s.shape[1]), row) for s in flat_shapes],
        out_shape=tuple(dsa_shapes) + flat_shapes,
        scratch_shapes=[pltpu.VMEM((split_slots, DSA_GROUP_COLS // LANES, tm, LANES), F32)],
        compiler_params=_params(("parallel",)),
        name="inproj",
    )(x2d, w['g_mix'], w['wqk'], w['wv'], w['wc'], w['wkpe'], w['wgate'], w['b_gate'],
      tabs['ca'], tabs['sa'], tabs['cb'], tabs['sb'])


def _mla_proj_kernel(cq_ref, ckv_ref, kpe_ref, gqa_ref, gkva_ref, wq_ref, wk_ref, wv_ref,
                     cb_ref, sb_ref, q_ref, k_ref, v_ref):
    qn = _rms(cq_ref[...], gqa_ref[...]).astype(BF16)
    rows = qn.shape[0]
    lane = lax.broadcasted_iota(jnp.int32, (rows, LANES), 1)
    up_b = (lane >= MLA_NOPE) & (lane < MLA_NOPE + MLA_ROPE // 2)
    cb = cb_ref[...]
    sb = sb_ref[...]
    q = _dot(qn, wq_ref[...])
    kvn = _rms(ckv_ref[...], gkva_ref[...]).astype(BF16)
    k = _dot(kvn, wk_ref[...])
    kpe = kpe_ref[...]
    scale = MLA_QK ** -0.5 * LOG2_E
    for s in range(MLA_HEADS):
        sl = slice(s * LANES, (s + 1) * LANES)
        q_ref[:, sl] = (_rope_slab(q[:, sl], cb, sb, up_b, MLA_ROPE // 2) * scale).astype(BF16)
        k_ref[:, sl] = (k[:, sl] + kpe).astype(BF16)
    wide_lane = lax.broadcasted_iota(jnp.int32, (1, MLA_HEADS * MLA_SLAB), 1)
    one_lane = jnp.where(wide_lane % MLA_SLAB == MLA_ONE_LANE, 1.0, 0.0)
    v_ref[...] = (_dot(kvn, wv_ref[...]) + one_lane).astype(BF16)


def _mla_proj(cq, ckv, kpe, seq, w, tabs, tm):
    t = cq.shape[0]
    nseq = seq // tm
    row = lambda i: (i, 0)
    tab = lambda i: (i % nseq, 0)
    wide = MLA_HEADS * MLA_SLAB
    out_shape = tuple(jax.ShapeDtypeStruct((t, wide), BF16) for _ in range(3))
    return pl.pallas_call(
        _mla_proj_kernel,
        grid=(t // tm,),
        in_specs=[
            pl.BlockSpec((tm, MLA_Q_LORA), row), pl.BlockSpec((tm, MLA_KV_LORA), row),
            pl.BlockSpec((tm, MLA_SLAB), row),
            _const_spec((1, MLA_Q_LORA)), _const_spec((1, MLA_KV_LORA)),
            _const_spec(w['wq_mla'].shape), _const_spec(w['wk_mla'].shape), _const_spec(w['wv_mla'].shape),
            pl.BlockSpec((tm, LANES), tab), pl.BlockSpec((tm, LANES), tab),
        ],
        out_specs=[pl.BlockSpec((tm, wide), row) for _ in range(3)],
        out_shape=out_shape,
        compiler_params=_params(("parallel",)),
        name="mla_proj",
    )(cq, ckv, kpe, w['g_qa'], w['g_kva'], w['wq_mla'], w['wk_mla'], w['wv_mla'], tabs['cb'], tabs['sb'])


def _mla_attn_kernel(q_ref, k_ref, v_ref, o_ref, m_ref, acc_ref):
    j = pl.program_id(3)

    @pl.when(j == 0)
    def _():
        m_ref[...] = jnp.full(m_ref.shape, -jnp.inf, F32)
        acc_ref[...] = jnp.zeros(acc_ref.shape, F32)

    tk = k_ref.shape[0]
    for h in range(MLA_HEADS_PER_STEP):
        sl = slice(h * MLA_SLAB, (h + 1) * MLA_SLAB)
        q = q_ref[:, sl]
        m_run = m_ref[:, sl]
        acc = acc_ref[:, sl]
        for c in range(tk // MLA_KEY_CHUNK):
            rows = slice(c * MLA_KEY_CHUNK, (c + 1) * MLA_KEY_CHUNK)
            s = _dot_nt(q, k_ref[rows, sl])
            cols = [s[:, i * LANES:(i + 1) * LANES] for i in range(MLA_KEY_CHUNK // LANES)]
            m_loc = functools.reduce(jnp.maximum, cols)
            m_new = jnp.maximum(m_run, jnp.max(m_loc, axis=1, keepdims=True))
            p = jnp.concatenate([jnp.exp2(col - m_new) for col in cols], axis=1).astype(BF16)
            acc = jnp.exp2(m_run - m_new) * acc + _dot(p, v_ref[rows, sl])
            m_run = m_new
        acc_ref[:, sl] = acc
        m_ref[:, sl] = m_run

    @pl.when(j == pl.num_programs(3) - 1)
    def _():
        for h in range(MLA_HEADS_PER_STEP):
            sl = slice(h * MLA_SLAB, (h + 1) * MLA_SLAB)
            acc = acc_ref[:, sl]
            den = acc[:, MLA_ONE_LANE:MLA_ONE_LANE + 1]
            o_ref[:, sl] = (acc / den).astype(o_ref.dtype)


def _mla_attn(q, k, v, tq, tk):
    b, s, wide = q.shape
    cols = MLA_HEADS_PER_STEP * MLA_SLAB
    return pl.pallas_call(
        _mla_attn_kernel,
        grid=(b, MLA_HEADS // MLA_HEADS_PER_STEP, s // tq, s // tk),
        in_specs=[
            pl.BlockSpec((None, tq, cols), lambda bi, h, i, j: (bi, i, h)),
            pl.BlockSpec((None, tk, cols), lambda bi, h, i, j: (bi, j, h)),
            pl.BlockSpec((None, tk, cols), lambda bi, h, i, j: (bi, j, h)),
        ],
        out_specs=pl.BlockSpec((None, tq, cols), lambda bi, h, i, j: (bi, i, h)),
        out_shape=jax.ShapeDtypeStruct((b, s, wide), BF16),
        scratch_shapes=[pltpu.VMEM((tq, cols), F32), pltpu.VMEM((tq, cols), F32)],
        compiler_params=_params(("parallel", "parallel", "parallel", "arbitrary")),
        name="mla_attn",
    )(q, k, v)


def _dsa_kernel(q_ref, kp_ref, kc_ref, kn_ref, vp_ref, vc_ref, vn_ref, o_ref, lse_ref, *, cls_len):
    j = pl.program_id(2)
    tq = q_ref.shape[0]
    nk = tq + 2 * DSA_HALF
    q = q_ref[...]
    kk = jnp.concatenate([kp_ref[...], kc_ref[...], kn_ref[...]], axis=0)
    vv = jnp.concatenate([vp_ref[...], vc_ref[...], vn_ref[...]], axis=0)
    qpos = j * tq + lax.broadcasted_iota(jnp.int32, (tq, nk), 0)
    kpos = j * tq - DSA_HALF + lax.broadcasted_iota(jnp.int32, (tq, nk), 1)
    ok = (jnp.abs(kpos - qpos) <= DSA_HALF) & (kpos >= 0) & (kpos < cls_len)
    lane = lax.broadcasted_iota(jnp.int32, (tq, DSA_GROUP_COLS), 1)
    o = jnp.zeros((tq, DSA_GROUP_COLS), F32)
    lse = jnp.zeros((tq, DSA_GROUP_COLS), F32)
    for h in range(DSA_HEADS):
        in_head = (lane // DSA_HEAD_DIM) == h
        s = _dot_nt(jnp.where(in_head, q, jnp.zeros_like(q)), kk)
        s = jnp.where(ok, s, NEG_INF)
        m = jnp.max(s, axis=1, keepdims=True)
        p = jnp.exp(s - m)
        den = jnp.sum(p, axis=1, keepdims=True)
        oh = _dot((p / den).astype(BF16), vv)
        o = jnp.where(in_head, oh, o)
        lse = jnp.where(in_head, m + jnp.log(den), lse)
    o_ref[...] = o
    lse_ref[...] = lse


def _dsa_group(q, k, v, g, tq):
    b, dil, cls_len, _ = q.shape
    tq = min(tq, cls_len)
    per = tq // DSA_HALF
    last = cls_len // DSA_HALF - 1
    prev = lambda j: jnp.maximum(j * per - 1, 0)
    nxt = lambda j: jnp.minimum((j + 1) * per, last)
    halo = (None, None, DSA_HALF, DSA_GROUP_COLS)
    cur = (None, None, tq, DSA_GROUP_COLS)
    cur_spec = pl.BlockSpec(cur, lambda bi, r, j: (bi, r, j, 0))
    prev_spec = pl.BlockSpec(halo, lambda bi, r, j: (bi, r, prev(j), 0))
    next_spec = pl.BlockSpec(halo, lambda bi, r, j: (bi, r, nxt(j), 0))
    out_sd = jax.ShapeDtypeStruct((b, dil, cls_len, DSA_GROUP_COLS), F32)
    return pl.pallas_call(
        functools.partial(_dsa_kernel, cls_len=cls_len),
        grid=(b, dil, cls_len // tq),
        in_specs=[cur_spec, prev_spec, cur_spec, next_spec, prev_spec, cur_spec, next_spec],
        out_specs=[cur_spec, cur_spec],
        out_shape=(out_sd, out_sd),
        compiler_params=_params(("parallel", "parallel", "parallel")),
        name=f"dsa_g{g}",
    )(q, k, k, k, v, v, v)


def _merge_kernel(o0_ref, l0_ref, o1_ref, l1_ref, o2_ref, l2_ref, yb_ref, gate_ref, x_ref,
                  wpa_ref, wpb_ref, wout_ref, out_ref, join_ref):
    def position_order(ref, slot):
        dil, per_class, _ = ref.shape
        if dil == 1:
            return ref[0]
        halves = []
        for c in range(DSA_GROUP_COLS // LANES):
            for r in range(dil):
                join_ref[slot, c, pl.ds(r, per_class, stride=dil), :] = ref[r, :, c * LANES:(c + 1) * LANES]
            halves.append(join_ref[slot, c])
        return jnp.concatenate(halves, axis=1)

    o0, l0 = position_order(o0_ref, 0), position_order(l0_ref, 0)
    o1, l1 = position_order(o1_ref, 0), position_order(l1_ref, 1)
    o2, l2 = position_order(o2_ref, 2), position_order(l2_ref, 3)
    m = jnp.maximum(jnp.maximum(l0, l1), l2)
    e0, e1, e2 = jnp.exp(l0 - m), jnp.exp(l1 - m), jnp.exp(l2 - m)
    den = e0 + e1 + e2
    ya = (e0 / den) * o0 + (e1 / den) * o1 + (e2 / den) * o2
    gate = gate_ref[...]
    pa = _dot(ya.astype(BF16), wpa_ref[...])
    pb = _dot(yb_ref[...], wpb_ref[...])
    merged = gate[:, :D_MODEL] * pa + gate[:, D_MODEL:] * pb
    out_ref[...] = x_ref[...] + _dot(merged.astype(BF16), wout_ref[...])


def _merge(dsa_out, yb, gates, x2d, seq, w, tm):
    t = x2d.shape[0]
    nseq = seq // tm
    row = lambda i: (i, 0)
    flat, specs = [], []
    for pair in dsa_out:
        for a in pair:
            dil = a.shape[1]
            flat.append(a)
            specs.append(pl.BlockSpec((None, dil, tm // dil, DSA_GROUP_COLS),
                                      lambda i: (i // nseq, 0, i % nseq, 0)))
    return pl.pallas_call(
        _merge_kernel,
        grid=(t // tm,),
        in_specs=specs + [
            pl.BlockSpec((tm, MLA_HEADS * MLA_SLAB), row),
            pl.BlockSpec((tm, 2 * D_MODEL), row),
            pl.BlockSpec((tm, D_MODEL), row),
            _const_spec(w['wpa'].shape), _const_spec(w['wpb'].shape), _const_spec(w['wout'].shape),
        ],
        out_specs=pl.BlockSpec((tm, D_MODEL), row),
        out_shape=jax.ShapeDtypeStruct((t, D_MODEL), F32),
        scratch_shapes=[pltpu.VMEM((4, DSA_GROUP_COLS // LANES, tm, LANES), F32)],
        compiler_params=_params(("parallel",)),
        name="merge",
    )(*flat, yb, gates, x2d, w['wpa'], w['wpb'], w['wout'])


def _norm_mm_kernel(x_ref, g_ref, w_ref, o_ref):
    o_ref[...] = _dot(_rms(x_ref[...], g_ref[...]).astype(BF16), w_ref[...]).astype(o_ref.dtype)


def _norm_mm(x2d, g, wmat, tm):
    t, d = x2d.shape
    n = wmat.shape[1]
    return pl.pallas_call(
        _norm_mm_kernel,
        grid=(t // tm,),
        in_specs=[pl.BlockSpec((tm, d), lambda i: (i, 0)), _const_spec((1, d)), _const_spec(wmat.shape)],
        out_specs=pl.BlockSpec((tm, n), lambda i: (i, 0)),
        out_shape=jax.ShapeDtypeStruct((t, n), BF16),
        compiler_params=_params(("parallel",)),
        name="mem_kv",
    )(x2d, g, wmat)


def _xattn_kernel(x_ref, g_ref, kv_ref, wq_ref, wo_ref, out_ref):
    x = x_ref[...]
    hq = _rms(x, g_ref[...]).astype(BF16)
    q = (_dot(hq, wq_ref[...]) * (MEM_HEAD_DIM ** -0.5)).astype(BF16)
    kv = kv_ref[...]
    outs = []
    for h in range(MEM_HEADS):
        sl = slice(h * MEM_HEAD_DIM, (h + 1) * MEM_HEAD_DIM)
        vsl = slice(D_MODEL + h * MEM_HEAD_DIM, D_MODEL + (h + 1) * MEM_HEAD_DIM)
        s = _dot_nt(q[:, sl], kv[:, sl])
        m = jnp.max(s, axis=1, keepdims=True)
        p = jnp.exp(s - m)
        den = jnp.sum(p, axis=1, keepdims=True)
        outs.append(_dot((p / den).astype(BF16), kv[:, vsl]).astype(BF16))
    o = jnp.concatenate(outs, axis=1)
    out_ref[...] = x + _dot(o, wo_ref[...])


def _xattn(x3d, kv3d, w, tm):
    b, s, d = x3d.shape
    n_mem = kv3d.shape[1]
    return pl.pallas_call(
        _xattn_kernel,
        grid=(b, s // tm),
        in_specs=[
            pl.BlockSpec((None, tm, d), lambda bi, i: (bi, i, 0)),
            _const_spec((1, d)),
            pl.BlockSpec((None, n_mem, 2 * d), lambda bi, i: (bi, 0, 0)),
            _const_spec(w['wmq'].shape), _const_spec(w['wmo'].shape),
        ],
        out_specs=pl.BlockSpec((None, tm, d), lambda bi, i: (bi, i, 0)),
        out_shape=jax.ShapeDtypeStruct((b, s, d), F32),
        compiler_params=_params(("parallel", "parallel")),
        name="xattn",
    )(x3d, w['g_xq'], kv3d, w['wmq'], w['wmo'])


def _top_values(sc, count, with_rank, break_ties):
    n = sc.shape[0]
    row = lax.broadcasted_iota(jnp.int32, sc.shape, 0)
    rank = jnp.full(sc.shape, float(n - 1), F32)
    vals = []
    for k in range(count):
        m = jnp.max(sc, axis=0, keepdims=True)
        hit = sc == m
        if break_ties:
            hit = row == jnp.min(jnp.where(hit, row, n), axis=0, keepdims=True)
        vals.append(m)
        if with_rank:
            rank = jnp.where(hit, float(k), rank)
        sc = jnp.where(hit, -jnp.inf, sc)
    dropped = jnp.sum(jnp.where(sc == -jnp.inf, 1.0, 0.0), axis=0, keepdims=True)
    return jnp.concatenate(vals, axis=0), rank, dropped


def _pack_bf16_rows(x):
    return pltpu.bitcast(x.astype(BF16), jnp.int32)


def _unpack_bf16_rows(words):
    return pltpu.bitcast(words, BF16)


def _bf16_pair_word(x):
    bits = lax.bitcast_convert_type(x.astype(BF16).astype(F32), jnp.uint32)
    return lax.bitcast_convert_type(bits | (bits >> 16), jnp.int32)


def _peer_score_kernel(x_ref, g_ref, wpq_ref, keys_ref, hqt_ref, r2_ref, cnt_ref, e1_ref, e2_ref):
    hq32 = _rms(x_ref[...], g_ref[...])

    @pl.when(pl.program_id(1) == 0)
    def _():
        hqt_ref[...] = _pack_bf16_rows(hq32.T)

    q = _dot(hq32.astype(BF16), wpq_ref[...]).astype(BF16)
    s1 = _dot_nt(keys_ref[0], q[:, :PEER_HALF])
    s2 = _dot_nt(keys_ref[1], q[:, PEER_HALF:])

    def statistics(break_ties):
        v1, _, d1 = _top_values(s1, PEER_TOPK, False, break_ties)
        v2, r2, d2 = _top_values(s2, PEER_TOPK, True, break_ties)
        cand = [v1[0:1] + v2]
        cand += [v1[i:i + 1] + v2[0:8] for i in range(1, 8)]
        cand += [v1[8:16] + v2[0:1]]
        top, _, d3 = _top_values(jnp.concatenate(cand, axis=0), PEER_TOPK, False, break_ties)
        z = jnp.sum(jnp.exp(top - top[0:1]), axis=0, keepdims=True)
        tau = top[PEER_TOPK - 1:PEER_TOPK]
        cnt = jnp.zeros(s1.shape, F32)
        for r in range(PEER_TOPK):
            cnt = cnt + jnp.where(s1 + v2[r:r + 1] >= tau, 1.0, 0.0)
        r2_ref[...] = _pack_bf16_rows(r2)
        cnt_ref[...] = _bf16_pair_word(cnt)
        e1_ref[...] = _bf16_pair_word(jnp.exp(s1 - v1[0:1]) / z)
        e2_ref[...] = _pack_bf16_rows(jnp.exp(s2 - v2[0:1]))
        return jnp.maximum(jnp.maximum(d1, d2), d3)

    most_dropped = statistics(break_ties=False)

    @pl.when(jnp.max(most_dropped) > PEER_TOPK)
    def _():
        statistics(break_ties=True)


def _peer_scores(x2d, w):
    t = x2d.shape[0]
    tt = PEER_TOKENS
    groups = t // tt
    words = lambda rows: jax.ShapeDtypeStruct((groups, PEER_HEADS, rows, tt), jnp.int32)
    spec = lambda rows: pl.BlockSpec((None, None, rows, tt), lambda i, h: (i, h, 0, 0))
    half = PEER_NKEYS // 2
    return pl.pallas_call(
        _peer_score_kernel,
        grid=(groups, PEER_HEADS),
        in_specs=[
            pl.BlockSpec((tt, D_MODEL), lambda i, h: (i, 0)),
            _const_spec((1, D_MODEL)),
            pl.BlockSpec((D_MODEL, 2 * PEER_HALF), lambda i, h: (0, h)),
            pl.BlockSpec((None, 2, PEER_NKEYS, PEER_HALF), lambda i, h: (h, 0, 0, 0)),
        ],
        out_specs=[pl.BlockSpec((None, D_MODEL // 2, tt), lambda i, h: (i, 0, 0)),
                   spec(half), spec(PEER_NKEYS), spec(PEER_NKEYS), spec(half)],
        out_shape=(jax.ShapeDtypeStruct((groups, D_MODEL // 2, tt), jnp.int32),
                   words(half), words(PEER_NKEYS), words(PEER_NKEYS), words(half)),
        compiler_params=_params(("parallel", "arbitrary")),
        name="peer_scores",
    )(x2d, w['g_ffn'], w['wpq'], w['keys'])


def _peer_mix_kernel(hqt_ref, u_ref, vt_ref, r2_ref, cnt_ref, e1_ref, e2_ref, x_ref, gf_ref,
                     out_ref, acc_ref, wg_ref, *, blocks):
    j = pl.program_id(1)
    last = pl.num_programs(1) - 1
    groups, _, tt = hqt_ref.shape
    half_rows = PEER_NKEYS // 2

    @pl.when(j == 0)
    def _():
        acc_ref[...] = jnp.zeros(acc_ref.shape, F32)
        wg_ref[...] = jnp.zeros(wg_ref.shape, jnp.int32)

    def mix_previous_block(g):
        acc_ref[g] += _dot(_unpack_bf16_rows(vt_ref[...]), _unpack_bf16_rows(wg_ref[g]))

    def row_bcast(words):
        return _unpack_bf16_rows(jnp.broadcast_to(words, (half_rows, LANES)))

    def activate_and_mix(g, carry):
        mix_previous_block(g)
        hqt = _unpack_bf16_rows(hqt_ref[g])
        for c in range(blocks // 2):
            u_pair = _unpack_bf16_rows(u_ref[c * PEER_NKEYS:(c + 1) * PEER_NKEYS, :])
            pre2 = _dot(u_pair, hqt)
            for half in range(2):
                i = 2 * c + half
                a = j * blocks + i
                cnt_rows = [cnt_ref[g, h, pl.ds(a, 1), :] for h in range(PEER_HEADS)]
                e1_rows = [e1_ref[g, h, pl.ds(a, 1), :] for h in range(PEER_HEADS)]
                for col in range(tt // LANES):
                    cs = slice(col * LANES, (col + 1) * LANES)
                    wsum = jnp.zeros((PEER_NKEYS, LANES), BF16)
                    for h in range(PEER_HEADS):
                        chosen = _unpack_bf16_rows(r2_ref[g, h, :, cs]) < row_bcast(cnt_rows[h][:, cs])
                        picked = jnp.where(chosen, _unpack_bf16_rows(e2_ref[g, h, :, cs]), jnp.zeros((), BF16))
                        wsum = wsum + picked * row_bcast(e1_rows[h][:, cs])
                    pre = pre2[half * PEER_NKEYS:(half + 1) * PEER_NKEYS, cs]
                    act = 0.5 * pre * (1.0 + lax.erf(pre * (0.5 ** 0.5)))
                    wg_ref[g, i * half_rows:(i + 1) * half_rows, cs] = pltpu.bitcast(
                        wsum * act.astype(BF16), jnp.int32)
        return carry

    @pl.when(j < last)
    def _():
        lax.fori_loop(0, groups, activate_and_mix, 0)

    @pl.when(j == last)
    def _():
        for g in range(groups):
            mix_previous_block(g)
            rows = slice(g * tt, (g + 1) * tt)
            out_ref[rows, :] = _rms(x_ref[rows, :] + acc_ref[g].T, gf_ref[...])


def _peer_mix(hqt, stats, x2d, w, groups, blocks):
    t = x2d.shape[0]
    tt = PEER_TOKENS
    rows = groups * tt
    ne = blocks * PEER_NKEYS
    nblk = PEER_NKEYS // blocks
    spec = lambda r: pl.BlockSpec((groups, PEER_HEADS, r, tt), lambda i, j: (i, 0, 0, 0))
    half = PEER_NKEYS // 2
    return pl.pallas_call(
        functools.partial(_peer_mix_kernel, blocks=blocks),
        grid=(t // rows, nblk + 1),
        in_specs=[
            pl.BlockSpec((groups, D_MODEL // 2, tt), lambda i, j: (i, 0, 0)),
            pl.BlockSpec((ne // 2, D_MODEL), lambda i, j: (jnp.minimum(j, nblk - 1), 0)),
            pl.BlockSpec((D_MODEL // 2, ne), lambda i, j: (0, jnp.maximum(j - 1, 0))),
            spec(half), spec(PEER_NKEYS), spec(PEER_NKEYS), spec(half),
            pl.BlockSpec((rows, D_MODEL), lambda i, j: (i, 0)),
            _const_spec((1, D_MODEL)),
        ],
        out_specs=pl.BlockSpec((rows, D_MODEL), lambda i, j: (i, 0)),
        out_shape=jax.ShapeDtypeStruct((t, D_MODEL), F32),
        scratch_shapes=[pltpu.VMEM((groups, D_MODEL, tt), F32), pltpu.VMEM((groups, ne // 2, tt), jnp.int32)],
        compiler_params=_params(("parallel", "arbitrary")),
        name="peer_mix",
    )(hqt, w['peer_u'], w['peer_vt'], *stats, x2d, w['g_final'])


def _pack_rows_host(a):
    m2, n = a.shape
    return lax.bitcast_convert_type(jnp.swapaxes(a.reshape(m2 // 2, 2, n), 1, 2), jnp.int32)


def _prep_weights(g_mix, w_in, g_qa, g_kva, w_uq, w_ukv, w_gate, b_gate, w_pa, w_pb, w_out,
                  g_xq, g_mkv, w_mq, w_mkv, w_mo, g_ffn, w_pq, sub_keys, peer_u, peer_v, g_final):
    w_in = w_in[0]
    q_cols = DSA_QK_COLS // 2
    wqk = jnp.concatenate([w_in[:, :q_cols] * (DSA_HEAD_DIM ** -0.5), w_in[:, q_cols:DSA_QK_COLS]], axis=1)
    c0 = DSA_QK_COLS + DSA_V_COLS
    c1 = c0 + MLA_Q_LORA + MLA_KV_LORA
    wkpe = jnp.zeros((D_MODEL, MLA_SLAB), F32).at[:, MLA_NOPE:MLA_QK].set(w_in[:, c1:])
    pad_head = lambda a: jnp.pad(a, ((0, 0), (0, 0), (0, MLA_SLAB - a.shape[2])))
    wq = pad_head(w_uq[0].reshape(MLA_Q_LORA, MLA_HEADS, MLA_QK)).reshape(MLA_Q_LORA, -1)
    ukv = w_ukv[0].reshape(MLA_KV_LORA, MLA_HEADS, MLA_NOPE + MLA_V)
    wk = pad_head(ukv[:, :, :MLA_NOPE]).reshape(MLA_KV_LORA, -1)
    wv = pad_head(ukv[:, :, MLA_NOPE:]).reshape(MLA_KV_LORA, -1)
    wpb = jnp.pad(w_pb[0].reshape(MLA_HEADS, MLA_V, D_MODEL), ((0, 0), (0, MLA_SLAB - MLA_V), (0, 0)))
    bf = lambda a: a.astype(BF16)
    return {
        'g_mix': g_mix, 'wqk': bf(wqk), 'wv': bf(w_in[:, DSA_QK_COLS:c0]), 'wc': bf(w_in[:, c0:c1]),
        'wkpe': bf(wkpe), 'wgate': bf(w_gate[0]), 'b_gate': b_gate,
        'g_qa': g_qa, 'g_kva': g_kva, 'wq_mla': bf(wq), 'wk_mla': bf(wk), 'wv_mla': bf(wv),
        'wpa': bf(w_pa[0]), 'wpb': bf(wpb.reshape(-1, D_MODEL)), 'wout': bf(w_out[0]),
        'g_xq': g_xq, 'g_mkv': g_mkv, 'wmq': bf(w_mq[0]), 'wmkv': bf(w_mkv[0]), 'wmo': bf(w_mo[0]),
        'g_ffn': g_ffn, 'wpq': bf(w_pq[0]),
        'keys': bf(sub_keys[0]),
        'peer_u': _pack_rows_host(bf(peer_u[0])), 'peer_vt': _pack_rows_host(bf(peer_v[0]).T),
        'g_final': g_final.reshape(1, D_MODEL),
    }


def _rope_tables(seq):
    def table(rot, lead, period):
        inv = ROPE_THETA ** (-jnp.arange(0, rot, 2, dtype=F32) / rot)
        ang = jnp.arange(seq, dtype=F32)[:, None] * inv[None, :]
        cos, sin = jnp.cos(ang), jnp.sin(ang)
        ones = lambda n: jnp.ones((seq, n), F32)
        zeros = lambda n: jnp.zeros((seq, n), F32)
        tail = period - lead - rot
        c = jnp.concatenate([ones(lead), cos, cos, ones(tail)], axis=1)
        s = jnp.concatenate([zeros(lead), -sin, sin, zeros(tail)], axis=1)
        reps = LANES // period
        return jnp.tile(c, (1, reps)), jnp.tile(s, (1, reps))
    ca, sa = table(DSA_ROT, 0, DSA_HEAD_DIM)
    cb, sb = table(MLA_ROPE, MLA_NOPE, MLA_SLAB)
    return {'ca': ca, 'sa': sa, 'cb': cb, 'sb': sb}


def _trunk(x, mem, w):
    b, s, d = x.shape
    t = b * s
    tabs = _rope_tables(s)
    x2d = x.reshape(t, d)
    tm = 256
    *dsa_in, cq, ckv, kpe, gates = _inproj(x2d, b, s, w, tabs, tm)
    dsa_out = [_dsa_group(*dsa_in[3 * g:3 * g + 3], g, 256) for g in range(len(DSA_DILATIONS))]
    q_b, k_b, v_b = _mla_proj(cq, ckv, kpe, s, w, tabs, tm)
    wide = MLA_HEADS * MLA_SLAB
    yb = _mla_attn(q_b.reshape(b, s, wide), k_b.reshape(b, s, wide), v_b.reshape(b, s, wide),
                   min(512, s), min(2048, s))
    x1 = _merge(dsa_out, yb.reshape(t, wide), gates, x2d, s, w, tm)
    n_mem = mem.shape[1]
    kv = _norm_mm(mem.reshape(b * n_mem, d), w['g_mkv'], w['wmkv'], n_mem)
    x2 = _xattn(x1.reshape(b, s, d), kv.reshape(b, n_mem, 2 * d), w, tm)
    x2 = x2.reshape(t, d)
    hqt, *stats = _peer_scores(x2, w)
    y = _peer_mix(hqt, stats, x2, w, 2, 8)
    return y.reshape(b, s, d)


def kernel(x_prompt, x_sample, mem_prompt, mem_sample, g_mix, w_in, g_qa, g_kva, w_uq, w_ukv, w_gate,
           b_gate, w_pa, w_pb, w_out, g_xq, g_mkv, w_mq, w_mkv, w_mo, g_ffn, w_pq, sub_keys, peer_u,
           peer_v, g_final):
    w = _prep_weights(g_mix, w_in, g_qa, g_kva, w_uq, w_ukv, w_gate, b_gate, w_pa, w_pb, w_out,
                      g_xq, g_mkv, w_mq, w_mkv, w_mo, g_ffn, w_pq, sub_keys, peer_u, peer_v, g_final)
    return (_trunk(x_prompt, mem_prompt, w), _trunk(x_sample, mem_sample, w))
```

```python
import functools

import jax
import jax.numpy as jnp
from jax import lax
from jax.experimental import pallas as pl
from jax.experimental.pallas import tpu as pltpu

F32 = jnp.float32
BF16 = jnp.bfloat16

D_MODEL = 1024
EPS = 1e-6
NEG_INF = -1e30
ROPE_THETA = 500000.0

DSA_DILATIONS = (1, 4, 16)
DSA_HALF = 64
DSA_HEADS = 4
DSA_HEAD_DIM = 64
DSA_ROT = 16
DSA_GROUP_COLS = DSA_HEADS * DSA_HEAD_DIM
DSA_QK_COLS = 2 * 3 * DSA_GROUP_COLS
DSA_V_COLS = 3 * DSA_GROUP_COLS

MLA_HEADS = 8
MLA_Q_LORA = 768
MLA_KV_LORA = 256
MLA_NOPE = 64
MLA_ROPE = 32
MLA_V = 64
MLA_QK = MLA_NOPE + MLA_ROPE
MLA_SLAB = 128
MLA_ONE_LANE = MLA_V
MLA_HEADS_PER_STEP = 1
MLA_KEY_CHUNK = 512
LOG2_E = 1.4426950408889634

MEM_HEADS = 4
MEM_HEAD_DIM = 256

PEER_HEADS = 8
PEER_NKEYS = 128
PEER_HALF = 128
PEER_TOPK = 16
PEER_TOKENS = 256
PEER_SCORE_HEADS = 4

LANES = 128
VMEM_LIMIT_BYTES = 56 * 1024 * 1024


def _params(sem):
    return pltpu.CompilerParams(dimension_semantics=sem, vmem_limit_bytes=VMEM_LIMIT_BYTES)


def _const_spec(shape):
    nd = len(shape)
    return pl.BlockSpec(shape, lambda *_: (0,) * nd)


def _rms(x, g):
    y = x * lax.rsqrt(jnp.mean(x * x, axis=-1, keepdims=True) + EPS)
    return y * g


def _dot(a, b):
    return jnp.dot(a, b, preferred_element_type=F32)


def _dot_nt(a, b):
    return lax.dot_general(a, b, (((1,), (1,)), ((), ())), preferred_element_type=F32)


def _rope_slab(x, cos_t, sin_t, take_up, half):
    up = pltpu.roll(x, LANES - half, 1)
    dn = pltpu.roll(x, half, 1)
    return x * cos_t + jnp.where(take_up, up, dn) * sin_t


def _inproj_kernel(x_ref, g_ref, wqk_ref, wv_ref, wc_ref, wkpe_ref, wg_ref, bg_ref,
                   ca_ref, sa_ref, cb_ref, sb_ref, *rest):
    dsa_refs = rest[:9]
    cq_ref, ckv_ref, kpe_ref, gate_ref, split_ref = rest[9:]
    h = _rms(x_ref[...], g_ref[...]).astype(BF16)
    rows = h.shape[0]
    lane = lax.broadcasted_iota(jnp.int32, (rows, LANES), 1)
    up_a = (lane % DSA_HEAD_DIM) < (DSA_ROT // 2)
    ca = ca_ref[...]
    sa = sa_ref[...]
    zqk = _dot(h, wqk_ref[...])
    zv = _dot(h, wv_ref[...])
    slot = 0
    for g, dil in enumerate(DSA_DILATIONS):
        for part in range(3):
            if part < 2:
                first = (part * DSA_QK_COLS // 2 + g * DSA_GROUP_COLS) // LANES
                val = jnp.concatenate(
                    [_rope_slab(zqk[:, s * LANES:(s + 1) * LANES], ca, sa, up_a, DSA_ROT // 2)
                     for s in range(first, first + DSA_GROUP_COLS // LANES)], axis=1)
            else:
                val = zv[:, g * DSA_GROUP_COLS:(g + 1) * DSA_GROUP_COLS]
            out_ref = dsa_refs[3 * g + part]
            if dil == 1:
                out_ref[0] = val.astype(BF16)
            else:
                for c in range(DSA_GROUP_COLS // LANES):
                    cs = slice(c * LANES, (c + 1) * LANES)
                    split_ref[slot, c] = val[:, cs]
                    for r in range(dil):
                        out_ref[r, :, cs] = split_ref[slot, c, pl.ds(r, rows // dil, stride=dil), :].astype(BF16)
                slot += 1
    c = _dot(h, wc_ref[...])
    cq_ref[...] = c[:, :MLA_Q_LORA]
    ckv_ref[...] = c[:, MLA_Q_LORA:]
    up_b = (lane >= MLA_NOPE) & (lane < MLA_NOPE + MLA_ROPE // 2)
    kpe_ref[...] = _rope_slab(_dot(h, wkpe_ref[...]), cb_ref[...], sb_ref[...], up_b, MLA_ROPE // 2)
    gate_ref[...] = jax.nn.sigmoid(_dot(h, wg_ref[...]) + bg_ref[...])


def _inproj(x2d, batch, seq, w, tabs, tm):
    t = x2d.shape[0]
    nseq = seq // tm
    row = lambda i: (i, 0)
    tab = lambda i: (i % nseq, 0)
    dsa_shapes, dsa_specs = [], []
    for dil in DSA_DILATIONS:
        for _ in range(3):
            dsa_shapes.append(jax.ShapeDtypeStruct((batch, dil, seq // dil, DSA_GROUP_COLS), BF16))
            dsa_specs.append(pl.BlockSpec((None, dil, tm // dil, DSA_GROUP_COLS),
                                          lambda i: (i // nseq, 0, i % nseq, 0)))
    flat_shapes = (
        jax.ShapeDtypeStruct((t, MLA_Q_LORA), F32),
        jax.ShapeDtypeStruct((t, MLA_KV_LORA), F32),
        jax.ShapeDtypeStruct((t, MLA_SLAB), F32),
        jax.ShapeDtypeStruct((t, 2 * D_MODEL), F32),
    )
    split_slots = 3 * sum(1 for dil in DSA_DILATIONS if dil > 1)
    return pl.pallas_call(
        _inproj_kernel,
        grid=(t // tm,),
        in_specs=[
            pl.BlockSpec((tm, D_MODEL), row),
            _const_spec((1, D_MODEL)),
            _const_spec(w['wqk'].shape), _const_spec(w['wv'].shape), _const_spec(w['wc'].shape),
            _const_spec(w['wkpe'].shape), _const_spec(w['wgate'].shape), _const_spec((1, 2 * D_MODEL)),
            pl.BlockSpec((tm, LANES), tab), pl.BlockSpec((tm, LANES), tab),
            pl.BlockSpec((tm, LANES), tab), pl.BlockSpec((tm, LANES), tab),
        ],
        out_specs=dsa_specs + [pl.BlockSpec((tm, s.shape[1]), row) for s in flat_shapes],
        out_shape=tuple(dsa_shapes) + flat_shapes,
        scratch_shapes=[pltpu.VMEM((split_slots, DSA_GROUP_COLS // LANES, tm, LANES), F32)],
        compiler_params=_params(("parallel",)),
        name="inproj",
    )(x2d, w['g_mix'], w['wqk'], w['wv'], w['wc'], w['wkpe'], w['wgate'], w['b_gate'],
      tabs['ca'], tabs['sa'], tabs['cb'], tabs['sb'])


def _mla_proj_kernel(cq_ref, ckv_ref, kpe_ref, gqa_ref, gkva_ref, wq_ref, wk_ref, wv_ref,
                     cb_ref, sb_ref, q_ref, k_ref, v_ref):
    qn = _rms(cq_ref[...], gqa_ref[...]).astype(BF16)
    rows = qn.shape[0]
    lane = lax.broadcasted_iota(jnp.int32, (rows, LANES), 1)
    up_b = (lane >= MLA_NOPE) & (lane < MLA_NOPE + MLA_ROPE // 2)
    cb = cb_ref[...]
    sb = sb_ref[...]
    q = _dot(qn, wq_ref[...])
    kvn = _rms(ckv_ref[...], gkva_ref[...]).astype(BF16)
    k = _dot(kvn, wk_ref[...])
    kpe = kpe_ref[...]
    scale = MLA_QK ** -0.5 * LOG2_E
    for s in range(MLA_HEADS):
        sl = slice(s * LANES, (s + 1) * LANES)
        q_ref[:, sl] = (_rope_slab(q[:, sl], cb, sb, up_b, MLA_ROPE // 2) * scale).astype(BF16)
        k_ref[:, sl] = (k[:, sl] + kpe).astype(BF16)
    wide_lane = lax.broadcasted_iota(jnp.int32, (1, MLA_HEADS * MLA_SLAB), 1)
    one_lane = jnp.where(wide_lane % MLA_SLAB == MLA_ONE_LANE, 1.0, 0.0)
    v_ref[...] = (_dot(kvn, wv_ref[...]) + one_lane).astype(BF16)


def _mla_proj(cq, ckv, kpe, seq, w, tabs, tm):
    t = cq.shape[0]
    nseq = seq // tm
    row = lambda i: (i, 0)
    tab = lambda i: (i % nseq, 0)
    wide = MLA_HEADS * MLA_SLAB
    out_shape = tuple(jax.ShapeDtypeStruct((t, wide), BF16) for _ in range(3))
    return pl.pallas_call(
        _mla_proj_kernel,
        grid=(t // tm,),
        in_specs=[
            pl.BlockSpec((tm, MLA_Q_LORA), row), pl.BlockSpec((tm, MLA_KV_LORA), row),
            pl.BlockSpec((tm, MLA_SLAB), row),
            _const_spec((1, MLA_Q_LORA)), _const_spec((1, MLA_KV_LORA)),
            _const_spec(w['wq_mla'].shape), _const_spec(w['wk_mla'].shape), _const_spec(w['wv_mla'].shape),
            pl.BlockSpec((tm, LANES), tab), pl.BlockSpec((tm, LANES), tab),
        ],
        out_specs=[pl.BlockSpec((tm, wide), row) for _ in range(3)],
        out_shape=out_shape,
        compiler_params=_params(("parallel",)),
        name="mla_proj",
    )(cq, ckv, kpe, w['g_qa'], w['g_kva'], w['wq_mla'], w['wk_mla'], w['wv_mla'], tabs['cb'], tabs['sb'])


def _mla_attn_kernel(q_ref, k_ref, v_ref, o_ref, m_ref, acc_ref):
    j = pl.program_id(3)

    @pl.when(j == 0)
    def _():
        m_ref[...] = jnp.full(m_ref.shape, -jnp.inf, F32)
        acc_ref[...] = jnp.zeros(acc_ref.shape, F32)

    tk = k_ref.shape[0]
    for h in range(MLA_HEADS_PER_STEP):
        sl = slice(h * MLA_SLAB, (h + 1) * MLA_SLAB)
        q = q_ref[:, sl]
        m_run = m_ref[:, sl]
        acc = acc_ref[:, sl]
        for c in range(tk // MLA_KEY_CHUNK):
            rows = slice(c * MLA_KEY_CHUNK, (c + 1) * MLA_KEY_CHUNK)
            s = _dot_nt(q, k_ref[rows, sl])
            cols = [s[:, i * LANES:(i + 1) * LANES] for i in range(MLA_KEY_CHUNK // LANES)]
            m_loc = functools.reduce(jnp.maximum, cols)
            m_new = jnp.maximum(m_run, jnp.max(m_loc, axis=1, keepdims=True))
            p = jnp.concatenate([jnp.exp2(col - m_new) for col in cols], axis=1).astype(BF16)
            acc = jnp.exp2(m_run - m_new) * acc + _dot(p, v_ref[rows, sl])
            m_run = m_new
        acc_ref[:, sl] = acc
        m_ref[:, sl] = m_run

    @pl.when(j == pl.num_programs(3) - 1)
    def _():
        for h in range(MLA_HEADS_PER_STEP):
            sl = slice(h * MLA_SLAB, (h + 1) * MLA_SLAB)
            acc = acc_ref[:, sl]
            den = acc[:, MLA_ONE_LANE:MLA_ONE_LANE + 1]
            o_ref[:, sl] = (acc / den).astype(o_ref.dtype)


def _mla_attn(q, k, v, tq, tk):
    b, s, wide = q.shape
    cols = MLA_HEADS_PER_STEP * MLA_SLAB
    return pl.pallas_call(
        _mla_attn_kernel,
        grid=(b, MLA_HEADS // MLA_HEADS_PER_STEP, s // tq, s // tk),
        in_specs=[
            pl.BlockSpec((None, tq, cols), lambda bi, h, i, j: (bi, i, h)),
            pl.BlockSpec((None, tk, cols), lambda bi, h, i, j: (bi, j, h)),
            pl.BlockSpec((None, tk, cols), lambda bi, h, i, j: (bi, j, h)),
        ],
        out_specs=pl.BlockSpec((None, tq, cols), lambda bi, h, i, j: (bi, i, h)),
        out_shape=jax.ShapeDtypeStruct((b, s, wide), BF16),
        scratch_shapes=[pltpu.VMEM((tq, cols), F32), pltpu.VMEM((tq, cols), F32)],
        compiler_params=_params(("parallel", "parallel", "parallel", "arbitrary")),
        name="mla_attn",
    )(q, k, v)


def _dsa_kernel(q_ref, kp_ref, kc_ref, kn_ref, vp_ref, vc_ref, vn_ref, o_ref, lse_ref, *, cls_len):
    j = pl.program_id(2)
    tq = q_ref.shape[0]
    nk = tq + 2 * DSA_HALF
    q = q_ref[...]
    kk = jnp.concatenate([kp_ref[...], kc_ref[...], kn_ref[...]], axis=0)
    vv = jnp.concatenate([vp_ref[...], vc_ref[...], vn_ref[...]], axis=0)
    qpos = j * tq + lax.broadcasted_iota(jnp.int32, (tq, nk), 0)
    kpos = j * tq - DSA_HALF + lax.broadcasted_iota(jnp.int32, (tq, nk), 1)
    ok = (jnp.abs(kpos - qpos) <= DSA_HALF) & (kpos >= 0) & (kpos < cls_len)
    lane = lax.broadcasted_iota(jnp.int32, (tq, DSA_GROUP_COLS), 1)
    o = jnp.zeros((tq, DSA_GROUP_COLS), F32)
    lse = jnp.zeros((tq, DSA_GROUP_COLS), F32)
    for h in range(DSA_HEADS):
        in_head = (lane // DSA_HEAD_DIM) == h
        s = _dot_nt(jnp.where(in_head, q, jnp.zeros_like(q)), kk)
        s = jnp.where(ok, s, NEG_INF)
        m = jnp.max(s, axis=1, keepdims=True)
        p = jnp.exp(s - m)
        den = jnp.sum(p, axis=1, keepdims=True)
        oh = _dot((p / den).astype(BF16), vv)
        o = jnp.where(in_head, oh, o)
        lse = jnp.where(in_head, m + jnp.log(den), lse)
    o_ref[...] = o
    lse_ref[...] = lse


def _dsa_group(q, k, v, g, tq):
    b, dil, cls_len, _ = q.shape
    tq = min(tq, cls_len)
    per = tq // DSA_HALF
    last = cls_len // DSA_HALF - 1
    prev = lambda j: jnp.maximum(j * per - 1, 0)
    nxt = lambda j: jnp.minimum((j + 1) * per, last)
    halo = (None, None, DSA_HALF, DSA_GROUP_COLS)
    cur = (None, None, tq, DSA_GROUP_COLS)
    cur_spec = pl.BlockSpec(cur, lambda bi, r, j: (bi, r, j, 0))
    prev_spec = pl.BlockSpec(halo, lambda bi, r, j: (bi, r, prev(j), 0))
    next_spec = pl.BlockSpec(halo, lambda bi, r, j: (bi, r, nxt(j), 0))
    out_sd = jax.ShapeDtypeStruct((b, dil, cls_len, DSA_GROUP_COLS), F32)
    return pl.pallas_call(
        functools.partial(_dsa_kernel, cls_len=cls_len),
        grid=(b, dil, cls_len // tq),
        in_specs=[cur_spec, prev_spec, cur_spec, next_spec, prev_spec, cur_spec, next_spec],
        out_specs=[cur_spec, cur_spec],
        out_shape=(out_sd, out_sd),
        compiler_params=_params(("parallel", "parallel", "parallel")),
        name=f"dsa_g{g}",
    )(q, k, k, k, v, v, v)


def _merge_kernel(o0_ref, l0_ref, o1_ref, l1_ref, o2_ref, l2_ref, yb_ref, gate_ref, x_ref,
                  wpa_ref, wpb_ref, wout_ref, out_ref, join_ref):
    def position_order(ref, slot):
        dil, per_class, _ = ref.shape
        if dil == 1:
            return ref[0]
        halves = []
        for c in range(DSA_GROUP_COLS // LANES):
            for r in range(dil):
                join_ref[slot, c, pl.ds(r, per_class, stride=dil), :] = ref[r, :, c * LANES:(c + 1) * LANES]
            halves.append(join_ref[slot, c])
        return jnp.concatenate(halves, axis=1)

    o0, l0 = position_order(o0_ref, 0), position_order(l0_ref, 0)
    o1, l1 = position_order(o1_ref, 0), position_order(l1_ref, 1)
    o2, l2 = position_order(o2_ref, 2), position_order(l2_ref, 3)
    m = jnp.maximum(jnp.maximum(l0, l1), l2)
    e0, e1, e2 = jnp.exp(l0 - m), jnp.exp(l1 - m), jnp.exp(l2 - m)
    den = e0 + e1 + e2
    ya = (e0 / den) * o0 + (e1 / den) * o1 + (e2 / den) * o2
    gate = gate_ref[...]
    pa = _dot(ya.astype(BF16), wpa_ref[...])
    pb = _dot(yb_ref[...], wpb_ref[...])
    merged = gate[:, :D_MODEL] * pa + gate[:, D_MODEL:] * pb
    out_ref[...] = x_ref[...] + _dot(merged.astype(BF16), wout_ref[...])


def _merge(dsa_out, yb, gates, x2d, seq, w, tm):
    t = x2d.shape[0]
    nseq = seq // tm
    row = lambda i: (i, 0)
    flat, specs = [], []
    for pair in dsa_out:
        for a in pair:
            dil = a.shape[1]
            flat.append(a)
            specs.append(pl.BlockSpec((None, dil, tm // dil, DSA_GROUP_COLS),
                                      lambda i: (i // nseq, 0, i % nseq, 0)))
    return pl.pallas_call(
        _merge_kernel,
        grid=(t // tm,),
        in_specs=specs + [
            pl.BlockSpec((tm, MLA_HEADS * MLA_SLAB), row),
            pl.BlockSpec((tm, 2 * D_MODEL), row),
            pl.BlockSpec((tm, D_MODEL), row),
            _const_spec(w['wpa'].shape), _const_spec(w['wpb'].shape), _const_spec(w['wout'].shape),
        ],
        out_specs=pl.BlockSpec((tm, D_MODEL), row),
        out_shape=jax.ShapeDtypeStruct((t, D_MODEL), F32),
        scratch_shapes=[pltpu.VMEM((4, DSA_GROUP_COLS // LANES, tm, LANES), F32)],
        compiler_params=_params(("parallel",)),
        name="merge",
    )(*flat, yb, gates, x2d, w['wpa'], w['wpb'], w['wout'])


def _norm_mm_kernel(x_ref, g_ref, w_ref, o_ref):
    o_ref[...] = _dot(_rms(x_ref[...], g_ref[...]).astype(BF16), w_ref[...]).astype(o_ref.dtype)


def _norm_mm(x2d, g, wmat, tm):
    t, d = x2d.shape
    n = wmat.shape[1]
    return pl.pallas_call(
        _norm_mm_kernel,
        grid=(t // tm,),
        in_specs=[pl.BlockSpec((tm, d), lambda i: (i, 0)), _const_spec((1, d)), _const_spec(wmat.shape)],
        out_specs=pl.BlockSpec((tm, n), lambda i: (i, 0)),
        out_shape=jax.ShapeDtypeStruct((t, n), BF16),
        compiler_params=_params(("parallel",)),
        name="mem_kv",
    )(x2d, g, wmat)


def _xattn_kernel(x_ref, g_ref, kv_ref, wq_ref, wo_ref, out_ref):
    x = x_ref[...]
    hq = _rms(x, g_ref[...]).astype(BF16)
    q = (_dot(hq, wq_ref[...]) * (MEM_HEAD_DIM ** -0.5)).astype(BF16)
    kv = kv_ref[...]
    outs = []
    for h in range(MEM_HEADS):
        sl = slice(h * MEM_HEAD_DIM, (h + 1) * MEM_HEAD_DIM)
        vsl = slice(D_MODEL + h * MEM_HEAD_DIM, D_MODEL + (h + 1) * MEM_HEAD_DIM)
        s = _dot_nt(q[:, sl], kv[:, sl])
        m = jnp.max(s, axis=1, keepdims=True)
        p = jnp.exp(s - m)
        den = jnp.sum(p, axis=1, keepdims=True)
        outs.append(_dot((p / den).astype(BF16), kv[:, vsl]).astype(BF16))
    o = jnp.concatenate(outs, axis=1)
    out_ref[...] = x + _dot(o, wo_ref[...])


def _xattn(x3d, kv3d, w, tm):
    b, s, d = x3d.shape
    n_mem = kv3d.shape[1]
    return pl.pallas_call(
        _xattn_kernel,
        grid=(b, s // tm),
        in_specs=[
            pl.BlockSpec((None, tm, d), lambda bi, i: (bi, i, 0)),
            _const_spec((1, d)),
            pl.BlockSpec((None, n_mem, 2 * d), lambda bi, i: (bi, 0, 0)),
            _const_spec(w['wmq'].shape), _const_spec(w['wmo'].shape),
        ],
        out_specs=pl.BlockSpec((None, tm, d), lambda bi, i: (bi, i, 0)),
        out_shape=jax.ShapeDtypeStruct((b, s, d), F32),
        compiler_params=_params(("parallel", "parallel")),
        name="xattn",
    )(x3d, w['g_xq'], kv3d, w['wmq'], w['wmo'])


def _top_values(sc, count, with_rank, break_ties):
    n = sc.shape[0]
    row = lax.broadcasted_iota(jnp.int32, sc.shape, 0)
    rank = jnp.full(sc.shape, float(n - 1), F32)
    vals = []
    for k in range(count):
        m = jnp.max(sc, axis=0, keepdims=True)
        hit = sc == m
        if break_ties:
            hit = row == jnp.min(jnp.where(hit, row, n), axis=0, keepdims=True)
        vals.append(m)
        if with_rank:
            rank = jnp.where(hit, float(k), rank)
        sc = jnp.where(hit, -jnp.inf, sc)
    dropped = jnp.sum(jnp.where(sc == -jnp.inf, 1.0, 0.0), axis=0, keepdims=True)
    return jnp.concatenate(vals, axis=0), rank, dropped


def _pack_bf16_rows(x):
    return pltpu.bitcast(x.astype(BF16), jnp.int32)


def _unpack_bf16_rows(words):
    return pltpu.bitcast(words, BF16)


def _bf16_pair_word(x):
    bits = lax.bitcast_convert_type(x.astype(BF16).astype(F32), jnp.uint32)
    return lax.bitcast_convert_type(bits | (bits >> 16), jnp.int32)


def _peer_score_kernel(x_ref, g_ref, wpq_ref, keys_ref, hqt_ref, r2s_ref, cnts_ref, e1s_ref, e2s_ref):
    hq32 = _rms(x_ref[...], g_ref[...])

    @pl.when(pl.program_id(1) == 0)
    def _():
        hqt_ref[...] = _pack_bf16_rows(hq32.T)

    q = _dot(hq32.astype(BF16), wpq_ref[...]).astype(BF16)
    heads = keys_ref.shape[0]
    halves = [[_dot_nt(keys_ref[h, p], q[:, (2 * h + p) * PEER_HALF:(2 * h + p + 1) * PEER_HALF])
               for p in range(2)] for h in range(heads)]

    def statistics(break_ties):
        worst = [head_statistics(h, break_ties) for h in range(heads)]
        return functools.reduce(jnp.maximum, worst)

    def head_statistics(h, break_ties):
        s1, s2 = halves[h]
        r2_ref, cnt_ref, e1_ref, e2_ref = (ref.at[h] for ref in (r2s_ref, cnts_ref, e1s_ref, e2s_ref))
        v1, _, d1 = _top_values(s1, PEER_TOPK, False, break_ties)
        v2, r2, d2 = _top_values(s2, PEER_TOPK, True, break_ties)
        cand = [v1[0:1] + v2]
        cand += [v1[i:i + 1] + v2[0:8] for i in range(1, 8)]
        cand += [v1[8:16] + v2[0:1]]
        top, _, d3 = _top_values(jnp.concatenate(cand, axis=0), PEER_TOPK, False, break_ties)
        z = jnp.sum(jnp.exp(top - top[0:1]), axis=0, keepdims=True)
        tau = top[PEER_TOPK - 1:PEER_TOPK]
        cnt = jnp.zeros(s1.shape, F32)
        for r in range(PEER_TOPK):
            cnt = cnt + jnp.where(s1 + v2[r:r + 1] >= tau, 1.0, 0.0)
        r2_ref[...] = _pack_bf16_rows(r2)
        cnt_ref[...] = _bf16_pair_word(cnt)
        e1_ref[...] = _bf16_pair_word(jnp.exp(s1 - v1[0:1]) * (0.5 / z))
        e2_ref[...] = _pack_bf16_rows(jnp.exp(s2 - v2[0:1]))
        return jnp.maximum(jnp.maximum(d1, d2), d3)

    most_dropped = statistics(break_ties=False)

    @pl.when(jnp.max(most_dropped) > PEER_TOPK)
    def _():
        statistics(break_ties=True)


def _peer_scores(x2d, w):
    t = x2d.shape[0]
    tt = PEER_TOKENS
    groups = t // tt
    words = lambda rows: jax.ShapeDtypeStruct((groups, PEER_HEADS, rows, tt), jnp.int32)
    hs = PEER_SCORE_HEADS
    spec = lambda rows: pl.BlockSpec((None, hs, rows, tt), lambda i, h: (i, h, 0, 0))
    half = PEER_NKEYS // 2
    return pl.pallas_call(
        _peer_score_kernel,
        grid=(groups, PEER_HEADS // hs),
        in_specs=[
            pl.BlockSpec((tt, D_MODEL), lambda i, h: (i, 0)),
            _const_spec((1, D_MODEL)),
            pl.BlockSpec((D_MODEL, hs * 2 * PEER_HALF), lambda i, h: (0, h)),
            pl.BlockSpec((hs, 2, PEER_NKEYS, PEER_HALF), lambda i, h: (h, 0, 0, 0)),
        ],
        out_specs=[pl.BlockSpec((None, D_MODEL // 2, tt), lambda i, h: (i, 0, 0)),
                   spec(half), spec(PEER_NKEYS), spec(PEER_NKEYS), spec(half)],
        out_shape=(jax.ShapeDtypeStruct((groups, D_MODEL // 2, tt), jnp.int32),
                   words(half), words(PEER_NKEYS), words(PEER_NKEYS), words(half)),
        compiler_params=_params(("parallel", "arbitrary")),
        name="peer_scores",
    )(x2d, w['g_ffn'], w['wpq'], w['keys'])


def _peer_mix_kernel(hqt_ref, u_ref, vt_ref, r2_ref, cnt_ref, e1_ref, e2_ref, x_ref, gf_ref,
                     out_ref, acc_ref, wg_ref, *, blocks):
    j = pl.program_id(1)
    last = pl.num_programs(1) - 1
    groups, _, tt = hqt_ref.shape
    half_rows = PEER_NKEYS // 2

    @pl.when(j == 0)
    def _():
        acc_ref[...] = jnp.zeros(acc_ref.shape, F32)
        wg_ref[...] = jnp.zeros(wg_ref.shape, jnp.int32)

    def mix_previous_block(g):
        acc_ref[g] += _dot(_unpack_bf16_rows(vt_ref[...]), _unpack_bf16_rows(wg_ref[g]))

    def row_bcast(words):
        return _unpack_bf16_rows(jnp.broadcast_to(words, (half_rows, LANES)))

    def activate_and_mix(g, carry):
        mix_previous_block(g)
        hqt = _unpack_bf16_rows(hqt_ref[g])
        for c in range(blocks // 2):
            u_pair = _unpack_bf16_rows(u_ref[c * PEER_NKEYS:(c + 1) * PEER_NKEYS, :])
            pre2 = _dot(u_pair, hqt)
            for half in range(2):
                i = 2 * c + half
                a = j * blocks + i
                cnt_rows = [cnt_ref[g, h, pl.ds(a, 1), :] for h in range(PEER_HEADS)]
                e1_rows = [e1_ref[g, h, pl.ds(a, 1), :] for h in range(PEER_HEADS)]
                for col in range(tt // LANES):
                    cs = slice(col * LANES, (col + 1) * LANES)
                    wsum = jnp.zeros((PEER_NKEYS, LANES), BF16)
                    for h in range(PEER_HEADS):
                        chosen = _unpack_bf16_rows(r2_ref[g, h, :, cs]) < row_bcast(cnt_rows[h][:, cs])
                        picked = jnp.where(chosen, _unpack_bf16_rows(e2_ref[g, h, :, cs]), jnp.zeros((), BF16))
                        wsum = wsum + picked * row_bcast(e1_rows[h][:, cs])
                    pre = pre2[half * PEER_NKEYS:(half + 1) * PEER_NKEYS, cs]
                    act = pre * (1.0 + lax.erf(pre * (0.5 ** 0.5)))
                    wg_ref[g, i * half_rows:(i + 1) * half_rows, cs] = pltpu.bitcast(
                        wsum * act.astype(BF16), jnp.int32)
        return carry

    @pl.when(j < last)
    def _():
        lax.fori_loop(0, groups, activate_and_mix, 0)

    @pl.when(j == last)
    def _():
        for g in range(groups):
            mix_previous_block(g)
            rows = slice(g * tt, (g + 1) * tt)
            out_ref[rows, :] = _rms(x_ref[rows, :] + acc_ref[g].T, gf_ref[...])


def _peer_mix(hqt, stats, x2d, w, groups, blocks):
    t = x2d.shape[0]
    tt = PEER_TOKENS
    rows = groups * tt
    ne = blocks * PEER_NKEYS
    nblk = PEER_NKEYS // blocks
    spec = lambda r: pl.BlockSpec((groups, PEER_HEADS, r, tt), lambda i, j: (i, 0, 0, 0))
    half = PEER_NKEYS // 2
    return pl.pallas_call(
        functools.partial(_peer_mix_kernel, blocks=blocks),
        grid=(t // rows, nblk + 1),
        in_specs=[
            pl.BlockSpec((groups, D_MODEL // 2, tt), lambda i, j: (i, 0, 0)),
            pl.BlockSpec((ne // 2, D_MODEL), lambda i, j: (jnp.minimum(j, nblk - 1), 0)),
            pl.BlockSpec((D_MODEL // 2, ne), lambda i, j: (0, jnp.maximum(j - 1, 0))),
            spec(half), spec(PEER_NKEYS), spec(PEER_NKEYS), spec(half),
            pl.BlockSpec((rows, D_MODEL), lambda i, j: (i, 0)),
            _const_spec((1, D_MODEL)),
        ],
        out_specs=pl.BlockSpec((rows, D_MODEL), lambda i, j: (i, 0)),
        out_shape=jax.ShapeDtypeStruct((t, D_MODEL), F32),
        scratch_shapes=[pltpu.VMEM((groups, D_MODEL, tt), F32), pltpu.VMEM((groups, ne // 2, tt), jnp.int32)],
        compiler_params=_params(("parallel", "arbitrary")),
        name="peer_mix",
    )(hqt, w['peer_u'], w['peer_vt'], *stats, x2d, w['g_final'])


def _pack_rows_host(a):
    m2, n = a.shape
    return lax.bitcast_convert_type(jnp.swapaxes(a.reshape(m2 // 2, 2, n), 1, 2), jnp.int32)


def _prep_weights(g_mix, w_in, g_qa, g_kva, w_uq, w_ukv, w_gate, b_gate, w_pa, w_pb, w_out,
                  g_xq, g_mkv, w_mq, w_mkv, w_mo, g_ffn, w_pq, sub_keys, peer_u, peer_v, g_final):
    w_in = w_in[0]
    q_cols = DSA_QK_COLS // 2
    wqk = jnp.concatenate([w_in[:, :q_cols] * (DSA_HEAD_DIM ** -0.5), w_in[:, q_cols:DSA_QK_COLS]], axis=1)
    c0 = DSA_QK_COLS + DSA_V_COLS
    c1 = c0 + MLA_Q_LORA + MLA_KV_LORA
    wkpe = jnp.zeros((D_MODEL, MLA_SLAB), F32).at[:, MLA_NOPE:MLA_QK].set(w_in[:, c1:])
    pad_head = lambda a: jnp.pad(a, ((0, 0), (0, 0), (0, MLA_SLAB - a.shape[2])))
    wq = pad_head(w_uq[0].reshape(MLA_Q_LORA, MLA_HEADS, MLA_QK)).reshape(MLA_Q_LORA, -1)
    ukv = w_ukv[0].reshape(MLA_KV_LORA, MLA_HEADS, MLA_NOPE + MLA_V)
    wk = pad_head(ukv[:, :, :MLA_NOPE]).reshape(MLA_KV_LORA, -1)
    wv = pad_head(ukv[:, :, MLA_NOPE:]).reshape(MLA_KV_LORA, -1)
    wpb = jnp.pad(w_pb[0].reshape(MLA_HEADS, MLA_V, D_MODEL), ((0, 0), (0, MLA_SLAB - MLA_V), (0, 0)))
    bf = lambda a: a.astype(BF16)
    return {
        'g_mix': g_mix, 'wqk': bf(wqk), 'wv': bf(w_in[:, DSA_QK_COLS:c0]), 'wc': bf(w_in[:, c0:c1]),
        'wkpe': bf(wkpe), 'wgate': bf(w_gate[0]), 'b_gate': b_gate,
        'g_qa': g_qa, 'g_kva': g_kva, 'wq_mla': bf(wq), 'wk_mla': bf(wk), 'wv_mla': bf(wv),
        'wpa': bf(w_pa[0]), 'wpb': bf(wpb.reshape(-1, D_MODEL)), 'wout': bf(w_out[0]),
        'g_xq': g_xq, 'g_mkv': g_mkv, 'wmq': bf(w_mq[0]), 'wmkv': bf(w_mkv[0]), 'wmo': bf(w_mo[0]),
        'g_ffn': g_ffn, 'wpq': bf(w_pq[0]),
        'keys': bf(sub_keys[0]),
        'peer_u': _pack_rows_host(bf(peer_u[0])), 'peer_vt': _pack_rows_host(bf(peer_v[0]).T),
        'g_final': g_final.reshape(1, D_MODEL),
    }


def _rope_tables(seq):
    def table(rot, lead, period):
        inv = ROPE_THETA ** (-jnp.arange(0, rot, 2, dtype=F32) / rot)
        ang = jnp.arange(seq, dtype=F32)[:, None] * inv[None, :]
        cos, sin = jnp.cos(ang), jnp.sin(ang)
        ones = lambda n: jnp.ones((seq, n), F32)
        zeros = lambda n: jnp.zeros((seq, n), F32)
        tail = period - lead - rot
        c = jnp.concatenate([ones(lead), cos, cos, ones(tail)], axis=1)
        s = jnp.concatenate([zeros(lead), -sin, sin, zeros(tail)], axis=1)
        reps = LANES // period
        return jnp.tile(c, (1, reps)), jnp.tile(s, (1, reps))
    ca, sa = table(DSA_ROT, 0, DSA_HEAD_DIM)
    cb, sb = table(MLA_ROPE, MLA_NOPE, MLA_SLAB)
    return {'ca': ca, 'sa': sa, 'cb': cb, 'sb': sb}


def _trunk(x, mem, w):
    b, s, d = x.shape
    t = b * s
    tabs = _rope_tables(s)
    x2d = x.reshape(t, d)
    tm = 256
    *dsa_in, cq, ckv, kpe, gates = _inproj(x2d, b, s, w, tabs, tm)
    dsa_out = [_dsa_group(*dsa_in[3 * g:3 * g + 3], g, 256) for g in range(len(DSA_DILATIONS))]
    q_b, k_b, v_b = _mla_proj(cq, ckv, kpe, s, w, tabs, tm)
    wide = MLA_HEADS * MLA_SLAB
    yb = _mla_attn(q_b.reshape(b, s, wide), k_b.reshape(b, s, wide), v_b.reshape(b, s, wide),
                   min(2048, s), min(2048, s))
    x1 = _merge(dsa_out, yb.reshape(t, wide), gates, x2d, s, w, tm)
    n_mem = mem.shape[1]
    kv = _norm_mm(mem.reshape(b * n_mem, d), w['g_mkv'], w['wmkv'], n_mem)
    x2 = _xattn(x1.reshape(b, s, d), kv.reshape(b, n_mem, 2 * d), w, tm)
    x2 = x2.reshape(t, d)
    hqt, *stats = _peer_scores(x2, w)
    y = _peer_mix(hqt, stats, x2, w, 2, 16)
    return y.reshape(b, s, d)


def kernel(x_prompt, x_sample, mem_prompt, mem_sample, g_mix, w_in, g_qa, g_kva, w_uq, w_ukv, w_gate,
           b_gate, w_pa, w_pb, w_out, g_xq, g_mkv, w_mq, w_mkv, w_mo, g_ffn, w_pq, sub_keys, peer_u,
           peer_v, g_final):
    w = _prep_weights(g_mix, w_in, g_qa, g_kva, w_uq, w_ukv, w_gate, b_gate, w_pa, w_pb, w_out,
                      g_xq, g_mkv, w_mq, w_mkv, w_mo, g_ffn, w_pq, sub_keys, peer_u, peer_v, g_final)
    return (_trunk(x_prompt, mem_prompt, w), _trunk(x_sample, mem_sample, w))
```

```python
import functools

import jax
import jax.numpy as jnp
from jax import lax
from jax.experimental import pallas as pl
from jax.experimental.pallas import tpu as pltpu

F32 = jnp.float32
BF16 = jnp.bfloat16

D_MODEL = 1024
EPS = 1e-6
NEG_INF = -1e30
ROPE_THETA = 500000.0

DSA_DILATIONS = (1, 4, 16)
DSA_HALF = 64
DSA_HEADS = 4
DSA_HEAD_DIM = 64
DSA_ROT = 16
DSA_GROUP_COLS = DSA_HEADS * DSA_HEAD_DIM
DSA_QK_COLS = 2 * 3 * DSA_GROUP_COLS
DSA_V_COLS = 3 * DSA_GROUP_COLS

MLA_HEADS = 8
MLA_Q_LORA = 768
MLA_KV_LORA = 256
MLA_NOPE = 64
MLA_ROPE = 32
MLA_V = 64
MLA_QK = MLA_NOPE + MLA_ROPE
MLA_SLAB = 128
MLA_ONE_LANE = MLA_V
MLA_HEADS_PER_STEP = 1
MLA_KEY_CHUNK = 512
LOG2_E = 1.4426950408889634

MEM_HEADS = 4
MEM_HEAD_DIM = 256

PEER_HEADS = 8
PEER_NKEYS = 128
PEER_HALF = 128
PEER_TOPK = 16
PEER_TOKENS = 256
PEER_SCORE_HEADS = 4

LANES = 128
VMEM_LIMIT_BYTES = 56 * 1024 * 1024


def _params(sem):
    return pltpu.CompilerParams(dimension_semantics=sem, vmem_limit_bytes=VMEM_LIMIT_BYTES)


def _const_spec(shape):
    nd = len(shape)
    return pl.BlockSpec(shape, lambda *_: (0,) * nd, pipeline_mode=pl.Buffered(1))


def _rms(x, g):
    y = x * lax.rsqrt(jnp.mean(x * x, axis=-1, keepdims=True) + EPS)
    return y * g


def _dot(a, b):
    return jnp.dot(a, b, preferred_element_type=F32)


def _dot_nt(a, b):
    return lax.dot_general(a, b, (((1,), (1,)), ((), ())), preferred_element_type=F32)


def _rope_slab(x, cos_t, sin_t, take_up, half):
    up = pltpu.roll(x, LANES - half, 1)
    dn = pltpu.roll(x, half, 1)
    return x * cos_t + jnp.where(take_up, up, dn) * sin_t


def _inproj_kernel(x_ref, g_ref, wqk_ref, wv_ref, wc_ref, wkpe_ref, wg_ref, bg_ref,
                   ca_ref, sa_ref, cb_ref, sb_ref, *rest):
    dsa_refs = rest[:9]
    cq_ref, ckv_ref, kpe_ref, gate_ref, split_ref = rest[9:]
    h = _rms(x_ref[...], g_ref[...]).astype(BF16)
    rows = h.shape[0]
    lane = lax.broadcasted_iota(jnp.int32, (rows, LANES), 1)
    up_a = (lane % DSA_HEAD_DIM) < (DSA_ROT // 2)
    ca = ca_ref[...]
    sa = sa_ref[...]
    zqk = _dot(h, wqk_ref[...])
    zv = _dot(h, wv_ref[...])
    slot = 0
    for g, dil in enumerate(DSA_DILATIONS):
        for part in range(3):
            if part < 2:
                first = (part * DSA_QK_COLS // 2 + g * DSA_GROUP_COLS) // LANES
                val = jnp.concatenate(
                    [_rope_slab(zqk[:, s * LANES:(s + 1) * LANES], ca, sa, up_a, DSA_ROT // 2)
                     for s in range(first, first + DSA_GROUP_COLS // LANES)], axis=1)
            else:
                val = zv[:, g * DSA_GROUP_COLS:(g + 1) * DSA_GROUP_COLS]
            out_ref = dsa_refs[3 * g + part]
            if dil == 1:
                out_ref[0] = val.astype(BF16)
            else:
                for c in range(DSA_GROUP_COLS // LANES):
                    cs = slice(c * LANES, (c + 1) * LANES)
                    split_ref[slot, c] = val[:, cs]
                    for r in range(dil):
                        out_ref[r, :, cs] = split_ref[slot, c, pl.ds(r, rows // dil, stride=dil), :].astype(BF16)
                slot += 1
    c = _dot(h, wc_ref[...])
    cq_ref[...] = c[:, :MLA_Q_LORA]
    ckv_ref[...] = c[:, MLA_Q_LORA:]
    up_b = (lane >= MLA_NOPE) & (lane < MLA_NOPE + MLA_ROPE // 2)
    kpe_ref[...] = _rope_slab(_dot(h, wkpe_ref[...]), cb_ref[...], sb_ref[...], up_b, MLA_ROPE // 2)
    gate_ref[...] = jax.nn.sigmoid(_dot(h, wg_ref[...]) + bg_ref[...])


def _inproj(x2d, batch, seq, w, tabs, tm):
    t = x2d.shape[0]
    nseq = seq // tm
    row = lambda i: (i, 0)
    tab = lambda i: (i % nseq, 0)
    dsa_shapes, dsa_specs = [], []
    for dil in DSA_DILATIONS:
        for _ in range(3):
            dsa_shapes.append(jax.ShapeDtypeStruct((batch, dil, seq // dil, DSA_GROUP_COLS), BF16))
            dsa_specs.append(pl.BlockSpec((None, dil, tm // dil, DSA_GROUP_COLS),
                                          lambda i: (i // nseq, 0, i % nseq, 0)))
    flat_shapes = (
        jax.ShapeDtypeStruct((t, MLA_Q_LORA), F32),
        jax.ShapeDtypeStruct((t, MLA_KV_LORA), F32),
        jax.ShapeDtypeStruct((t, MLA_SLAB), F32),
        jax.ShapeDtypeStruct((t, 2 * D_MODEL), F32),
    )
    split_slots = 3 * sum(1 for dil in DSA_DILATIONS if dil > 1)
    return pl.pallas_call(
        _inproj_kernel,
        grid=(t // tm,),
        in_specs=[
            pl.BlockSpec((tm, D_MODEL), row),
            _const_spec((1, D_MODEL)),
            _const_spec(w['wqk'].shape), _const_spec(w['wv'].shape), _const_spec(w['wc'].shape),
            _const_spec(w['wkpe'].shape), _const_spec(w['wgate'].shape), _const_spec((1, 2 * D_MODEL)),
            pl.BlockSpec((tm, LANES), tab), pl.BlockSpec((tm, LANES), tab),
            pl.BlockSpec((tm, LANES), tab), pl.BlockSpec((tm, LANES), tab),
        ],
        out_specs=dsa_specs + [pl.BlockSpec((tm, s.shape[1]), row) for s in flat_shapes],
        out_shape=tuple(dsa_shapes) + flat_shapes,
        scratch_shapes=[pltpu.VMEM((split_slots, DSA_GROUP_COLS // LANES, tm, LANES), F32)],
        compiler_params=_params(("parallel",)),
        name="inproj",
    )(x2d, w['g_mix'], w['wqk'], w['wv'], w['wc'], w['wkpe'], w['wgate'], w['b_gate'],
      tabs['ca'], tabs['sa'], tabs['cb'], tabs['sb'])


def _mla_proj_kernel(cq_ref, ckv_ref, kpe_ref, gqa_ref, gkva_ref, wq_ref, wk_ref, wv_ref,
                     cb_ref, sb_ref, q_ref, k_ref, v_ref):
    qn = _rms(cq_ref[...], gqa_ref[...]).astype(BF16)
    rows = qn.shape[0]
    lane = lax.broadcasted_iota(jnp.int32, (rows, LANES), 1)
    up_b = (lane >= MLA_NOPE) & (lane < MLA_NOPE + MLA_ROPE // 2)
    cb = cb_ref[...]
    sb = sb_ref[...]
    q = _dot(qn, wq_ref[...])
    kvn = _rms(ckv_ref[...], gkva_ref[...]).astype(BF16)
    k = _dot(kvn, wk_ref[...])
    kpe = kpe_ref[...]
    scale = MLA_QK ** -0.5 * LOG2_E
    for s in range(MLA_HEADS):
        sl = slice(s * LANES, (s + 1) * LANES)
        q_ref[:, sl] = (_rope_slab(q[:, sl], cb, sb, up_b, MLA_ROPE // 2) * scale).astype(BF16)
        k_ref[:, sl] = (k[:, sl] + kpe).astype(BF16)
    wide_lane = lax.broadcasted_iota(jnp.int32, (1, MLA_HEADS * MLA_SLAB), 1)
    one_lane = jnp.where(wide_lane % MLA_SLAB == MLA_ONE_LANE, 1.0, 0.0)
    v_ref[...] = (_dot(kvn, wv_ref[...]) + one_lane).astype(BF16)


def _mla_proj(cq, ckv, kpe, seq, w, tabs, tm):
    t = cq.shape[0]
    nseq = seq // tm
    row = lambda i: (i, 0)
    tab = lambda i: (i % nseq, 0)
    wide = MLA_HEADS * MLA_SLAB
    out_shape = tuple(jax.ShapeDtypeStruct((t, wide), BF16) for _ in range(3))
    return pl.pallas_call(
        _mla_proj_kernel,
        grid=(t // tm,),
        in_specs=[
            pl.BlockSpec((tm, MLA_Q_LORA), row), pl.BlockSpec((tm, MLA_KV_LORA), row),
            pl.BlockSpec((tm, MLA_SLAB), row),
            _const_spec((1, MLA_Q_LORA)), _const_spec((1, MLA_KV_LORA)),
            _const_spec(w['wq_mla'].shape), _const_spec(w['wk_mla'].shape), _const_spec(w['wv_mla'].shape),
            pl.BlockSpec((tm, LANES), tab), pl.BlockSpec((tm, LANES), tab),
        ],
        out_specs=[pl.BlockSpec((tm, wide), row) for _ in range(3)],
        out_shape=out_shape,
        compiler_params=_params(("parallel",)),
        name="mla_proj",
    )(cq, ckv, kpe, w['g_qa'], w['g_kva'], w['wq_mla'], w['wk_mla'], w['wv_mla'], tabs['cb'], tabs['sb'])


def _mla_attn_kernel(q_ref, k_ref, v_ref, o_ref, m_ref, acc_ref):
    j = pl.program_id(3)

    @pl.when(j == 0)
    def _():
        m_ref[...] = jnp.full(m_ref.shape, -jnp.inf, F32)
        acc_ref[...] = jnp.zeros(acc_ref.shape, F32)

    tk = k_ref.shape[0]
    for h in range(MLA_HEADS_PER_STEP):
        sl = slice(h * MLA_SLAB, (h + 1) * MLA_SLAB)
        q = q_ref[:, sl]
        m_run = m_ref[:, sl]
        acc = acc_ref[:, sl]
        for c in range(tk // MLA_KEY_CHUNK):
            rows = slice(c * MLA_KEY_CHUNK, (c + 1) * MLA_KEY_CHUNK)
            s = _dot_nt(q, k_ref[rows, sl])
            cols = [s[:, i * LANES:(i + 1) * LANES] for i in range(MLA_KEY_CHUNK // LANES)]
            m_loc = functools.reduce(jnp.maximum, cols)
            m_new = jnp.maximum(m_run, jnp.max(m_loc, axis=1, keepdims=True))
            p = jnp.concatenate([jnp.exp2(col - m_new) for col in cols], axis=1).astype(BF16)
            acc = jnp.exp2(m_run - m_new) * acc + _dot(p, v_ref[rows, sl])
            m_run = m_new
        acc_ref[:, sl] = acc
        m_ref[:, sl] = m_run

    @pl.when(j == pl.num_programs(3) - 1)
    def _():
        for h in range(MLA_HEADS_PER_STEP):
            sl = slice(h * MLA_SLAB, (h + 1) * MLA_SLAB)
            acc = acc_ref[:, sl]
            den = acc[:, MLA_ONE_LANE:MLA_ONE_LANE + 1]
            o_ref[:, sl] = (acc / den).astype(o_ref.dtype)


def _mla_attn(q, k, v, tq, tk):
    b, s, wide = q.shape
    cols = MLA_HEADS_PER_STEP * MLA_SLAB
    return pl.pallas_call(
        _mla_attn_kernel,
        grid=(b, MLA_HEADS // MLA_HEADS_PER_STEP, s // tq, s // tk),
        in_specs=[
            pl.BlockSpec((None, tq, cols), lambda bi, h, i, j: (bi, i, h)),
            pl.BlockSpec((None, tk, cols), lambda bi, h, i, j: (bi, j, h)),
            pl.BlockSpec((None, tk, cols), lambda bi, h, i, j: (bi, j, h)),
        ],
        out_specs=pl.BlockSpec((None, tq, cols), lambda bi, h, i, j: (bi, i, h)),
        out_shape=jax.ShapeDtypeStruct((b, s, wide), BF16),
        scratch_shapes=[pltpu.VMEM((tq, cols), F32), pltpu.VMEM((tq, cols), F32)],
        compiler_params=_params(("parallel", "parallel", "parallel", "arbitrary")),
        name="mla_attn",
    )(q, k, v)


def _dsa_kernel(q_ref, kp_ref, kc_ref, kn_ref, vp_ref, vc_ref, vn_ref, o_ref, lse_ref, *, cls_len):
    j = pl.program_id(2)
    tq = q_ref.shape[0]
    nk = tq + 2 * DSA_HALF
    q = q_ref[...]
    kk = jnp.concatenate([kp_ref[...], kc_ref[...], kn_ref[...]], axis=0)
    vv = jnp.concatenate([vp_ref[...], vc_ref[...], vn_ref[...]], axis=0)
    qpos = j * tq + lax.broadcasted_iota(jnp.int32, (tq, nk), 0)
    kpos = j * tq - DSA_HALF + lax.broadcasted_iota(jnp.int32, (tq, nk), 1)
    ok = (jnp.abs(kpos - qpos) <= DSA_HALF) & (kpos >= 0) & (kpos < cls_len)
    lane = lax.broadcasted_iota(jnp.int32, (tq, DSA_GROUP_COLS), 1)
    heads = range(DSA_HEADS)
    in_head = [(lane // DSA_HEAD_DIM) == h for h in heads]
    s = [jnp.where(ok, _dot_nt(jnp.where(in_head[h], q, jnp.zeros_like(q)), kk), NEG_INF) for h in heads]
    m = [jnp.max(s[h], axis=1, keepdims=True) for h in heads]
    p = [jnp.exp(s[h] - m[h]) for h in heads]
    den = [jnp.sum(p[h], axis=1, keepdims=True) for h in heads]
    oh = [_dot((p[h] / den[h]).astype(BF16), vv) for h in heads]
    o = jnp.zeros((tq, DSA_GROUP_COLS), F32)
    lse = jnp.zeros((tq, DSA_GROUP_COLS), F32)
    for h in heads:
        o = jnp.where(in_head[h], oh[h], o)
        lse = jnp.where(in_head[h], m[h] + jnp.log(den[h]), lse)
    o_ref[...] = o
    lse_ref[...] = lse


def _dsa_group(q, k, v, g, tq):
    b, dil, cls_len, _ = q.shape
    tq = min(tq, cls_len)
    per = tq // DSA_HALF
    last = cls_len // DSA_HALF - 1
    prev = lambda j: jnp.maximum(j * per - 1, 0)
    nxt = lambda j: jnp.minimum((j + 1) * per, last)
    halo = (None, None, DSA_HALF, DSA_GROUP_COLS)
    cur = (None, None, tq, DSA_GROUP_COLS)
    cur_spec = pl.BlockSpec(cur, lambda bi, r, j: (bi, r, j, 0))
    prev_spec = pl.BlockSpec(halo, lambda bi, r, j: (bi, r, prev(j), 0))
    next_spec = pl.BlockSpec(halo, lambda bi, r, j: (bi, r, nxt(j), 0))
    out_sd = jax.ShapeDtypeStruct((b, dil, cls_len, DSA_GROUP_COLS), F32)
    return pl.pallas_call(
        functools.partial(_dsa_kernel, cls_len=cls_len),
        grid=(b, dil, cls_len // tq),
        in_specs=[cur_spec, prev_spec, cur_spec, next_spec, prev_spec, cur_spec, next_spec],
        out_specs=[cur_spec, cur_spec],
        out_shape=(out_sd, out_sd),
        compiler_params=_params(("parallel", "parallel", "parallel")),
        name=f"dsa_g{g}",
    )(q, k, k, k, v, v, v)


def _merge_kernel(o0_ref, l0_ref, o1_ref, l1_ref, o2_ref, l2_ref, yb_ref, gate_ref, x_ref,
                  wpa_ref, wpb_ref, wout_ref, out_ref, join_ref):
    def position_order(ref, slot):
        dil, per_class, _ = ref.shape
        if dil == 1:
            return ref[0]
        halves = []
        for c in range(DSA_GROUP_COLS // LANES):
            for r in range(dil):
                join_ref[slot, c, pl.ds(r, per_class, stride=dil), :] = ref[r, :, c * LANES:(c + 1) * LANES]
            halves.append(join_ref[slot, c])
        return jnp.concatenate(halves, axis=1)

    o0, l0 = position_order(o0_ref, 0), position_order(l0_ref, 0)
    o1, l1 = position_order(o1_ref, 0), position_order(l1_ref, 1)
    o2, l2 = position_order(o2_ref, 2), position_order(l2_ref, 3)
    m = jnp.maximum(jnp.maximum(l0, l1), l2)
    e0, e1, e2 = jnp.exp(l0 - m), jnp.exp(l1 - m), jnp.exp(l2 - m)
    den = e0 + e1 + e2
    ya = (e0 / den) * o0 + (e1 / den) * o1 + (e2 / den) * o2
    gate = gate_ref[...]
    pa = _dot(ya.astype(BF16), wpa_ref[...])
    pb = _dot(yb_ref[...], wpb_ref[...])
    merged = gate[:, :D_MODEL] * pa + gate[:, D_MODEL:] * pb
    out_ref[...] = x_ref[...] + _dot(merged.astype(BF16), wout_ref[...])


def _merge(dsa_out, yb, gates, x2d, seq, w, tm):
    t = x2d.shape[0]
    nseq = seq // tm
    row = lambda i: (i, 0)
    flat, specs = [], []
    for pair in dsa_out:
        for a in pair:
            dil = a.shape[1]
            flat.append(a)
            specs.append(pl.BlockSpec((None, dil, tm // dil, DSA_GROUP_COLS),
                                      lambda i: (i // nseq, 0, i % nseq, 0)))
    return pl.pallas_call(
        _merge_kernel,
        grid=(t // tm,),
        in_specs=specs + [
            pl.BlockSpec((tm, MLA_HEADS * MLA_SLAB), row),
            pl.BlockSpec((tm, 2 * D_MODEL), row),
            pl.BlockSpec((tm, D_MODEL), row),
            _const_spec(w['wpa'].shape), _const_spec(w['wpb'].shape), _const_spec(w['wout'].shape),
        ],
        out_specs=pl.BlockSpec((tm, D_MODEL), row),
        out_shape=jax.ShapeDtypeStruct((t, D_MODEL), F32),
        scratch_shapes=[pltpu.VMEM((4, DSA_GROUP_COLS // LANES, tm, LANES), F32)],
        compiler_params=_params(("parallel",)),
        name="merge",
    )(*flat, yb, gates, x2d, w['wpa'], w['wpb'], w['wout'])


def _norm_mm_kernel(x_ref, g_ref, w_ref, o_ref):
    o_ref[...] = _dot(_rms(x_ref[...], g_ref[...]).astype(BF16), w_ref[...]).astype(o_ref.dtype)


def _norm_mm(x2d, g, wmat, tm):
    t, d = x2d.shape
    n = wmat.shape[1]
    return pl.pallas_call(
        _norm_mm_kernel,
        grid=(t // tm,),
        in_specs=[pl.BlockSpec((tm, d), lambda i: (i, 0)), _const_spec((1, d)), _const_spec(wmat.shape)],
        out_specs=pl.BlockSpec((tm, n), lambda i: (i, 0)),
        out_shape=jax.ShapeDtypeStruct((t, n), BF16),
        compiler_params=_params(("parallel",)),
        name="mem_kv",
    )(x2d, g, wmat)


def _xattn_kernel(x_ref, g_ref, kv_ref, wq_ref, wo_ref, out_ref):
    x = x_ref[...]
    hq = _rms(x, g_ref[...]).astype(BF16)
    q = (_dot(hq, wq_ref[...]) * (MEM_HEAD_DIM ** -0.5)).astype(BF16)
    kv = kv_ref[...]
    outs = []
    for h in range(MEM_HEADS):
        sl = slice(h * MEM_HEAD_DIM, (h + 1) * MEM_HEAD_DIM)
        vsl = slice(D_MODEL + h * MEM_HEAD_DIM, D_MODEL + (h + 1) * MEM_HEAD_DIM)
        s = _dot_nt(q[:, sl], kv[:, sl])
        m = jnp.max(s, axis=1, keepdims=True)
        p = jnp.exp(s - m)
        den = jnp.sum(p, axis=1, keepdims=True)
        outs.append(_dot((p / den).astype(BF16), kv[:, vsl]).astype(BF16))
    o = jnp.concatenate(outs, axis=1)
    out_ref[...] = x + _dot(o, wo_ref[...])


def _xattn(x3d, kv3d, w, tm):
    b, s, d = x3d.shape
    n_mem = kv3d.shape[1]
    return pl.pallas_call(
        _xattn_kernel,
        grid=(b, s // tm),
        in_specs=[
            pl.BlockSpec((None, tm, d), lambda bi, i: (bi, i, 0)),
            _const_spec((1, d)),
            pl.BlockSpec((None, n_mem, 2 * d), lambda bi, i: (bi, 0, 0)),
            _const_spec(w['wmq'].shape), _const_spec(w['wmo'].shape),
        ],
        out_specs=pl.BlockSpec((None, tm, d), lambda bi, i: (bi, i, 0)),
        out_shape=jax.ShapeDtypeStruct((b, s, d), F32),
        compiler_params=_params(("parallel", "parallel")),
        name="xattn",
    )(x3d, w['g_xq'], kv3d, w['wmq'], w['wmo'])


def _top_values(sc, count, with_rank, break_ties):
    n = sc.shape[0]
    row = lax.broadcasted_iota(jnp.int32, sc.shape, 0)
    rank = jnp.full(sc.shape, float(n - 1), F32)
    vals = []
    for k in range(count):
        m = jnp.max(sc, axis=0, keepdims=True)
        hit = sc == m
        if break_ties:
            hit = row == jnp.min(jnp.where(hit, row, n), axis=0, keepdims=True)
        vals.append(m)
        if with_rank:
            rank = jnp.where(hit, float(k), rank)
        sc = jnp.where(hit, -jnp.inf, sc)
    dropped = jnp.sum(jnp.where(sc == -jnp.inf, 1.0, 0.0), axis=0, keepdims=True)
    return jnp.concatenate(vals, axis=0), rank, dropped


def _pack_bf16_rows(x):
    return pltpu.bitcast(x.astype(BF16), jnp.int32)


def _unpack_bf16_rows(words):
    return pltpu.bitcast(words, BF16)


def _bf16_pair_word(x):
    bits = lax.bitcast_convert_type(x.astype(BF16).astype(F32), jnp.uint32)
    return lax.bitcast_convert_type(bits | (bits >> 16), jnp.int32)


def _peer_score_kernel(x_ref, g_ref, wpq_ref, keys_ref, hqt_ref, r2s_ref, cnts_ref, e1s_ref, e2s_ref):
    hq32 = _rms(x_ref[...], g_ref[...])

    @pl.when(pl.program_id(1) == 0)
    def _():
        hqt_ref[...] = _pack_bf16_rows(hq32.T)

    q = _dot(hq32.astype(BF16), wpq_ref[...]).astype(BF16)
    heads = keys_ref.shape[0]
    halves = [[_dot_nt(keys_ref[h, p], q[:, (2 * h + p) * PEER_HALF:(2 * h + p + 1) * PEER_HALF])
               for p in range(2)] for h in range(heads)]

    def statistics(break_ties):
        worst = [head_statistics(h, break_ties) for h in range(heads)]
        return functools.reduce(jnp.maximum, worst)

    def head_statistics(h, break_ties):
        s1, s2 = halves[h]
        r2_ref, cnt_ref, e1_ref, e2_ref = (ref.at[h] for ref in (r2s_ref, cnts_ref, e1s_ref, e2s_ref))
        v1, _, d1 = _top_values(s1, PEER_TOPK, False, break_ties)
        v2, r2, d2 = _top_values(s2, PEER_TOPK, True, break_ties)
        cand = [v1[0:1] + v2]
        cand += [v1[i:i + 1] + v2[0:8] for i in range(1, 8)]
        cand += [v1[8:16] + v2[0:1]]
        top, _, d3 = _top_values(jnp.concatenate(cand, axis=0), PEER_TOPK, False, break_ties)
        z = jnp.sum(jnp.exp(top - top[0:1]), axis=0, keepdims=True)
        tau = top[PEER_TOPK - 1:PEER_TOPK]
        cnt = jnp.zeros(s1.shape, F32)
        for r in range(PEER_TOPK):
            cnt = cnt + jnp.where(s1 + v2[r:r + 1] >= tau, 1.0, 0.0)
        r2_ref[...] = _pack_bf16_rows(r2)
        cnt_ref[...] = _bf16_pair_word(cnt)
        e1_ref[...] = _bf16_pair_word(jnp.exp(s1 - v1[0:1]) * (0.5 / z))
        e2_ref[...] = _pack_bf16_rows(jnp.exp(s2 - v2[0:1]))
        return jnp.maximum(jnp.maximum(d1, d2), d3)

    most_dropped = statistics(break_ties=False)

    @pl.when(jnp.max(most_dropped) > PEER_TOPK)
    def _():
        statistics(break_ties=True)


def _peer_scores(x2d, w):
    t = x2d.shape[0]
    tt = PEER_TOKENS
    groups = t // tt
    words = lambda rows: jax.ShapeDtypeStruct((groups, PEER_HEADS, rows, tt), jnp.int32)
    hs = PEER_SCORE_HEADS
    spec = lambda rows: pl.BlockSpec((None, hs, rows, tt), lambda i, h: (i, h, 0, 0))
    half = PEER_NKEYS // 2
    return pl.pallas_call(
        _peer_score_kernel,
        grid=(groups, PEER_HEADS // hs),
        in_specs=[
            pl.BlockSpec((tt, D_MODEL), lambda i, h: (i, 0)),
            _const_spec((1, D_MODEL)),
            pl.BlockSpec((D_MODEL, hs * 2 * PEER_HALF), lambda i, h: (0, h)),
            pl.BlockSpec((hs, 2, PEER_NKEYS, PEER_HALF), lambda i, h: (h, 0, 0, 0)),
        ],
        out_specs=[pl.BlockSpec((None, D_MODEL // 2, tt), lambda i, h: (i, 0, 0)),
                   spec(half), spec(PEER_NKEYS), spec(PEER_NKEYS), spec(half)],
        out_shape=(jax.ShapeDtypeStruct((groups, D_MODEL // 2, tt), jnp.int32),
                   words(half), words(PEER_NKEYS), words(PEER_NKEYS), words(half)),
        compiler_params=_params(("parallel", "arbitrary")),
        name="peer_scores",
    )(x2d, w['g_ffn'], w['wpq'], w['keys'])


def _peer_mix_kernel(hqt_ref, u_ref, vt_ref, r2_ref, cnt_ref, e1_ref, e2_ref, x_ref, gf_ref,
                     out_ref, acc_ref, wg_ref, *, blocks):
    j = pl.program_id(1)
    last = pl.num_programs(1) - 1
    groups, _, tt = hqt_ref.shape
    half_rows = PEER_NKEYS // 2

    @pl.when(j == 0)
    def _():
        acc_ref[...] = jnp.zeros(acc_ref.shape, F32)
        wg_ref[...] = jnp.zeros(wg_ref.shape, jnp.int32)

    def mix_previous_block(g):
        acc_ref[g] += _dot(_unpack_bf16_rows(vt_ref[...]), _unpack_bf16_rows(wg_ref[g]))

    def row_bcast(words):
        return _unpack_bf16_rows(jnp.broadcast_to(words, (half_rows, LANES)))

    def activate_and_mix(g, carry):
        mix_previous_block(g)
        hqt = _unpack_bf16_rows(hqt_ref[g])
        for c in range(blocks // 2):
            u_pair = _unpack_bf16_rows(u_ref[c * PEER_NKEYS:(c + 1) * PEER_NKEYS, :])
            pre2 = _dot(u_pair, hqt)
            for half in range(2):
                i = 2 * c + half
                a = j * blocks + i
                cnt_rows = [cnt_ref[g, h, pl.ds(a, 1), :] for h in range(PEER_HEADS)]
                e1_rows = [e1_ref[g, h, pl.ds(a, 1), :] for h in range(PEER_HEADS)]
                for col in range(tt // LANES):
                    cs = slice(col * LANES, (col + 1) * LANES)
                    wsum = jnp.zeros((PEER_NKEYS, LANES), BF16)
                    for h in range(PEER_HEADS):
                        chosen = _unpack_bf16_rows(r2_ref[g, h, :, cs]) < row_bcast(cnt_rows[h][:, cs])
                        picked = jnp.where(chosen, _unpack_bf16_rows(e2_ref[g, h, :, cs]), jnp.zeros((), BF16))
                        wsum = wsum + picked * row_bcast(e1_rows[h][:, cs])
                    pre = pre2[half * PEER_NKEYS:(half + 1) * PEER_NKEYS, cs]
                    act = pre * (1.0 + lax.erf(pre * (0.5 ** 0.5)))
                    wg_ref[g, i * half_rows:(i + 1) * half_rows, cs] = pltpu.bitcast(
                        wsum * act.astype(BF16), jnp.int32)
        return carry

    @pl.when(j < last)
    def _():
        for g in range(groups):
            activate_and_mix(g, 0)

    @pl.when(j == last)
    def _():
        for g in range(groups):
            mix_previous_block(g)
            rows = slice(g * tt, (g + 1) * tt)
            out_ref[rows, :] = _rms(x_ref[rows, :] + acc_ref[g].T, gf_ref[...])


def _peer_mix(hqt, stats, x2d, w, groups, blocks):
    t = x2d.shape[0]
    tt = PEER_TOKENS
    rows = groups * tt
    ne = blocks * PEER_NKEYS
    nblk = PEER_NKEYS // blocks
    spec = lambda r: pl.BlockSpec((groups, PEER_HEADS, r, tt), lambda i, j: (i, 0, 0, 0))
    half = PEER_NKEYS // 2
    return pl.pallas_call(
        functools.partial(_peer_mix_kernel, blocks=blocks),
        grid=(t // rows, nblk + 1),
        in_specs=[
            pl.BlockSpec((groups, D_MODEL // 2, tt), lambda i, j: (i, 0, 0)),
            pl.BlockSpec((ne // 2, D_MODEL), lambda i, j: (jnp.minimum(j, nblk - 1), 0)),
            pl.BlockSpec((D_MODEL // 2, ne), lambda i, j: (0, jnp.maximum(j - 1, 0))),
            spec(half), spec(PEER_NKEYS), spec(PEER_NKEYS), spec(half),
            pl.BlockSpec((rows, D_MODEL), lambda i, j: (i, 0)),
            _const_spec((1, D_MODEL)),
        ],
        out_specs=pl.BlockSpec((rows, D_MODEL), lambda i, j: (i, 0)),
        out_shape=jax.ShapeDtypeStruct((t, D_MODEL), F32),
        scratch_shapes=[pltpu.VMEM((groups, D_MODEL, tt), F32), pltpu.VMEM((groups, ne // 2, tt), jnp.int32)],
        compiler_params=_params(("parallel", "arbitrary")),
        name="peer_mix",
    )(hqt, w['peer_u'], w['peer_vt'], *stats, x2d, w['g_final'])


def _pack_rows_host(a):
    m2, n = a.shape
    return lax.bitcast_convert_type(jnp.swapaxes(a.reshape(m2 // 2, 2, n), 1, 2), jnp.int32)


def _prep_weights(g_mix, w_in, g_qa, g_kva, w_uq, w_ukv, w_gate, b_gate, w_pa, w_pb, w_out,
                  g_xq, g_mkv, w_mq, w_mkv, w_mo, g_ffn, w_pq, sub_keys, peer_u, peer_v, g_final):
    w_in = w_in[0]
    q_cols = DSA_QK_COLS // 2
    wqk = jnp.concatenate([w_in[:, :q_cols] * (DSA_HEAD_DIM ** -0.5), w_in[:, q_cols:DSA_QK_COLS]], axis=1)
    c0 = DSA_QK_COLS + DSA_V_COLS
    c1 = c0 + MLA_Q_LORA + MLA_KV_LORA
    wkpe = jnp.zeros((D_MODEL, MLA_SLAB), F32).at[:, MLA_NOPE:MLA_QK].set(w_in[:, c1:])
    pad_head = lambda a: jnp.pad(a, ((0, 0), (0, 0), (0, MLA_SLAB - a.shape[2])))
    wq = pad_head(w_uq[0].reshape(MLA_Q_LORA, MLA_HEADS, MLA_QK)).reshape(MLA_Q_LORA, -1)
    ukv = w_ukv[0].reshape(MLA_KV_LORA, MLA_HEADS, MLA_NOPE + MLA_V)
    wk = pad_head(ukv[:, :, :MLA_NOPE]).reshape(MLA_KV_LORA, -1)
    wv = pad_head(ukv[:, :, MLA_NOPE:]).reshape(MLA_KV_LORA, -1)
    wpb = jnp.pad(w_pb[0].reshape(MLA_HEADS, MLA_V, D_MODEL), ((0, 0), (0, MLA_SLAB - MLA_V), (0, 0)))
    bf = lambda a: a.astype(BF16)
    return {
        'g_mix': g_mix, 'wqk': bf(wqk), 'wv': bf(w_in[:, DSA_QK_COLS:c0]), 'wc': bf(w_in[:, c0:c1]),
        'wkpe': bf(wkpe), 'wgate': bf(w_gate[0]), 'b_gate': b_gate,
        'g_qa': g_qa, 'g_kva': g_kva, 'wq_mla': bf(wq), 'wk_mla': bf(wk), 'wv_mla': bf(wv),
        'wpa': bf(w_pa[0]), 'wpb': bf(wpb.reshape(-1, D_MODEL)), 'wout': bf(w_out[0]),
        'g_xq': g_xq, 'g_mkv': g_mkv, 'wmq': bf(w_mq[0]), 'wmkv': bf(w_mkv[0]), 'wmo': bf(w_mo[0]),
        'g_ffn': g_ffn, 'wpq': bf(w_pq[0]),
        'keys': bf(sub_keys[0]),
        'peer_u': _pack_rows_host(bf(peer_u[0])), 'peer_vt': _pack_rows_host(bf(peer_v[0]).T),
        'g_final': g_final.reshape(1, D_MODEL),
    }


def _rope_tables(seq):
    def table(rot, lead, period):
        inv = ROPE_THETA ** (-jnp.arange(0, rot, 2, dtype=F32) / rot)
        ang = jnp.arange(seq, dtype=F32)[:, None] * inv[None, :]
        cos, sin = jnp.cos(ang), jnp.sin(ang)
        ones = lambda n: jnp.ones((seq, n), F32)
        zeros = lambda n: jnp.zeros((seq, n), F32)
        tail = period - lead - rot
        c = jnp.concatenate([ones(lead), cos, cos, ones(tail)], axis=1)
        s = jnp.concatenate([zeros(lead), -sin, sin, zeros(tail)], axis=1)
        reps = LANES // period
        return jnp.tile(c, (1, reps)), jnp.tile(s, (1, reps))
    ca, sa = table(DSA_ROT, 0, DSA_HEAD_DIM)
    cb, sb = table(MLA_ROPE, MLA_NOPE, MLA_SLAB)
    return {'ca': ca, 'sa': sa, 'cb': cb, 'sb': sb}


def _trunk(x, mem, w):
    b, s, d = x.shape
    t = b * s
    tabs = _rope_tables(s)
    x2d = x.reshape(t, d)
    tm = 512
    *dsa_in, cq, ckv, kpe, gates = _inproj(x2d, b, s, w, tabs, tm)
    dsa_out = [_dsa_group(*dsa_in[3 * g:3 * g + 3], g, 256) for g in range(len(DSA_DILATIONS))]
    q_b, k_b, v_b = _mla_proj(cq, ckv, kpe, s, w, tabs, tm)
    wide = MLA_HEADS * MLA_SLAB
    yb = _mla_attn(q_b.reshape(b, s, wide), k_b.reshape(b, s, wide), v_b.reshape(b, s, wide),
                   min(2048, s), min(2048, s))
    x1 = _merge(dsa_out, yb.reshape(t, wide), gates, x2d, s, w, tm)
    n_mem = mem.shape[1]
    kv = _norm_mm(mem.reshape(b * n_mem, d), w['g_mkv'], w['wmkv'], n_mem)
    x2 = _xattn(x1.reshape(b, s, d), kv.reshape(b, n_mem, 2 * d), w, tm)
    x2 = x2.reshape(t, d)
    hqt, *stats = _peer_scores(x2, w)
    y = _peer_mix(hqt, stats, x2, w, 2, 16)
    return y.reshape(b, s, d)


def kernel(x_prompt, x_sample, mem_prompt, mem_sample, g_mix, w_in, g_qa, g_kva, w_uq, w_ukv, w_gate,
           b_gate, w_pa, w_pb, w_out, g_xq, g_mkv, w_mq, w_mkv, w_mo, g_ffn, w_pq, sub_keys, peer_u,
           peer_v, g_final):
    w = _prep_weights(g_mix, w_in, g_qa, g_kva, w_uq, w_ukv, w_gate, b_gate, w_pa, w_pb, w_out,
                      g_xq, g_mkv, w_mq, w_mkv, w_mo, g_ffn, w_pq, sub_keys, peer_u, peer_v, g_final)
    return (_trunk(x_prompt, mem_prompt, w), _trunk(x_sample, mem_sample, w))
```

```python
import functools

import jax
import jax.numpy as jnp
from jax import lax
from jax.experimental import pallas as pl
from jax.experimental.pallas import tpu as pltpu

F32 = jnp.float32
BF16 = jnp.bfloat16

D_MODEL = 1024
EPS = 1e-6
NEG_INF = -1e30
ROPE_THETA = 500000.0

DSA_DILATIONS = (1, 4, 16)
DSA_HALF = 64
DSA_HEADS = 4
DSA_HEAD_DIM = 64
DSA_ROT = 16
DSA_GROUP_COLS = DSA_HEADS * DSA_HEAD_DIM
DSA_QK_COLS = 2 * 3 * DSA_GROUP_COLS
DSA_V_COLS = 3 * DSA_GROUP_COLS

MLA_HEADS = 8
MLA_Q_LORA = 768
MLA_KV_LORA = 256
MLA_NOPE = 64
MLA_ROPE = 32
MLA_V = 64
MLA_QK = MLA_NOPE + MLA_ROPE
MLA_SLAB = 128
MLA_ONE_LANE = MLA_V
MLA_HEADS_PER_STEP = 1
MLA_KEY_CHUNK = 512
LOG2_E = 1.4426950408889634

MEM_HEADS = 4
MEM_HEAD_DIM = 256

PEER_HEADS = 8
PEER_NKEYS = 128
PEER_HALF = 128
PEER_TOPK = 16
PEER_TOKENS = 256
PEER_SCORE_HEADS = 4

LANES = 128
VMEM_LIMIT_BYTES = 56 * 1024 * 1024


def _params(sem):
    return pltpu.CompilerParams(dimension_semantics=sem, vmem_limit_bytes=VMEM_LIMIT_BYTES)


def _const_spec(shape):
    nd = len(shape)
    return pl.BlockSpec(shape, lambda *_: (0,) * nd, pipeline_mode=pl.Buffered(1))


def _rms(x, g):
    y = x * lax.rsqrt(jnp.mean(x * x, axis=-1, keepdims=True) + EPS)
    return y * g


def _dot(a, b):
    return jnp.dot(a, b, preferred_element_type=F32)


def _dot_nt(a, b):
    return lax.dot_general(a, b, (((1,), (1,)), ((), ())), preferred_element_type=F32)


def _rope_slab(x, cos_t, sin_t, take_up, half):
    up = pltpu.roll(x, LANES - half, 1)
    dn = pltpu.roll(x, half, 1)
    return x * cos_t + jnp.where(take_up, up, dn) * sin_t


def _inproj_kernel(x_ref, g_ref, wqk_ref, wv_ref, wc_ref, wkpe_ref, wg_ref, bg_ref,
                   ca_ref, sa_ref, cb_ref, sb_ref, *rest):
    dsa_refs = rest[:9]
    cq_ref, ckv_ref, kpe_ref, gate_ref, split_ref = rest[9:]
    h = _rms(x_ref[...], g_ref[...]).astype(BF16)
    rows = h.shape[0]
    lane = lax.broadcasted_iota(jnp.int32, (rows, LANES), 1)
    up_a = (lane % DSA_HEAD_DIM) < (DSA_ROT // 2)
    ca = ca_ref[...]
    sa = sa_ref[...]
    zqk = _dot(h, wqk_ref[...])
    zv = _dot(h, wv_ref[...])
    slot = 0
    for g, dil in enumerate(DSA_DILATIONS):
        for part in range(3):
            if part < 2:
                first = (part * DSA_QK_COLS // 2 + g * DSA_GROUP_COLS) // LANES
                val = jnp.concatenate(
                    [_rope_slab(zqk[:, s * LANES:(s + 1) * LANES], ca, sa, up_a, DSA_ROT // 2)
                     for s in range(first, first + DSA_GROUP_COLS // LANES)], axis=1)
            else:
                val = zv[:, g * DSA_GROUP_COLS:(g + 1) * DSA_GROUP_COLS]
            out_ref = dsa_refs[3 * g + part]
            if dil == 1:
                out_ref[0] = val.astype(BF16)
            else:
                for c in range(DSA_GROUP_COLS // LANES):
                    cs = slice(c * LANES, (c + 1) * LANES)
                    split_ref[slot, c] = val[:, cs]
                    for r in range(dil):
                        out_ref[r, :, cs] = split_ref[slot, c, pl.ds(r, rows // dil, stride=dil), :].astype(BF16)
                slot += 1
    c = _dot(h, wc_ref[...])
    cq_ref[...] = c[:, :MLA_Q_LORA]
    ckv_ref[...] = c[:, MLA_Q_LORA:]
    up_b = (lane >= MLA_NOPE) & (lane < MLA_NOPE + MLA_ROPE // 2)
    kpe_ref[...] = _rope_slab(_dot(h, wkpe_ref[...]), cb_ref[...], sb_ref[...], up_b, MLA_ROPE // 2)
    gate_ref[...] = jax.nn.sigmoid(_dot(h, wg_ref[...]) + bg_ref[...])


def _inproj(x2d, batch, seq, w, tabs, tm):
    t = x2d.shape[0]
    nseq = seq // tm
    row = lambda i: (i, 0)
    tab = lambda i: (i % nseq, 0)
    dsa_shapes, dsa_specs = [], []
    for dil in DSA_DILATIONS:
        for _ in range(3):
            dsa_shapes.append(jax.ShapeDtypeStruct((batch, dil, seq // dil, DSA_GROUP_COLS), BF16))
            dsa_specs.append(pl.BlockSpec((None, dil, tm // dil, DSA_GROUP_COLS),
                                          lambda i: (i // nseq, 0, i % nseq, 0)))
    flat_shapes = (
        jax.ShapeDtypeStruct((t, MLA_Q_LORA), F32),
        jax.ShapeDtypeStruct((t, MLA_KV_LORA), F32),
        jax.ShapeDtypeStruct((t, MLA_SLAB), F32),
        jax.ShapeDtypeStruct((t, 2 * D_MODEL), F32),
    )
    split_slots = 3 * sum(1 for dil in DSA_DILATIONS if dil > 1)
    return pl.pallas_call(
        _inproj_kernel,
        grid=(t // tm,),
        in_specs=[
            pl.BlockSpec((tm, D_MODEL), row),
            _const_spec((1, D_MODEL)),
            _const_spec(w['wqk'].shape), _const_spec(w['wv'].shape), _const_spec(w['wc'].shape),
            _const_spec(w['wkpe'].shape), _const_spec(w['wgate'].shape), _const_spec((1, 2 * D_MODEL)),
            pl.BlockSpec((tm, LANES), tab), pl.BlockSpec((tm, LANES), tab),
            pl.BlockSpec((tm, LANES), tab), pl.BlockSpec((tm, LANES), tab),
        ],
        out_specs=dsa_specs + [pl.BlockSpec((tm, s.shape[1]), row) for s in flat_shapes],
        out_shape=tuple(dsa_shapes) + flat_shapes,
        scratch_shapes=[pltpu.VMEM((split_slots, DSA_GROUP_COLS // LANES, tm, LANES), F32)],
        compiler_params=_params(("parallel",)),
        name="inproj",
    )(x2d, w['g_mix'], w['wqk'], w['wv'], w['wc'], w['wkpe'], w['wgate'], w['b_gate'],
      tabs['ca'], tabs['sa'], tabs['cb'], tabs['sb'])


def _mla_proj_kernel(cq_ref, ckv_ref, kpe_ref, gqa_ref, gkva_ref, wq_ref, wk_ref, wv_ref,
                     cb_ref, sb_ref, q_ref, k_ref, v_ref):
    qn = _rms(cq_ref[...], gqa_ref[...]).astype(BF16)
    rows = qn.shape[0]
    lane = lax.broadcasted_iota(jnp.int32, (rows, LANES), 1)
    up_b = (lane >= MLA_NOPE) & (lane < MLA_NOPE + MLA_ROPE // 2)
    cb = cb_ref[...]
    sb = sb_ref[...]
    q = _dot(qn, wq_ref[...])
    kvn = _rms(ckv_ref[...], gkva_ref[...]).astype(BF16)
    k = _dot(kvn, wk_ref[...])
    kpe = kpe_ref[...]
    scale = MLA_QK ** -0.5 * LOG2_E
    for s in range(MLA_HEADS):
        sl = slice(s * LANES, (s + 1) * LANES)
        q_ref[:, sl] = (_rope_slab(q[:, sl], cb, sb, up_b, MLA_ROPE // 2) * scale).astype(BF16)
        k_ref[:, sl] = (k[:, sl] + kpe).astype(BF16)
    wide_lane = lax.broadcasted_iota(jnp.int32, (1, MLA_HEADS * MLA_SLAB), 1)
    one_lane = jnp.where(wide_lane % MLA_SLAB == MLA_ONE_LANE, 1.0, 0.0)
    v_ref[...] = (_dot(kvn, wv_ref[...]) + one_lane).astype(BF16)


def _mla_proj(cq, ckv, kpe, seq, w, tabs, tm):
    t = cq.shape[0]
    nseq = seq // tm
    row = lambda i: (i, 0)
    tab = lambda i: (i % nseq, 0)
    wide = MLA_HEADS * MLA_SLAB
    out_shape = tuple(jax.ShapeDtypeStruct((t, wide), BF16) for _ in range(3))
    return pl.pallas_call(
        _mla_proj_kernel,
        grid=(t // tm,),
        in_specs=[
            pl.BlockSpec((tm, MLA_Q_LORA), row), pl.BlockSpec((tm, MLA_KV_LORA), row),
            pl.BlockSpec((tm, MLA_SLAB), row),
            _const_spec((1, MLA_Q_LORA)), _const_spec((1, MLA_KV_LORA)),
            _const_spec(w['wq_mla'].shape), _const_spec(w['wk_mla'].shape), _const_spec(w['wv_mla'].shape),
            pl.BlockSpec((tm, LANES), tab), pl.BlockSpec((tm, LANES), tab),
        ],
        out_specs=[pl.BlockSpec((tm, wide), row) for _ in range(3)],
        out_shape=out_shape,
        compiler_params=_params(("parallel",)),
        name="mla_proj",
    )(cq, ckv, kpe, w['g_qa'], w['g_kva'], w['wq_mla'], w['wk_mla'], w['wv_mla'], tabs['cb'], tabs['sb'])


def _mla_attn_kernel(q_ref, k_ref, v_ref, o_ref, m_ref, acc_ref):
    j = pl.program_id(3)

    @pl.when(j == 0)
    def _():
        m_ref[...] = jnp.full(m_ref.shape, -jnp.inf, F32)
        acc_ref[...] = jnp.zeros(acc_ref.shape, F32)

    tk = k_ref.shape[0]
    for h in range(MLA_HEADS_PER_STEP):
        sl = slice(h * MLA_SLAB, (h + 1) * MLA_SLAB)
        q = q_ref[:, sl]
        m_run = m_ref[:, sl]
        acc = acc_ref[:, sl]
        for c in range(tk // MLA_KEY_CHUNK):
            rows = slice(c * MLA_KEY_CHUNK, (c + 1) * MLA_KEY_CHUNK)
            s = _dot_nt(q, k_ref[rows, sl])
            cols = [s[:, i * LANES:(i + 1) * LANES] for i in range(MLA_KEY_CHUNK // LANES)]
            m_loc = functools.reduce(jnp.maximum, cols)
            m_new = jnp.maximum(m_run, jnp.max(m_loc, axis=1, keepdims=True))
            p = jnp.concatenate([jnp.exp2(col - m_new) for col in cols], axis=1).astype(BF16)
            acc = jnp.exp2(m_run - m_new) * acc + _dot(p, v_ref[rows, sl])
            m_run = m_new
        acc_ref[:, sl] = acc
        m_ref[:, sl] = m_run

    @pl.when(j == pl.num_programs(3) - 1)
    def _():
        for h in range(MLA_HEADS_PER_STEP):
            sl = slice(h * MLA_SLAB, (h + 1) * MLA_SLAB)
            acc = acc_ref[:, sl]
            den = acc[:, MLA_ONE_LANE:MLA_ONE_LANE + 1]
            o_ref[:, sl] = (acc / den).astype(o_ref.dtype)


def _mla_attn(q, k, v, tq, tk):
    b, s, wide = q.shape
    cols = MLA_HEADS_PER_STEP * MLA_SLAB
    return pl.pallas_call(
        _mla_attn_kernel,
        grid=(b, MLA_HEADS // MLA_HEADS_PER_STEP, s // tq, s // tk),
        in_specs=[
            pl.BlockSpec((None, tq, cols), lambda bi, h, i, j: (bi, i, h)),
            pl.BlockSpec((None, tk, cols), lambda bi, h, i, j: (bi, j, h)),
            pl.BlockSpec((None, tk, cols), lambda bi, h, i, j: (bi, j, h)),
        ],
        out_specs=pl.BlockSpec((None, tq, cols), lambda bi, h, i, j: (bi, i, h)),
        out_shape=jax.ShapeDtypeStruct((b, s, wide), BF16),
        scratch_shapes=[pltpu.VMEM((tq, cols), F32), pltpu.VMEM((tq, cols), F32)],
        compiler_params=_params(("parallel", "parallel", "parallel", "arbitrary")),
        name="mla_attn",
    )(q, k, v)


def _dsa_kernel(q_ref, kp_ref, kc_ref, kn_ref, vp_ref, vc_ref, vn_ref, o_ref, lse_ref, *, cls_len):
    j = pl.program_id(2)
    tq = q_ref.shape[0]
    nk = tq + 2 * DSA_HALF
    q = q_ref[...]
    kk = jnp.concatenate([kp_ref[...], kc_ref[...], kn_ref[...]], axis=0)
    vv = jnp.concatenate([vp_ref[...], vc_ref[...], vn_ref[...]], axis=0)
    qpos = j * tq + lax.broadcasted_iota(jnp.int32, (tq, nk), 0)
    kpos = j * tq - DSA_HALF + lax.broadcasted_iota(jnp.int32, (tq, nk), 1)
    ok = (jnp.abs(kpos - qpos) <= DSA_HALF) & (kpos >= 0) & (kpos < cls_len)
    lane = lax.broadcasted_iota(jnp.int32, (tq, DSA_GROUP_COLS), 1)
    heads = range(DSA_HEADS)
    in_head = [(lane // DSA_HEAD_DIM) == h for h in heads]
    s = [jnp.where(ok, _dot_nt(jnp.where(in_head[h], q, jnp.zeros_like(q)), kk), NEG_INF) for h in heads]
    m = [jnp.max(s[h], axis=1, keepdims=True) for h in heads]
    p = [jnp.exp(s[h] - m[h]) for h in heads]
    den = [jnp.sum(p[h], axis=1, keepdims=True) for h in heads]
    oh = [_dot((p[h] / den[h]).astype(BF16), vv) for h in heads]
    o = jnp.zeros((tq, DSA_GROUP_COLS), F32)
    lse = jnp.zeros((tq, DSA_GROUP_COLS), F32)
    for h in heads:
        o = jnp.where(in_head[h], oh[h], o)
        lse = jnp.where(in_head[h], m[h] + jnp.log(den[h]), lse)
    o_ref[...] = o
    lse_ref[...] = lse


def _dsa_group(q, k, v, g, tq):
    b, dil, cls_len, _ = q.shape
    tq = min(tq, cls_len)
    per = tq // DSA_HALF
    last = cls_len // DSA_HALF - 1
    prev = lambda j: jnp.maximum(j * per - 1, 0)
    nxt = lambda j: jnp.minimum((j + 1) * per, last)
    halo = (None, None, DSA_HALF, DSA_GROUP_COLS)
    cur = (None, None, tq, DSA_GROUP_COLS)
    cur_spec = pl.BlockSpec(cur, lambda bi, r, j: (bi, r, j, 0))
    prev_spec = pl.BlockSpec(halo, lambda bi, r, j: (bi, r, prev(j), 0))
    next_spec = pl.BlockSpec(halo, lambda bi, r, j: (bi, r, nxt(j), 0))
    out_sd = jax.ShapeDtypeStruct((b, dil, cls_len, DSA_GROUP_COLS), F32)
    return pl.pallas_call(
        functools.partial(_dsa_kernel, cls_len=cls_len),
        grid=(b, dil, cls_len // tq),
        in_specs=[cur_spec, prev_spec, cur_spec, next_spec, prev_spec, cur_spec, next_spec],
        out_specs=[cur_spec, cur_spec],
        out_shape=(out_sd, out_sd),
        compiler_params=_params(("parallel", "parallel", "parallel")),
        name=f"dsa_g{g}",
    )(q, k, k, k, v, v, v)


def _merge_kernel(o0_ref, l0_ref, o1_ref, l1_ref, o2_ref, l2_ref, yb_ref, gate_ref, x_ref,
                  wpa_ref, wpb_ref, wout_ref, out_ref, join_ref):
    def position_order(ref, slot):
        dil, per_class, _ = ref.shape
        if dil == 1:
            return ref[0]
        halves = []
        for c in range(DSA_GROUP_COLS // LANES):
            for r in range(dil):
                join_ref[slot, c, pl.ds(r, per_class, stride=dil), :] = ref[r, :, c * LANES:(c + 1) * LANES]
            halves.append(join_ref[slot, c])
        return jnp.concatenate(halves, axis=1)

    o0, l0 = position_order(o0_ref, 0), position_order(l0_ref, 0)
    o1, l1 = position_order(o1_ref, 0), position_order(l1_ref, 1)
    o2, l2 = position_order(o2_ref, 2), position_order(l2_ref, 3)
    m = jnp.maximum(jnp.maximum(l0, l1), l2)
    e0, e1, e2 = jnp.exp(l0 - m), jnp.exp(l1 - m), jnp.exp(l2 - m)
    den = e0 + e1 + e2
    ya = (e0 / den) * o0 + (e1 / den) * o1 + (e2 / den) * o2
    gate = gate_ref[...]
    pa = _dot(ya.astype(BF16), wpa_ref[...])
    pb = _dot(yb_ref[...], wpb_ref[...])
    merged = gate[:, :D_MODEL] * pa + gate[:, D_MODEL:] * pb
    out_ref[...] = x_ref[...] + _dot(merged.astype(BF16), wout_ref[...])


def _merge(dsa_out, yb, gates, x2d, seq, w, tm):
    t = x2d.shape[0]
    nseq = seq // tm
    row = lambda i: (i, 0)
    flat, specs = [], []
    for pair in dsa_out:
        for a in pair:
            dil = a.shape[1]
            flat.append(a)
            specs.append(pl.BlockSpec((None, dil, tm // dil, DSA_GROUP_COLS),
                                      lambda i: (i // nseq, 0, i % nseq, 0)))
    return pl.pallas_call(
        _merge_kernel,
        grid=(t // tm,),
        in_specs=specs + [
            pl.BlockSpec((tm, MLA_HEADS * MLA_SLAB), row),
            pl.BlockSpec((tm, 2 * D_MODEL), row),
            pl.BlockSpec((tm, D_MODEL), row),
            _const_spec(w['wpa'].shape), _const_spec(w['wpb'].shape), _const_spec(w['wout'].shape),
        ],
        out_specs=pl.BlockSpec((tm, D_MODEL), row),
        out_shape=jax.ShapeDtypeStruct((t, D_MODEL), F32),
        scratch_shapes=[pltpu.VMEM((4, DSA_GROUP_COLS // LANES, tm, LANES), F32)],
        compiler_params=_params(("parallel",)),
        name="merge",
    )(*flat, yb, gates, x2d, w['wpa'], w['wpb'], w['wout'])


def _norm_mm_kernel(x_ref, g_ref, w_ref, o_ref):
    o_ref[...] = _dot(_rms(x_ref[...], g_ref[...]).astype(BF16), w_ref[...]).astype(o_ref.dtype)


def _norm_mm(x2d, g, wmat, tm):
    t, d = x2d.shape
    n = wmat.shape[1]
    return pl.pallas_call(
        _norm_mm_kernel,
        grid=(t // tm,),
        in_specs=[pl.BlockSpec((tm, d), lambda i: (i, 0)), _const_spec((1, d)), _const_spec(wmat.shape)],
        out_specs=pl.BlockSpec((tm, n), lambda i: (i, 0)),
        out_shape=jax.ShapeDtypeStruct((t, n), BF16),
        compiler_params=_params(("parallel",)),
        name="mem_kv",
    )(x2d, g, wmat)


def _xattn_kernel(x_ref, g_ref, kv_ref, wq_ref, wo_ref, out_ref):
    x = x_ref[...]
    hq = _rms(x, g_ref[...]).astype(BF16)
    q = (_dot(hq, wq_ref[...]) * (MEM_HEAD_DIM ** -0.5)).astype(BF16)
    kv = kv_ref[...]
    outs = []
    for h in range(MEM_HEADS):
        sl = slice(h * MEM_HEAD_DIM, (h + 1) * MEM_HEAD_DIM)
        vsl = slice(D_MODEL + h * MEM_HEAD_DIM, D_MODEL + (h + 1) * MEM_HEAD_DIM)
        s = _dot_nt(q[:, sl], kv[:, sl])
        m = jnp.max(s, axis=1, keepdims=True)
        p = jnp.exp(s - m)
        den = jnp.sum(p, axis=1, keepdims=True)
        outs.append(_dot((p / den).astype(BF16), kv[:, vsl]).astype(BF16))
    o = jnp.concatenate(outs, axis=1)
    out_ref[...] = x + _dot(o, wo_ref[...])


def _xattn(x3d, kv3d, w, tm):
    b, s, d = x3d.shape
    n_mem = kv3d.shape[1]
    return pl.pallas_call(
        _xattn_kernel,
        grid=(b, s // tm),
        in_specs=[
            pl.BlockSpec((None, tm, d), lambda bi, i: (bi, i, 0)),
            _const_spec((1, d)),
            pl.BlockSpec((None, n_mem, 2 * d), lambda bi, i: (bi, 0, 0)),
            _const_spec(w['wmq'].shape), _const_spec(w['wmo'].shape),
        ],
        out_specs=pl.BlockSpec((None, tm, d), lambda bi, i: (bi, i, 0)),
        out_shape=jax.ShapeDtypeStruct((b, s, d), F32),
        compiler_params=_params(("parallel", "parallel")),
        name="xattn",
    )(x3d, w['g_xq'], kv3d, w['wmq'], w['wmo'])


def _top_values(sc, count, with_rank, break_ties):
    n = sc.shape[0]
    row = lax.broadcasted_iota(jnp.int32, sc.shape, 0)
    rank = jnp.full(sc.shape, float(n - 1), F32)
    vals = []
    for k in range(count):
        m = jnp.max(sc, axis=0, keepdims=True)
        hit = sc == m
        if break_ties:
            hit = row == jnp.min(jnp.where(hit, row, n), axis=0, keepdims=True)
        vals.append(m)
        if with_rank:
            rank = jnp.where(hit, float(k), rank)
        sc = jnp.where(hit, -jnp.inf, sc)
    dropped = jnp.sum(jnp.where(sc == -jnp.inf, 1.0, 0.0), axis=0, keepdims=True)
    return jnp.concatenate(vals, axis=0), rank, dropped


def _selected_per_first_key(s1, v1, v2, tau):
    need = []
    for j in range(PEER_TOPK):
        reach = (v1 + v2[j:j + 1]) >= tau
        need.append(jnp.min(jnp.where(reach, v1, jnp.inf), axis=0, keepdims=True))
    c8 = s1 >= need[7]
    c4 = s1 >= jnp.where(c8, need[11], need[3])
    c2 = s1 >= jnp.where(c8, jnp.where(c4, need[13], need[9]), jnp.where(c4, need[5], need[1]))
    low = jnp.where(c4, jnp.where(c2, need[6], need[4]), jnp.where(c2, need[2], need[0]))
    high = jnp.where(c4, jnp.where(c2, need[14], need[12]), jnp.where(c2, need[10], need[8]))
    c1 = s1 >= jnp.where(c8, high, low)
    c16 = s1 >= need[15]
    return (jnp.where(c8, 8.0, 0.0) + jnp.where(c4, 4.0, 0.0) + jnp.where(c2, 2.0, 0.0)
            + jnp.where(c1, 1.0, 0.0) + jnp.where(c16, 1.0, 0.0))


def _pack_bf16_rows(x):
    return pltpu.bitcast(x.astype(BF16), jnp.int32)


def _unpack_bf16_rows(words):
    return pltpu.bitcast(words, BF16)


def _bf16_pair_word(x):
    bits = lax.bitcast_convert_type(x.astype(BF16).astype(F32), jnp.uint32)
    return lax.bitcast_convert_type(bits | (bits >> 16), jnp.int32)


def _peer_score_kernel(x_ref, g_ref, wpq_ref, keys_ref, hqt_ref, r2s_ref, cnts_ref, e1s_ref, e2s_ref):
    hq32 = _rms(x_ref[...], g_ref[...])

    @pl.when(pl.program_id(1) == 0)
    def _():
        hqt_ref[...] = _pack_bf16_rows(hq32.T)

    q = _dot(hq32.astype(BF16), wpq_ref[...]).astype(BF16)
    heads = keys_ref.shape[0]
    halves = [[_dot_nt(keys_ref[h, p], q[:, (2 * h + p) * PEER_HALF:(2 * h + p + 1) * PEER_HALF])
               for p in range(2)] for h in range(heads)]

    def statistics(break_ties):
        worst = [head_statistics(h, break_ties) for h in range(heads)]
        return functools.reduce(jnp.maximum, worst)

    def head_statistics(h, break_ties):
        s1, s2 = halves[h]
        r2_ref, cnt_ref, e1_ref, e2_ref = (ref.at[h] for ref in (r2s_ref, cnts_ref, e1s_ref, e2s_ref))
        v1, _, d1 = _top_values(s1, PEER_TOPK, False, break_ties)
        v2, r2, d2 = _top_values(s2, PEER_TOPK, True, break_ties)
        cand = [v1[0:1] + v2]
        cand += [v1[i:i + 1] + v2[0:8] for i in range(1, 8)]
        cand += [v1[8:16] + v2[0:1]]
        top, _, d3 = _top_values(jnp.concatenate(cand, axis=0), PEER_TOPK, False, break_ties)
        z = jnp.sum(jnp.exp(top - top[0:1]), axis=0, keepdims=True)
        tau = top[PEER_TOPK - 1:PEER_TOPK]
        cnt = _selected_per_first_key(s1, v1, v2, tau)
        r2_ref[...] = _pack_bf16_rows(r2)
        cnt_ref[...] = _bf16_pair_word(cnt)
        e1_ref[...] = _bf16_pair_word(jnp.exp(s1 - v1[0:1]) * (0.5 / z))
        e2_ref[...] = _pack_bf16_rows(jnp.exp(s2 - v2[0:1]))
        return jnp.maximum(jnp.maximum(d1, d2), d3)

    most_dropped = statistics(break_ties=False)

    @pl.when(jnp.max(most_dropped) > PEER_TOPK)
    def _():
        statistics(break_ties=True)


def _peer_scores(x2d, w):
    t = x2d.shape[0]
    tt = PEER_TOKENS
    groups = t // tt
    words = lambda rows: jax.ShapeDtypeStruct((groups, PEER_HEADS, rows, tt), jnp.int32)
    hs = PEER_SCORE_HEADS
    spec = lambda rows: pl.BlockSpec((None, hs, rows, tt), lambda i, h: (i, h, 0, 0))
    half = PEER_NKEYS // 2
    return pl.pallas_call(
        _peer_score_kernel,
        grid=(groups, PEER_HEADS // hs),
        in_specs=[
            pl.BlockSpec((tt, D_MODEL), lambda i, h: (i, 0)),
            _const_spec((1, D_MODEL)),
            pl.BlockSpec((D_MODEL, hs * 2 * PEER_HALF), lambda i, h: (0, h)),
            pl.BlockSpec((hs, 2, PEER_NKEYS, PEER_HALF), lambda i, h: (h, 0, 0, 0)),
        ],
        out_specs=[pl.BlockSpec((None, D_MODEL // 2, tt), lambda i, h: (i, 0, 0)),
                   spec(half), spec(PEER_NKEYS), spec(PEER_NKEYS), spec(half)],
        out_shape=(jax.ShapeDtypeStruct((groups, D_MODEL // 2, tt), jnp.int32),
                   words(half), words(PEER_NKEYS), words(PEER_NKEYS), words(half)),
        compiler_params=_params(("parallel", "arbitrary")),
        name="peer_scores",
    )(x2d, w['g_ffn'], w['wpq'], w['keys'])


def _peer_mix_kernel(hqt_ref, u_ref, vt_ref, r2_ref, cnt_ref, e1_ref, e2_ref, x_ref, gf_ref,
                     out_ref, acc_ref, wg_ref, *, blocks):
    j = pl.program_id(1)
    last = pl.num_programs(1) - 1
    groups, _, tt = hqt_ref.shape
    half_rows = PEER_NKEYS // 2

    @pl.when(j == 0)
    def _():
        acc_ref[...] = jnp.zeros(acc_ref.shape, F32)
        wg_ref[...] = jnp.zeros(wg_ref.shape, jnp.int32)

    def mix_previous_block(g):
        acc_ref[g] += _dot(_unpack_bf16_rows(vt_ref[...]), _unpack_bf16_rows(wg_ref[g]))

    def row_bcast(words):
        return _unpack_bf16_rows(jnp.broadcast_to(words, (half_rows, LANES)))

    def activate_and_mix(g, carry):
        mix_previous_block(g)
        hqt = _unpack_bf16_rows(hqt_ref[g])
        for c in range(blocks // 2):
            u_pair = _unpack_bf16_rows(u_ref[c * PEER_NKEYS:(c + 1) * PEER_NKEYS, :])
            pre2 = _dot(u_pair, hqt)
            for half in range(2):
                i = 2 * c + half
                a = j * blocks + i
                cnt_rows = [cnt_ref[g, h, pl.ds(a, 1), :] for h in range(PEER_HEADS)]
                e1_rows = [e1_ref[g, h, pl.ds(a, 1), :] for h in range(PEER_HEADS)]
                for col in range(tt // LANES):
                    cs = slice(col * LANES, (col + 1) * LANES)
                    wsum = jnp.zeros((PEER_NKEYS, LANES), BF16)
                    for h in range(PEER_HEADS):
                        chosen = _unpack_bf16_rows(r2_ref[g, h, :, cs]) < row_bcast(cnt_rows[h][:, cs])
                        picked = jnp.where(chosen, _unpack_bf16_rows(e2_ref[g, h, :, cs]), jnp.zeros((), BF16))
                        wsum = wsum + picked * row_bcast(e1_rows[h][:, cs])
                    pre = pre2[half * PEER_NKEYS:(half + 1) * PEER_NKEYS, cs]
                    act = pre * (1.0 + lax.erf(pre * (0.5 ** 0.5)))
                    wg_ref[g, i * half_rows:(i + 1) * half_rows, cs] = pltpu.bitcast(
                        wsum * act.astype(BF16), jnp.int32)
        return carry

    @pl.when(j < last)
    def _():
        for g in range(groups):
            activate_and_mix(g, 0)

    @pl.when(j == last)
    def _():
        for g in range(groups):
            mix_previous_block(g)
            rows = slice(g * tt, (g + 1) * tt)
            out_ref[rows, :] = _rms(x_ref[rows, :] + acc_ref[g].T, gf_ref[...])


def _peer_mix(hqt, stats, x2d, w, groups, blocks):
    t = x2d.shape[0]
    tt = PEER_TOKENS
    rows = groups * tt
    ne = blocks * PEER_NKEYS
    nblk = PEER_NKEYS // blocks
    spec = lambda r: pl.BlockSpec((groups, PEER_HEADS, r, tt), lambda i, j: (i, 0, 0, 0))
    half = PEER_NKEYS // 2
    return pl.pallas_call(
        functools.partial(_peer_mix_kernel, blocks=blocks),
        grid=(t // rows, nblk + 1),
        in_specs=[
            pl.BlockSpec((groups, D_MODEL // 2, tt), lambda i, j: (i, 0, 0)),
            pl.BlockSpec((ne // 2, D_MODEL), lambda i, j: (jnp.minimum(j, nblk - 1), 0)),
            pl.BlockSpec((D_MODEL // 2, ne), lambda i, j: (0, jnp.maximum(j - 1, 0))),
            spec(half), spec(PEER_NKEYS), spec(PEER_NKEYS), spec(half),
            pl.BlockSpec((rows, D_MODEL), lambda i, j: (i, 0)),
            _const_spec((1, D_MODEL)),
        ],
        out_specs=pl.BlockSpec((rows, D_MODEL), lambda i, j: (i, 0)),
        out_shape=jax.ShapeDtypeStruct((t, D_MODEL), F32),
        scratch_shapes=[pltpu.VMEM((groups, D_MODEL, tt), F32), pltpu.VMEM((groups, ne // 2, tt), jnp.int32)],
        compiler_params=_params(("parallel", "arbitrary")),
        name="peer_mix",
    )(hqt, w['peer_u'], w['peer_vt'], *stats, x2d, w['g_final'])


def _pack_rows_host(a):
    m2, n = a.shape
    return lax.bitcast_convert_type(jnp.swapaxes(a.reshape(m2 // 2, 2, n), 1, 2), jnp.int32)


def _prep_weights(g_mix, w_in, g_qa, g_kva, w_uq, w_ukv, w_gate, b_gate, w_pa, w_pb, w_out,
                  g_xq, g_mkv, w_mq, w_mkv, w_mo, g_ffn, w_pq, sub_keys, peer_u, peer_v, g_final):
    w_in = w_in[0]
    q_cols = DSA_QK_COLS // 2
    wqk = jnp.concatenate([w_in[:, :q_cols] * (DSA_HEAD_DIM ** -0.5), w_in[:, q_cols:DSA_QK_COLS]], axis=1)
    c0 = DSA_QK_COLS + DSA_V_COLS
    c1 = c0 + MLA_Q_LORA + MLA_KV_LORA
    wkpe = jnp.zeros((D_MODEL, MLA_SLAB), F32).at[:, MLA_NOPE:MLA_QK].set(w_in[:, c1:])
    pad_head = lambda a: jnp.pad(a, ((0, 0), (0, 0), (0, MLA_SLAB - a.shape[2])))
    wq = pad_head(w_uq[0].reshape(MLA_Q_LORA, MLA_HEADS, MLA_QK)).reshape(MLA_Q_LORA, -1)
    ukv = w_ukv[0].reshape(MLA_KV_LORA, MLA_HEADS, MLA_NOPE + MLA_V)
    wk = pad_head(ukv[:, :, :MLA_NOPE]).reshape(MLA_KV_LORA, -1)
    wv = pad_head(ukv[:, :, MLA_NOPE:]).reshape(MLA_KV_LORA, -1)
    wpb = jnp.pad(w_pb[0].reshape(MLA_HEADS, MLA_V, D_MODEL), ((0, 0), (0, MLA_SLAB - MLA_V), (0, 0)))
    bf = lambda a: a.astype(BF16)
    return {
        'g_mix': g_mix, 'wqk': bf(wqk), 'wv': bf(w_in[:, DSA_QK_COLS:c0]), 'wc': bf(w_in[:, c0:c1]),
        'wkpe': bf(wkpe), 'wgate': bf(w_gate[0]), 'b_gate': b_gate,
        'g_qa': g_qa, 'g_kva': g_kva, 'wq_mla': bf(wq), 'wk_mla': bf(wk), 'wv_mla': bf(wv),
        'wpa': bf(w_pa[0]), 'wpb': bf(wpb.reshape(-1, D_MODEL)), 'wout': bf(w_out[0]),
        'g_xq': g_xq, 'g_mkv': g_mkv, 'wmq': bf(w_mq[0]), 'wmkv': bf(w_mkv[0]), 'wmo': bf(w_mo[0]),
        'g_ffn': g_ffn, 'wpq': bf(w_pq[0]),
        'keys': bf(sub_keys[0]),
        'peer_u': _pack_rows_host(bf(peer_u[0])), 'peer_vt': _pack_rows_host(bf(peer_v[0]).T),
        'g_final': g_final.reshape(1, D_MODEL),
    }


def _rope_tables(seq):
    def table(rot, lead, period):
        inv = ROPE_THETA ** (-jnp.arange(0, rot, 2, dtype=F32) / rot)
        ang = jnp.arange(seq, dtype=F32)[:, None] * inv[None, :]
        cos, sin = jnp.cos(ang), jnp.sin(ang)
        ones = lambda n: jnp.ones((seq, n), F32)
        zeros = lambda n: jnp.zeros((seq, n), F32)
        tail = period - lead - rot
        c = jnp.concatenate([ones(lead), cos, cos, ones(tail)], axis=1)
        s = jnp.concatenate([zeros(lead), -sin, sin, zeros(tail)], axis=1)
        reps = LANES // period
        return jnp.tile(c, (1, reps)), jnp.tile(s, (1, reps))
    ca, sa = table(DSA_ROT, 0, DSA_HEAD_DIM)
    cb, sb = table(MLA_ROPE, MLA_NOPE, MLA_SLAB)
    return {'ca': ca, 'sa': sa, 'cb': cb, 'sb': sb}


def _trunk(x, mem, w):
    b, s, d = x.shape
    t = b * s
    tabs = _rope_tables(s)
    x2d = x.reshape(t, d)
    tm = 512
    *dsa_in, cq, ckv, kpe, gates = _inproj(x2d, b, s, w, tabs, tm)
    dsa_out = [_dsa_group(*dsa_in[3 * g:3 * g + 3], g, 256) for g in range(len(DSA_DILATIONS))]
    q_b, k_b, v_b = _mla_proj(cq, ckv, kpe, s, w, tabs, tm)
    wide = MLA_HEADS * MLA_SLAB
    yb = _mla_attn(q_b.reshape(b, s, wide), k_b.reshape(b, s, wide), v_b.reshape(b, s, wide),
                   min(2048, s), min(2048, s))
    x1 = _merge(dsa_out, yb.reshape(t, wide), gates, x2d, s, w, tm)
    n_mem = mem.shape[1]
    kv = _norm_mm(mem.reshape(b * n_mem, d), w['g_mkv'], w['wmkv'], n_mem)
    x2 = _xattn(x1.reshape(b, s, d), kv.reshape(b, n_mem, 2 * d), w, tm)
    x2 = x2.reshape(t, d)
    hqt, *stats = _peer_scores(x2, w)
    y = _peer_mix(hqt, stats, x2, w, 2, 16)
    return y.reshape(b, s, d)


def kernel(x_prompt, x_sample, mem_prompt, mem_sample, g_mix, w_in, g_qa, g_kva, w_uq, w_ukv, w_gate,
           b_gate, w_pa, w_pb, w_out, g_xq, g_mkv, w_mq, w_mkv, w_mo, g_ffn, w_pq, sub_keys, peer_u,
           peer_v, g_final):
    w = _prep_weights(g_mix, w_in, g_qa, g_kva, w_uq, w_ukv, w_gate, b_gate, w_pa, w_pb, w_out,
                      g_xq, g_mkv, w_mq, w_mkv, w_mo, g_ffn, w_pq, sub_keys, peer_u, peer_v, g_final)
    return (_trunk(x_prompt, mem_prompt, w), _trunk(x_sample, mem_sample, w))
```

```python
import functools

import jax
import jax.numpy as jnp
from jax import lax
from jax.experimental import pallas as pl
from jax.experimental.pallas import tpu as pltpu

F32 = jnp.float32
BF16 = jnp.bfloat16

D_MODEL = 1024
EPS = 1e-6
NEG_INF = -1e30
ROPE_THETA = 500000.0

DSA_DILATIONS = (1, 4, 16)
DSA_HALF = 64
DSA_HEADS = 4
DSA_HEAD_DIM = 64
DSA_ROT = 16
DSA_GROUP_COLS = DSA_HEADS * DSA_HEAD_DIM
DSA_QK_COLS = 2 * 3 * DSA_GROUP_COLS
DSA_V_COLS = 3 * DSA_GROUP_COLS

MLA_HEADS = 8
MLA_Q_LORA = 768
MLA_KV_LORA = 256
MLA_NOPE = 64
MLA_ROPE = 32
MLA_V = 64
MLA_QK = MLA_NOPE + MLA_ROPE
MLA_SLAB = 128
MLA_ONE_LANE = MLA_V
MLA_HEADS_PER_STEP = 1
MLA_KEY_CHUNK = 512
LOG2_E = 1.4426950408889634

MEM_HEADS = 4
MEM_HEAD_DIM = 256

PEER_HEADS = 8
PEER_NKEYS = 128
PEER_HALF = 128
PEER_TOPK = 16
PEER_TOKENS = 256
PEER_SCORE_HEADS = 8

LANES = 128
VMEM_LIMIT_BYTES = 56 * 1024 * 1024

ROW_TILE = 512
DSA_QUERY_TILE = 256
MLA_QUERY_TILE = 2048
MLA_KEY_TILE = 2048
PEER_MIX_GROUPS = 2
PEER_MIX_BLOCKS = 16


def _params(sem):
    return pltpu.CompilerParams(dimension_semantics=sem, vmem_limit_bytes=VMEM_LIMIT_BYTES)


def _const_spec(shape):
    nd = len(shape)
    return pl.BlockSpec(shape, lambda *_: (0,) * nd, pipeline_mode=pl.Buffered(1))


def _rms(x, g):
    y = x * lax.rsqrt(jnp.mean(x * x, axis=-1, keepdims=True) + EPS)
    return y * g


def _dot(a, b):
    return jnp.dot(a, b, preferred_element_type=F32)


def _dot_nt(a, b):
    return lax.dot_general(a, b, (((1,), (1,)), ((), ())), preferred_element_type=F32)


def _rope_slab(x, cos_t, sin_t, take_up, half):
    up = pltpu.roll(x, LANES - half, 1)
    dn = pltpu.roll(x, half, 1)
    return x * cos_t + jnp.where(take_up, up, dn) * sin_t


def _inproj_kernel(x_ref, g_ref, wqk_ref, wv_ref, wc_ref, wkpe_ref, wg_ref, bg_ref,
                   ca_ref, sa_ref, cb_ref, sb_ref, *rest):
    dsa_refs = rest[:9]
    cq_ref, ckv_ref, kpe_ref, gate_ref, split_ref = rest[9:]
    h = _rms(x_ref[...], g_ref[...]).astype(BF16)
    rows = h.shape[0]
    lane = lax.broadcasted_iota(jnp.int32, (rows, LANES), 1)
    up_a = (lane % DSA_HEAD_DIM) < (DSA_ROT // 2)
    ca = ca_ref[...]
    sa = sa_ref[...]
    zqk = _dot(h, wqk_ref[...])
    zv = _dot(h, wv_ref[...])
    slot = 0
    for g, dil in enumerate(DSA_DILATIONS):
        for part in range(3):
            if part < 2:
                first = (part * DSA_QK_COLS // 2 + g * DSA_GROUP_COLS) // LANES
                val = jnp.concatenate(
                    [_rope_slab(zqk[:, s * LANES:(s + 1) * LANES], ca, sa, up_a, DSA_ROT // 2)
                     for s in range(first, first + DSA_GROUP_COLS // LANES)], axis=1)
            else:
                val = zv[:, g * DSA_GROUP_COLS:(g + 1) * DSA_GROUP_COLS]
            out_ref = dsa_refs[3 * g + part]
            if dil == 1:
                out_ref[0] = val.astype(BF16)
            else:
                for c in range(DSA_GROUP_COLS // LANES):
                    cs = slice(c * LANES, (c + 1) * LANES)
                    split_ref[slot, c] = val[:, cs]
                    for r in range(dil):
                        out_ref[r, :, cs] = split_ref[slot, c, pl.ds(r, rows // dil, stride=dil), :].astype(BF16)
                slot += 1
    c = _dot(h, wc_ref[...])
    cq_ref[...] = c[:, :MLA_Q_LORA]
    ckv_ref[...] = c[:, MLA_Q_LORA:]
    up_b = (lane >= MLA_NOPE) & (lane < MLA_NOPE + MLA_ROPE // 2)
    kpe_ref[...] = _rope_slab(_dot(h, wkpe_ref[...]), cb_ref[...], sb_ref[...], up_b, MLA_ROPE // 2)
    gate_ref[...] = jax.nn.sigmoid(_dot(h, wg_ref[...]) + bg_ref[...])


def _inproj(x2d, batch, seq, w, tabs, tm):
    t = x2d.shape[0]
    nseq = seq // tm
    row = lambda i: (i, 0)
    tab = lambda i: (i % nseq, 0)
    dsa_shapes, dsa_specs = [], []
    for dil in DSA_DILATIONS:
        for _ in range(3):
            dsa_shapes.append(jax.ShapeDtypeStruct((batch, dil, seq // dil, DSA_GROUP_COLS), BF16))
            dsa_specs.append(pl.BlockSpec((None, dil, tm // dil, DSA_GROUP_COLS),
                                          lambda i: (i // nseq, 0, i % nseq, 0)))
    flat_shapes = (
        jax.ShapeDtypeStruct((t, MLA_Q_LORA), F32),
        jax.ShapeDtypeStruct((t, MLA_KV_LORA), F32),
        jax.ShapeDtypeStruct((t, MLA_SLAB), F32),
        jax.ShapeDtypeStruct((t, 2 * D_MODEL), F32),
    )
    split_slots = 3 * sum(1 for dil in DSA_DILATIONS if dil > 1)
    return pl.pallas_call(
        _inproj_kernel,
        grid=(t // tm,),
        in_specs=[
            pl.BlockSpec((tm, D_MODEL), row),
            _const_spec((1, D_MODEL)),
            _const_spec(w['wqk'].shape), _const_spec(w['wv'].shape), _const_spec(w['wc'].shape),
            _const_spec(w['wkpe'].shape), _const_spec(w['wgate'].shape), _const_spec((1, 2 * D_MODEL)),
            pl.BlockSpec((tm, LANES), tab), pl.BlockSpec((tm, LANES), tab),
            pl.BlockSpec((tm, LANES), tab), pl.BlockSpec((tm, LANES), tab),
        ],
        out_specs=dsa_specs + [pl.BlockSpec((tm, s.shape[1]), row) for s in flat_shapes],
        out_shape=tuple(dsa_shapes) + flat_shapes,
        scratch_shapes=[pltpu.VMEM((split_slots, DSA_GROUP_COLS // LANES, tm, LANES), F32)],
        compiler_params=_params(("parallel",)),
        name="inproj",
    )(x2d, w['g_mix'], w['wqk'], w['wv'], w['wc'], w['wkpe'], w['wgate'], w['b_gate'],
      tabs['ca'], tabs['sa'], tabs['cb'], tabs['sb'])


def _mla_proj_kernel(cq_ref, ckv_ref, kpe_ref, gqa_ref, gkva_ref, wq_ref, wk_ref, wv_ref,
                     cb_ref, sb_ref, q_ref, k_ref, v_ref):
    qn = _rms(cq_ref[...], gqa_ref[...]).astype(BF16)
    rows = qn.shape[0]
    lane = lax.broadcasted_iota(jnp.int32, (rows, LANES), 1)
    up_b = (lane >= MLA_NOPE) & (lane < MLA_NOPE + MLA_ROPE // 2)
    cb = cb_ref[...]
    sb = sb_ref[...]
    q = _dot(qn, wq_ref[...])
    kvn = _rms(ckv_ref[...], gkva_ref[...]).astype(BF16)
    k = _dot(kvn, wk_ref[...])
    kpe = kpe_ref[...]
    scale = MLA_QK ** -0.5 * LOG2_E
    for s in range(MLA_HEADS):
        sl = slice(s * LANES, (s + 1) * LANES)
        q_ref[:, sl] = (_rope_slab(q[:, sl], cb, sb, up_b, MLA_ROPE // 2) * scale).astype(BF16)
        k_ref[:, sl] = (k[:, sl] + kpe).astype(BF16)
    wide_lane = lax.broadcasted_iota(jnp.int32, (1, MLA_HEADS * MLA_SLAB), 1)
    one_lane = jnp.where(wide_lane % MLA_SLAB == MLA_ONE_LANE, 1.0, 0.0)
    v_ref[...] = (_dot(kvn, wv_ref[...]) + one_lane).astype(BF16)


def _mla_proj(cq, ckv, kpe, seq, w, tabs, tm):
    t = cq.shape[0]
    nseq = seq // tm
    row = lambda i: (i, 0)
    tab = lambda i: (i % nseq, 0)
    wide = MLA_HEADS * MLA_SLAB
    out_shape = tuple(jax.ShapeDtypeStruct((t, wide), BF16) for _ in range(3))
    return pl.pallas_call(
        _mla_proj_kernel,
        grid=(t // tm,),
        in_specs=[
            pl.BlockSpec((tm, MLA_Q_LORA), row), pl.BlockSpec((tm, MLA_KV_LORA), row),
            pl.BlockSpec((tm, MLA_SLAB), row),
            _const_spec((1, MLA_Q_LORA)), _const_spec((1, MLA_KV_LORA)),
            _const_spec(w['wq_mla'].shape), _const_spec(w['wk_mla'].shape), _const_spec(w['wv_mla'].shape),
            pl.BlockSpec((tm, LANES), tab), pl.BlockSpec((tm, LANES), tab),
        ],
        out_specs=[pl.BlockSpec((tm, wide), row) for _ in range(3)],
        out_shape=out_shape,
        compiler_params=_params(("parallel",)),
        name="mla_proj",
    )(cq, ckv, kpe, w['g_qa'], w['g_kva'], w['wq_mla'], w['wk_mla'], w['wv_mla'], tabs['cb'], tabs['sb'])


def _mla_attn_kernel(q_ref, k_ref, v_ref, o_ref, m_ref, acc_ref):
    j = pl.program_id(3)

    @pl.when(j == 0)
    def _():
        m_ref[...] = jnp.full(m_ref.shape, -jnp.inf, F32)
        acc_ref[...] = jnp.zeros(acc_ref.shape, F32)

    tk = k_ref.shape[0]
    for h in range(MLA_HEADS_PER_STEP):
        sl = slice(h * MLA_SLAB, (h + 1) * MLA_SLAB)
        q = q_ref[:, sl]
        m_run = m_ref[:, sl]
        acc = acc_ref[:, sl]
        for c in range(tk // MLA_KEY_CHUNK):
            rows = slice(c * MLA_KEY_CHUNK, (c + 1) * MLA_KEY_CHUNK)
            s = _dot_nt(q, k_ref[rows, sl])
            cols = [s[:, i * LANES:(i + 1) * LANES] for i in range(MLA_KEY_CHUNK // LANES)]
            m_loc = functools.reduce(jnp.maximum, cols)
            m_new = jnp.maximum(m_run, jnp.max(m_loc, axis=1, keepdims=True))
            p = jnp.concatenate([jnp.exp2(col - m_new) for col in cols], axis=1).astype(BF16)
            acc = jnp.exp2(m_run - m_new) * acc + _dot(p, v_ref[rows, sl])
            m_run = m_new
        acc_ref[:, sl] = acc
        m_ref[:, sl] = m_run

    @pl.when(j == pl.num_programs(3) - 1)
    def _():
        for h in range(MLA_HEADS_PER_STEP):
            sl = slice(h * MLA_SLAB, (h + 1) * MLA_SLAB)
            acc = acc_ref[:, sl]
            den = acc[:, MLA_ONE_LANE:MLA_ONE_LANE + 1]
            o_ref[:, sl] = (acc / den).astype(o_ref.dtype)


def _mla_attn(q, k, v, tq, tk):
    b, s, wide = q.shape
    cols = MLA_HEADS_PER_STEP * MLA_SLAB
    return pl.pallas_call(
        _mla_attn_kernel,
        grid=(b, MLA_HEADS // MLA_HEADS_PER_STEP, s // tq, s // tk),
        in_specs=[
            pl.BlockSpec((None, tq, cols), lambda bi, h, i, j: (bi, i, h)),
            pl.BlockSpec((None, tk, cols), lambda bi, h, i, j: (bi, j, h)),
            pl.BlockSpec((None, tk, cols), lambda bi, h, i, j: (bi, j, h)),
        ],
        out_specs=pl.BlockSpec((None, tq, cols), lambda bi, h, i, j: (bi, i, h)),
        out_shape=jax.ShapeDtypeStruct((b, s, wide), BF16),
        scratch_shapes=[pltpu.VMEM((tq, cols), F32), pltpu.VMEM((tq, cols), F32)],
        compiler_params=_params(("parallel", "parallel", "parallel", "arbitrary")),
        name="mla_attn",
    )(q, k, v)


def _dsa_kernel(q_ref, kp_ref, kc_ref, kn_ref, vp_ref, vc_ref, vn_ref, o_ref, lse_ref, *, cls_len):
    j = pl.program_id(2)
    tq = q_ref.shape[0]
    nk = tq + 2 * DSA_HALF
    q = q_ref[...]
    kk = jnp.concatenate([kp_ref[...], kc_ref[...], kn_ref[...]], axis=0)
    vv = jnp.concatenate([vp_ref[...], vc_ref[...], vn_ref[...]], axis=0)
    qpos = j * tq + lax.broadcasted_iota(jnp.int32, (tq, nk), 0)
    kpos = j * tq - DSA_HALF + lax.broadcasted_iota(jnp.int32, (tq, nk), 1)
    ok = (jnp.abs(kpos - qpos) <= DSA_HALF) & (kpos >= 0) & (kpos < cls_len)
    lane = lax.broadcasted_iota(jnp.int32, (tq, DSA_GROUP_COLS), 1)
    heads = range(DSA_HEADS)
    in_head = [(lane // DSA_HEAD_DIM) == h for h in heads]
    s = [jnp.where(ok, _dot_nt(jnp.where(in_head[h], q, jnp.zeros_like(q)), kk), NEG_INF) for h in heads]
    m = [jnp.max(s[h], axis=1, keepdims=True) for h in heads]
    p = [jnp.exp(s[h] - m[h]) for h in heads]
    den = [jnp.sum(p[h], axis=1, keepdims=True) for h in heads]
    oh = [_dot((p[h] / den[h]).astype(BF16), vv) for h in heads]
    o = jnp.zeros((tq, DSA_GROUP_COLS), F32)
    lse = jnp.zeros((tq, DSA_GROUP_COLS), F32)
    for h in heads:
        o = jnp.where(in_head[h], oh[h], o)
        lse = jnp.where(in_head[h], m[h] + jnp.log(den[h]), lse)
    o_ref[...] = o
    lse_ref[...] = lse


def _dsa_group(q, k, v, g, tq):
    b, dil, cls_len, _ = q.shape
    tq = min(tq, cls_len)
    per = tq // DSA_HALF
    last = cls_len // DSA_HALF - 1
    prev = lambda j: jnp.maximum(j * per - 1, 0)
    nxt = lambda j: jnp.minimum((j + 1) * per, last)
    halo = (None, None, DSA_HALF, DSA_GROUP_COLS)
    cur = (None, None, tq, DSA_GROUP_COLS)
    cur_spec = pl.BlockSpec(cur, lambda bi, r, j: (bi, r, j, 0))
    prev_spec = pl.BlockSpec(halo, lambda bi, r, j: (bi, r, prev(j), 0))
    next_spec = pl.BlockSpec(halo, lambda bi, r, j: (bi, r, nxt(j), 0))
    out_sd = jax.ShapeDtypeStruct((b, dil, cls_len, DSA_GROUP_COLS), F32)
    return pl.pallas_call(
        functools.partial(_dsa_kernel, cls_len=cls_len),
        grid=(b, dil, cls_len // tq),
        in_specs=[cur_spec, prev_spec, cur_spec, next_spec, prev_spec, cur_spec, next_spec],
        out_specs=[cur_spec, cur_spec],
        out_shape=(out_sd, out_sd),
        compiler_params=_params(("parallel", "parallel", "parallel")),
        name=f"dsa_g{g}",
    )(q, k, k, k, v, v, v)


def _merge_kernel(o0_ref, l0_ref, o1_ref, l1_ref, o2_ref, l2_ref, yb_ref, gate_ref, x_ref,
                  wpa_ref, wpb_ref, wout_ref, out_ref, join_ref):
    def position_order(ref, slot):
        dil, per_class, _ = ref.shape
        if dil == 1:
            return ref[0]
        halves = []
        for c in range(DSA_GROUP_COLS // LANES):
            for r in range(dil):
                join_ref[slot, c, pl.ds(r, per_class, stride=dil), :] = ref[r, :, c * LANES:(c + 1) * LANES]
            halves.append(join_ref[slot, c])
        return jnp.concatenate(halves, axis=1)

    o0, l0 = position_order(o0_ref, 0), position_order(l0_ref, 0)
    o1, l1 = position_order(o1_ref, 0), position_order(l1_ref, 1)
    o2, l2 = position_order(o2_ref, 2), position_order(l2_ref, 3)
    m = jnp.maximum(jnp.maximum(l0, l1), l2)
    e0, e1, e2 = jnp.exp(l0 - m), jnp.exp(l1 - m), jnp.exp(l2 - m)
    den = e0 + e1 + e2
    ya = (e0 / den) * o0 + (e1 / den) * o1 + (e2 / den) * o2
    gate = gate_ref[...]
    pa = _dot(ya.astype(BF16), wpa_ref[...])
    pb = _dot(yb_ref[...], wpb_ref[...])
    merged = gate[:, :D_MODEL] * pa + gate[:, D_MODEL:] * pb
    out_ref[...] = x_ref[...] + _dot(merged.astype(BF16), wout_ref[...])


def _merge(dsa_out, yb, gates, x2d, seq, w, tm):
    t = x2d.shape[0]
    nseq = seq // tm
    row = lambda i: (i, 0)
    flat, specs = [], []
    for pair in dsa_out:
        for a in pair:
            dil = a.shape[1]
            flat.append(a)
            specs.append(pl.BlockSpec((None, dil, tm // dil, DSA_GROUP_COLS),
                                      lambda i: (i // nseq, 0, i % nseq, 0)))
    return pl.pallas_call(
        _merge_kernel,
        grid=(t // tm,),
        in_specs=specs + [
            pl.BlockSpec((tm, MLA_HEADS * MLA_SLAB), row),
            pl.BlockSpec((tm, 2 * D_MODEL), row),
            pl.BlockSpec((tm, D_MODEL), row),
            _const_spec(w['wpa'].shape), _const_spec(w['wpb'].shape), _const_spec(w['wout'].shape),
        ],
        out_specs=pl.BlockSpec((tm, D_MODEL), row),
        out_shape=jax.ShapeDtypeStruct((t, D_MODEL), F32),
        scratch_shapes=[pltpu.VMEM((4, DSA_GROUP_COLS // LANES, tm, LANES), F32)],
        compiler_params=_params(("parallel",)),
        name="merge",
    )(*flat, yb, gates, x2d, w['wpa'], w['wpb'], w['wout'])


def _norm_mm_kernel(x_ref, g_ref, w_ref, o_ref):
    o_ref[...] = _dot(_rms(x_ref[...], g_ref[...]).astype(BF16), w_ref[...]).astype(o_ref.dtype)


def _norm_mm(x2d, g, wmat, tm):
    t, d = x2d.shape
    n = wmat.shape[1]
    return pl.pallas_call(
        _norm_mm_kernel,
        grid=(t // tm,),
        in_specs=[pl.BlockSpec((tm, d), lambda i: (i, 0)), _const_spec((1, d)), _const_spec(wmat.shape)],
        out_specs=pl.BlockSpec((tm, n), lambda i: (i, 0)),
        out_shape=jax.ShapeDtypeStruct((t, n), BF16),
        compiler_params=_params(("parallel",)),
        name="mem_kv",
    )(x2d, g, wmat)


def _xattn_kernel(x_ref, g_ref, kv_ref, wq_ref, wo_ref, out_ref):
    x = x_ref[...]
    hq = _rms(x, g_ref[...]).astype(BF16)
    q = (_dot(hq, wq_ref[...]) * (MEM_HEAD_DIM ** -0.5)).astype(BF16)
    kv = kv_ref[...]
    outs = []
    for h in range(MEM_HEADS):
        sl = slice(h * MEM_HEAD_DIM, (h + 1) * MEM_HEAD_DIM)
        vsl = slice(D_MODEL + h * MEM_HEAD_DIM, D_MODEL + (h + 1) * MEM_HEAD_DIM)
        s = _dot_nt(q[:, sl], kv[:, sl])
        m = jnp.max(s, axis=1, keepdims=True)
        p = jnp.exp(s - m)
        den = jnp.sum(p, axis=1, keepdims=True)
        outs.append(_dot((p / den).astype(BF16), kv[:, vsl]).astype(BF16))
    o = jnp.concatenate(outs, axis=1)
    out_ref[...] = x + _dot(o, wo_ref[...])


def _xattn(x3d, kv3d, w, tm):
    b, s, d = x3d.shape
    n_mem = kv3d.shape[1]
    return pl.pallas_call(
        _xattn_kernel,
        grid=(b, s // tm),
        in_specs=[
            pl.BlockSpec((None, tm, d), lambda bi, i: (bi, i, 0)),
            _const_spec((1, d)),
            pl.BlockSpec((None, n_mem, 2 * d), lambda bi, i: (bi, 0, 0)),
            _const_spec(w['wmq'].shape), _const_spec(w['wmo'].shape),
        ],
        out_specs=pl.BlockSpec((None, tm, d), lambda bi, i: (bi, i, 0)),
        out_shape=jax.ShapeDtypeStruct((b, s, d), F32),
        compiler_params=_params(("parallel", "parallel")),
        name="xattn",
    )(x3d, w['g_xq'], kv3d, w['wmq'], w['wmo'])


def _top_values(sc, count, with_rank, break_ties):
    n = sc.shape[0]
    row = lax.broadcasted_iota(jnp.int32, sc.shape, 0)
    rank = jnp.full(sc.shape, float(n - 1), F32)
    vals = []
    for k in range(count):
        m = jnp.max(sc, axis=0, keepdims=True)
        hit = sc == m
        if break_ties:
            hit = row == jnp.min(jnp.where(hit, row, n), axis=0, keepdims=True)
        vals.append(m)
        if with_rank:
            rank = jnp.where(hit, float(k), rank)
        sc = jnp.where(hit, -jnp.inf, sc)
    dropped = jnp.sum(jnp.where(sc == -jnp.inf, 1.0, 0.0), axis=0, keepdims=True)
    return jnp.concatenate(vals, axis=0), rank, dropped


def _selected_per_first_key(s1, v1, v2, tau):
    need = []
    for j in range(PEER_TOPK):
        reach = (v1 + v2[j:j + 1]) >= tau
        need.append(jnp.min(jnp.where(reach, v1, jnp.inf), axis=0, keepdims=True))
    c8 = s1 >= need[7]
    c4 = s1 >= jnp.where(c8, need[11], need[3])
    c2 = s1 >= jnp.where(c8, jnp.where(c4, need[13], need[9]), jnp.where(c4, need[5], need[1]))
    low = jnp.where(c4, jnp.where(c2, need[6], need[4]), jnp.where(c2, need[2], need[0]))
    high = jnp.where(c4, jnp.where(c2, need[14], need[12]), jnp.where(c2, need[10], need[8]))
    c1 = s1 >= jnp.where(c8, high, low)
    c16 = s1 >= need[15]
    return (jnp.where(c8, 8.0, 0.0) + jnp.where(c4, 4.0, 0.0) + jnp.where(c2, 2.0, 0.0)
            + jnp.where(c1, 1.0, 0.0) + jnp.where(c16, 1.0, 0.0))


def _pack_bf16_rows(x):
    return pltpu.bitcast(x.astype(BF16), jnp.int32)


def _unpack_bf16_rows(words):
    return pltpu.bitcast(words, BF16)


def _bf16_pair_word(x):
    bits = lax.bitcast_convert_type(x.astype(BF16).astype(F32), jnp.uint32)
    return lax.bitcast_convert_type(bits | (bits >> 16), jnp.int32)


def _peer_score_kernel(x_ref, g_ref, wpq_ref, keys_ref, hqt_ref, r2s_ref, cnts_ref, e1s_ref, e2s_ref):
    hq32 = _rms(x_ref[...], g_ref[...])

    @pl.when(pl.program_id(1) == 0)
    def _():
        hqt_ref[...] = _pack_bf16_rows(hq32.T)

    q = _dot(hq32.astype(BF16), wpq_ref[...]).astype(BF16)
    heads = keys_ref.shape[0]
    halves = [[_dot_nt(keys_ref[h, p], q[:, (2 * h + p) * PEER_HALF:(2 * h + p + 1) * PEER_HALF])
               for p in range(2)] for h in range(heads)]

    def statistics(break_ties):
        worst = [head_statistics(h, break_ties) for h in range(heads)]
        return functools.reduce(jnp.maximum, worst)

    def head_statistics(h, break_ties):
        s1, s2 = halves[h]
        r2_ref, cnt_ref, e1_ref, e2_ref = (ref.at[h] for ref in (r2s_ref, cnts_ref, e1s_ref, e2s_ref))
        v1, _, d1 = _top_values(s1, PEER_TOPK, False, break_ties)
        v2, r2, d2 = _top_values(s2, PEER_TOPK, True, break_ties)
        cand = [v1[0:1] + v2]
        cand += [v1[i:i + 1] + v2[0:8] for i in range(1, 8)]
        cand += [v1[8:16] + v2[0:1]]
        top, _, d3 = _top_values(jnp.concatenate(cand, axis=0), PEER_TOPK, False, break_ties)
        z = jnp.sum(jnp.exp(top - top[0:1]), axis=0, keepdims=True)
        tau = top[PEER_TOPK - 1:PEER_TOPK]
        cnt = _selected_per_first_key(s1, v1, v2, tau)
        r2_ref[...] = _pack_bf16_rows(r2)
        cnt_ref[...] = _bf16_pair_word(cnt)
        e1_ref[...] = _bf16_pair_word(jnp.exp(s1 - v1[0:1]) * (0.5 / z))
        e2_ref[...] = _pack_bf16_rows(jnp.exp(s2 - v2[0:1]))
        return jnp.maximum(jnp.maximum(d1, d2), d3)

    most_dropped = statistics(break_ties=False)

    @pl.when(jnp.max(most_dropped) > PEER_TOPK)
    def _():
        statistics(break_ties=True)


def _peer_scores(x2d, w):
    t = x2d.shape[0]
    tt = PEER_TOKENS
    groups = t // tt
    words = lambda rows: jax.ShapeDtypeStruct((groups, PEER_HEADS, rows, tt), jnp.int32)
    hs = PEER_SCORE_HEADS
    spec = lambda rows: pl.BlockSpec((None, hs, rows, tt), lambda i, h: (i, h, 0, 0))
    half = PEER_NKEYS // 2
    return pl.pallas_call(
        _peer_score_kernel,
        grid=(groups, PEER_HEADS // hs),
        in_specs=[
            pl.BlockSpec((tt, D_MODEL), lambda i, h: (i, 0)),
            _const_spec((1, D_MODEL)),
            pl.BlockSpec((D_MODEL, hs * 2 * PEER_HALF), lambda i, h: (0, h)),
            pl.BlockSpec((hs, 2, PEER_NKEYS, PEER_HALF), lambda i, h: (h, 0, 0, 0)),
        ],
        out_specs=[pl.BlockSpec((None, D_MODEL // 2, tt), lambda i, h: (i, 0, 0)),
                   spec(half), spec(PEER_NKEYS), spec(PEER_NKEYS), spec(half)],
        out_shape=(jax.ShapeDtypeStruct((groups, D_MODEL // 2, tt), jnp.int32),
                   words(half), words(PEER_NKEYS), words(PEER_NKEYS), words(half)),
        compiler_params=_params(("parallel", "arbitrary")),
        name="peer_scores",
    )(x2d, w['g_ffn'], w['wpq'], w['keys'])


def _peer_mix_kernel(hqt_ref, u_ref, vt_ref, r2_ref, cnt_ref, e1_ref, e2_ref, x_ref, gf_ref,
                     out_ref, acc_ref, wg_ref, *, blocks):
    j = pl.program_id(1)
    last = pl.num_programs(1) - 1
    groups, _, tt = hqt_ref.shape
    half_rows = PEER_NKEYS // 2

    @pl.when(j == 0)
    def _():
        acc_ref[...] = jnp.zeros(acc_ref.shape, F32)
        wg_ref[...] = jnp.zeros(wg_ref.shape, jnp.int32)

    def mix_previous_block(g):
        acc_ref[g] += _dot(_unpack_bf16_rows(vt_ref[...]), _unpack_bf16_rows(wg_ref[g]))

    def row_bcast(words):
        return _unpack_bf16_rows(jnp.broadcast_to(words, (half_rows, LANES)))

    def activate_and_mix(g, carry):
        mix_previous_block(g)
        hqt = _unpack_bf16_rows(hqt_ref[g])
        for c in range(blocks // 2):
            u_pair = _unpack_bf16_rows(u_ref[c * PEER_NKEYS:(c + 1) * PEER_NKEYS, :])
            pre2 = _dot(u_pair, hqt)
            for half in range(2):
                i = 2 * c + half
                a = j * blocks + i
                cnt_rows = [cnt_ref[g, h, pl.ds(a, 1), :] for h in range(PEER_HEADS)]
                e1_rows = [e1_ref[g, h, pl.ds(a, 1), :] for h in range(PEER_HEADS)]
                for col in range(tt // LANES):
                    cs = slice(col * LANES, (col + 1) * LANES)
                    wsum = jnp.zeros((PEER_NKEYS, LANES), BF16)
                    for h in range(PEER_HEADS):
                        chosen = _unpack_bf16_rows(r2_ref[g, h, :, cs]) < row_bcast(cnt_rows[h][:, cs])
                        picked = jnp.where(chosen, _unpack_bf16_rows(e2_ref[g, h, :, cs]), jnp.zeros((), BF16))
                        wsum = wsum + picked * row_bcast(e1_rows[h][:, cs])
                    pre = pre2[half * PEER_NKEYS:(half + 1) * PEER_NKEYS, cs]
                    act = pre * (1.0 + lax.erf(pre * (0.5 ** 0.5)))
                    wg_ref[g, i * half_rows:(i + 1) * half_rows, cs] = pltpu.bitcast(
                        wsum * act.astype(BF16), jnp.int32)
        return carry

    @pl.when(j < last)
    def _():
        for g in range(groups):
            activate_and_mix(g, 0)

    @pl.when(j == last)
    def _():
        for g in range(groups):
            mix_previous_block(g)
            rows = slice(g * tt, (g + 1) * tt)
            out_ref[rows, :] = _rms(x_ref[rows, :] + acc_ref[g].T, gf_ref[...])


def _peer_mix(hqt, stats, x2d, w, groups, blocks):
    t = x2d.shape[0]
    tt = PEER_TOKENS
    rows = groups * tt
    ne = blocks * PEER_NKEYS
    nblk = PEER_NKEYS // blocks
    spec = lambda r: pl.BlockSpec((groups, PEER_HEADS, r, tt), lambda i, j: (i, 0, 0, 0))
    half = PEER_NKEYS // 2
    return pl.pallas_call(
        functools.partial(_peer_mix_kernel, blocks=blocks),
        grid=(t // rows, nblk + 1),
        in_specs=[
            pl.BlockSpec((groups, D_MODEL // 2, tt), lambda i, j: (i, 0, 0)),
            pl.BlockSpec((ne // 2, D_MODEL), lambda i, j: (jnp.minimum(j, nblk - 1), 0)),
            pl.BlockSpec((D_MODEL // 2, ne), lambda i, j: (0, jnp.maximum(j - 1, 0))),
            spec(half), spec(PEER_NKEYS), spec(PEER_NKEYS), spec(half),
            pl.BlockSpec((rows, D_MODEL), lambda i, j: (i, 0)),
            _const_spec((1, D_MODEL)),
        ],
        out_specs=pl.BlockSpec((rows, D_MODEL), lambda i, j: (i, 0)),
        out_shape=jax.ShapeDtypeStruct((t, D_MODEL), F32),
        scratch_shapes=[pltpu.VMEM((groups, D_MODEL, tt), F32), pltpu.VMEM((groups, ne // 2, tt), jnp.int32)],
        compiler_params=_params(("parallel", "arbitrary")),
        name="peer_mix",
    )(hqt, w['peer_u'], w['peer_vt'], *stats, x2d, w['g_final'])


def _pack_rows_host(a):
    m2, n = a.shape
    return lax.bitcast_convert_type(jnp.swapaxes(a.reshape(m2 // 2, 2, n), 1, 2), jnp.int32)


def _prep_weights(g_mix, w_in, g_qa, g_kva, w_uq, w_ukv, w_gate, b_gate, w_pa, w_pb, w_out,
                  g_xq, g_mkv, w_mq, w_mkv, w_mo, g_ffn, w_pq, sub_keys, peer_u, peer_v, g_final):
    w_in = w_in[0]
    q_cols = DSA_QK_COLS // 2
    wqk = jnp.concatenate([w_in[:, :q_cols] * (DSA_HEAD_DIM ** -0.5), w_in[:, q_cols:DSA_QK_COLS]], axis=1)
    c0 = DSA_QK_COLS + DSA_V_COLS
    c1 = c0 + MLA_Q_LORA + MLA_KV_LORA
    wkpe = jnp.zeros((D_MODEL, MLA_SLAB), F32).at[:, MLA_NOPE:MLA_QK].set(w_in[:, c1:])
    pad_head = lambda a: jnp.pad(a, ((0, 0), (0, 0), (0, MLA_SLAB - a.shape[2])))
    wq = pad_head(w_uq[0].reshape(MLA_Q_LORA, MLA_HEADS, MLA_QK)).reshape(MLA_Q_LORA, -1)
    ukv = w_ukv[0].reshape(MLA_KV_LORA, MLA_HEADS, MLA_NOPE + MLA_V)
    wk = pad_head(ukv[:, :, :MLA_NOPE]).reshape(MLA_KV_LORA, -1)
    wv = pad_head(ukv[:, :, MLA_NOPE:]).reshape(MLA_KV_LORA, -1)
    wpb = jnp.pad(w_pb[0].reshape(MLA_HEADS, MLA_V, D_MODEL), ((0, 0), (0, MLA_SLAB - MLA_V), (0, 0)))
    bf = lambda a: a.astype(BF16)
    return {
        'g_mix': g_mix, 'wqk': bf(wqk), 'wv': bf(w_in[:, DSA_QK_COLS:c0]), 'wc': bf(w_in[:, c0:c1]),
        'wkpe': bf(wkpe), 'wgate': bf(w_gate[0]), 'b_gate': b_gate,
        'g_qa': g_qa, 'g_kva': g_kva, 'wq_mla': bf(wq), 'wk_mla': bf(wk), 'wv_mla': bf(wv),
        'wpa': bf(w_pa[0]), 'wpb': bf(wpb.reshape(-1, D_MODEL)), 'wout': bf(w_out[0]),
        'g_xq': g_xq, 'g_mkv': g_mkv, 'wmq': bf(w_mq[0]), 'wmkv': bf(w_mkv[0]), 'wmo': bf(w_mo[0]),
        'g_ffn': g_ffn, 'wpq': bf(w_pq[0]),
        'keys': bf(sub_keys[0]),
        'peer_u': _pack_rows_host(bf(peer_u[0])), 'peer_vt': _pack_rows_host(bf(peer_v[0]).T),
        'g_final': g_final.reshape(1, D_MODEL),
    }


def _rope_tables(seq):
    def table(rot, lead, period):
        inv = ROPE_THETA ** (-jnp.arange(0, rot, 2, dtype=F32) / rot)
        ang = jnp.arange(seq, dtype=F32)[:, None] * inv[None, :]
        cos, sin = jnp.cos(ang), jnp.sin(ang)
        ones = lambda n: jnp.ones((seq, n), F32)
        zeros = lambda n: jnp.zeros((seq, n), F32)
        tail = period - lead - rot
        c = jnp.concatenate([ones(lead), cos, cos, ones(tail)], axis=1)
        s = jnp.concatenate([zeros(lead), -sin, sin, zeros(tail)], axis=1)
        reps = LANES // period
        return jnp.tile(c, (1, reps)), jnp.tile(s, (1, reps))
    ca, sa = table(DSA_ROT, 0, DSA_HEAD_DIM)
    cb, sb = table(MLA_ROPE, MLA_NOPE, MLA_SLAB)
    return {'ca': ca, 'sa': sa, 'cb': cb, 'sb': sb}


def _trunk(x, mem, w):
    b, s, d = x.shape
    t = b * s
    tabs = _rope_tables(s)
    x2d = x.reshape(t, d)
    tm = ROW_TILE
    *dsa_in, cq, ckv, kpe, gates = _inproj(x2d, b, s, w, tabs, tm)
    dsa_out = [_dsa_group(*dsa_in[3 * g:3 * g + 3], g, DSA_QUERY_TILE) for g in range(len(DSA_DILATIONS))]
    q_b, k_b, v_b = _mla_proj(cq, ckv, kpe, s, w, tabs, tm)
    wide = MLA_HEADS * MLA_SLAB
    yb = _mla_attn(q_b.reshape(b, s, wide), k_b.reshape(b, s, wide), v_b.reshape(b, s, wide),
                   min(MLA_QUERY_TILE, s), min(MLA_KEY_TILE, s))
    x1 = _merge(dsa_out, yb.reshape(t, wide), gates, x2d, s, w, tm)
    n_mem = mem.shape[1]
    kv = _norm_mm(mem.reshape(b * n_mem, d), w['g_mkv'], w['wmkv'], n_mem)
    x2 = _xattn(x1.reshape(b, s, d), kv.reshape(b, n_mem, 2 * d), w, tm)
    x2 = x2.reshape(t, d)
    hqt, *stats = _peer_scores(x2, w)
    y = _peer_mix(hqt, stats, x2, w, PEER_MIX_GROUPS, PEER_MIX_BLOCKS)
    return y.reshape(b, s, d)


def kernel(x_prompt, x_sample, mem_prompt, mem_sample, g_mix, w_in, g_qa, g_kva, w_uq, w_ukv, w_gate,
           b_gate, w_pa, w_pb, w_out, g_xq, g_mkv, w_mq, w_mkv, w_mo, g_ffn, w_pq, sub_keys, peer_u,
           peer_v, g_final):
    w = _prep_weights(g_mix, w_in, g_qa, g_kva, w_uq, w_ukv, w_gate, b_gate, w_pa, w_pb, w_out,
                      g_xq, g_mkv, w_mq, w_mkv, w_mo, g_ffn, w_pq, sub_keys, peer_u, peer_v, g_final)
    return (_trunk(x_prompt, mem_prompt, w), _trunk(x_sample, mem_sample, w))
```

```python
import functools

import jax
import jax.numpy as jnp
from jax import lax
from jax.experimental import pallas as pl
from jax.experimental.pallas import tpu as pltpu

F32 = jnp.float32
BF16 = jnp.bfloat16

D_MODEL = 1024
EPS = 1e-6
NEG_INF = -1e30
ROPE_THETA = 500000.0

DSA_DILATIONS = (1, 4, 16)
DSA_HALF = 64
DSA_HEADS = 4
DSA_HEAD_DIM = 64
DSA_ROT = 16
DSA_GROUP_COLS = DSA_HEADS * DSA_HEAD_DIM
DSA_QK_COLS = 2 * 3 * DSA_GROUP_COLS
DSA_V_COLS = 3 * DSA_GROUP_COLS

MLA_HEADS = 8
MLA_Q_LORA = 768
MLA_KV_LORA = 256
MLA_NOPE = 64
MLA_ROPE = 32
MLA_V = 64
MLA_QK = MLA_NOPE + MLA_ROPE
MLA_SLAB = 128
MLA_ONE_LANE = MLA_V
MLA_HEADS_PER_STEP = 1
MLA_KEY_CHUNK = 512
LOG2_E = 1.4426950408889634

MEM_HEADS = 4
MEM_HEAD_DIM = 256

PEER_HEADS = 8
PEER_NKEYS = 128
PEER_HALF = 128
PEER_TOPK = 16
PEER_TOKENS = 256
PEER_SCORE_HEADS = 4

LANES = 128
VMEM_LIMIT_BYTES = 56 * 1024 * 1024

ROW_TILE = 512
DSA_QUERY_TILE = 256
MLA_QUERY_TILE = 2048
MLA_KEY_TILE = 2048
PEER_MIX_GROUPS = 2
PEER_MIX_BLOCKS = 16


def _params(sem):
    return pltpu.CompilerParams(dimension_semantics=sem, vmem_limit_bytes=VMEM_LIMIT_BYTES)


def _const_spec(shape):
    nd = len(shape)
    return pl.BlockSpec(shape, lambda *_: (0,) * nd, pipeline_mode=pl.Buffered(1))


def _rms(x, g):
    y = x * lax.rsqrt(jnp.mean(x * x, axis=-1, keepdims=True) + EPS)
    return y * g


def _dot(a, b):
    return jnp.dot(a, b, preferred_element_type=F32)


def _dot_nt(a, b):
    return lax.dot_general(a, b, (((1,), (1,)), ((), ())), preferred_element_type=F32)


def _rope_slab(x, cos_t, sin_t, take_up, half):
    up = pltpu.roll(x, LANES - half, 1)
    dn = pltpu.roll(x, half, 1)
    return x * cos_t + jnp.where(take_up, up, dn) * sin_t


def _inproj_kernel(x_ref, g_ref, wqk_ref, wv_ref, wc_ref, wkpe_ref, wg_ref, bg_ref,
                   ca_ref, sa_ref, cb_ref, sb_ref, *rest):
    dsa_refs = rest[:9]
    cq_ref, ckv_ref, kpe_ref, gate_ref, split_ref = rest[9:]
    h = _rms(x_ref[...], g_ref[...]).astype(BF16)
    rows = h.shape[0]
    lane = lax.broadcasted_iota(jnp.int32, (rows, LANES), 1)
    up_a = (lane % DSA_HEAD_DIM) < (DSA_ROT // 2)
    ca = ca_ref[...]
    sa = sa_ref[...]
    zqk = _dot(h, wqk_ref[...])
    zv = _dot(h, wv_ref[...])
    slot = 0
    for g, dil in enumerate(DSA_DILATIONS):
        for part in range(3):
            if part < 2:
                first = (part * DSA_QK_COLS // 2 + g * DSA_GROUP_COLS) // LANES
                val = jnp.concatenate(
                    [_rope_slab(zqk[:, s * LANES:(s + 1) * LANES], ca, sa, up_a, DSA_ROT // 2)
                     for s in range(first, first + DSA_GROUP_COLS // LANES)], axis=1)
            else:
                val = zv[:, g * DSA_GROUP_COLS:(g + 1) * DSA_GROUP_COLS]
            out_ref = dsa_refs[3 * g + part]
            if dil == 1:
                out_ref[0] = val.astype(BF16)
            else:
                for c in range(DSA_GROUP_COLS // LANES):
                    cs = slice(c * LANES, (c + 1) * LANES)
                    split_ref[slot, c] = val[:, cs]
                    for r in range(dil):
                        out_ref[r, :, cs] = split_ref[slot, c, pl.ds(r, rows // dil, stride=dil), :].astype(BF16)
                slot += 1
    c = _dot(h, wc_ref[...])
    cq_ref[...] = c[:, :MLA_Q_LORA]
    ckv_ref[...] = c[:, MLA_Q_LORA:]
    up_b = (lane >= MLA_NOPE) & (lane < MLA_NOPE + MLA_ROPE // 2)
    kpe_ref[...] = _rope_slab(_dot(h, wkpe_ref[...]), cb_ref[...], sb_ref[...], up_b, MLA_ROPE // 2)
    gate_ref[...] = jax.nn.sigmoid(_dot(h, wg_ref[...]) + bg_ref[...])


def _inproj(x2d, batch, seq, w, tabs, tm):
    t = x2d.shape[0]
    nseq = seq // tm
    row = lambda i: (i, 0)
    tab = lambda i: (i % nseq, 0)
    dsa_shapes, dsa_specs = [], []
    for dil in DSA_DILATIONS:
        for _ in range(3):
            dsa_shapes.append(jax.ShapeDtypeStruct((batch, dil, seq // dil, DSA_GROUP_COLS), BF16))
            dsa_specs.append(pl.BlockSpec((None, dil, tm // dil, DSA_GROUP_COLS),
                                          lambda i: (i // nseq, 0, i % nseq, 0)))
    flat_shapes = (
        jax.ShapeDtypeStruct((t, MLA_Q_LORA), F32),
        jax.ShapeDtypeStruct((t, MLA_KV_LORA), F32),
        jax.ShapeDtypeStruct((t, MLA_SLAB), F32),
        jax.ShapeDtypeStruct((t, 2 * D_MODEL), F32),
    )
    split_slots = 3 * sum(1 for dil in DSA_DILATIONS if dil > 1)
    return pl.pallas_call(
        _inproj_kernel,
        grid=(t // tm,),
        in_specs=[
            pl.BlockSpec((tm, D_MODEL), row),
            _const_spec((1, D_MODEL)),
            _const_spec(w['wqk'].shape), _const_spec(w['wv'].shape), _const_spec(w['wc'].shape),
            _const_spec(w['wkpe'].shape), _const_spec(w['wgate'].shape), _const_spec((1, 2 * D_MODEL)),
            pl.BlockSpec((tm, LANES), tab), pl.BlockSpec((tm, LANES), tab),
            pl.BlockSpec((tm, LANES), tab), pl.BlockSpec((tm, LANES), tab),
        ],
        out_specs=dsa_specs + [pl.BlockSpec((tm, s.shape[1]), row) for s in flat_shapes],
        out_shape=tuple(dsa_shapes) + flat_shapes,
        scratch_shapes=[pltpu.VMEM((split_slots, DSA_GROUP_COLS // LANES, tm, LANES), F32)],
        compiler_params=_params(("parallel",)),
        name="inproj",
    )(x2d, w['g_mix'], w['wqk'], w['wv'], w['wc'], w['wkpe'], w['wgate'], w['b_gate'],
      tabs['ca'], tabs['sa'], tabs['cb'], tabs['sb'])


def _mla_proj_kernel(cq_ref, ckv_ref, kpe_ref, gqa_ref, gkva_ref, wq_ref, wk_ref, wv_ref,
                     cb_ref, sb_ref, q_ref, k_ref, v_ref):
    qn = _rms(cq_ref[...], gqa_ref[...]).astype(BF16)
    rows = qn.shape[0]
    lane = lax.broadcasted_iota(jnp.int32, (rows, LANES), 1)
    up_b = (lane >= MLA_NOPE) & (lane < MLA_NOPE + MLA_ROPE // 2)
    cb = cb_ref[...]
    sb = sb_ref[...]
    q = _dot(qn, wq_ref[...])
    kvn = _rms(ckv_ref[...], gkva_ref[...]).astype(BF16)
    k = _dot(kvn, wk_ref[...])
    kpe = kpe_ref[...]
    scale = MLA_QK ** -0.5 * LOG2_E
    for s in range(MLA_HEADS):
        sl = slice(s * LANES, (s + 1) * LANES)
        q_ref[:, sl] = (_rope_slab(q[:, sl], cb, sb, up_b, MLA_ROPE // 2) * scale).astype(BF16)
        k_ref[:, sl] = (k[:, sl] + kpe).astype(BF16)
    wide_lane = lax.broadcasted_iota(jnp.int32, (1, MLA_HEADS * MLA_SLAB), 1)
    one_lane = jnp.where(wide_lane % MLA_SLAB == MLA_ONE_LANE, 1.0, 0.0)
    v_ref[...] = (_dot(kvn, wv_ref[...]) + one_lane).astype(BF16)


def _mla_proj(cq, ckv, kpe, seq, w, tabs, tm):
    t = cq.shape[0]
    nseq = seq // tm
    row = lambda i: (i, 0)
    tab = lambda i: (i % nseq, 0)
    wide = MLA_HEADS * MLA_SLAB
    out_shape = tuple(jax.ShapeDtypeStruct((t, wide), BF16) for _ in range(3))
    return pl.pallas_call(
        _mla_proj_kernel,
        grid=(t // tm,),
        in_specs=[
            pl.BlockSpec((tm, MLA_Q_LORA), row), pl.BlockSpec((tm, MLA_KV_LORA), row),
            pl.BlockSpec((tm, MLA_SLAB), row),
            _const_spec((1, MLA_Q_LORA)), _const_spec((1, MLA_KV_LORA)),
            _const_spec(w['wq_mla'].shape), _const_spec(w['wk_mla'].shape), _const_spec(w['wv_mla'].shape),
            pl.BlockSpec((tm, LANES), tab), pl.BlockSpec((tm, LANES), tab),
        ],
        out_specs=[pl.BlockSpec((tm, wide), row) for _ in range(3)],
        out_shape=out_shape,
        compiler_params=_params(("parallel",)),
        name="mla_proj",
    )(cq, ckv, kpe, w['g_qa'], w['g_kva'], w['wq_mla'], w['wk_mla'], w['wv_mla'], tabs['cb'], tabs['sb'])


def _mla_attn_kernel(q_ref, k_ref, v_ref, o_ref, m_ref, acc_ref):
    j = pl.program_id(3)

    @pl.when(j == 0)
    def _():
        m_ref[...] = jnp.full(m_ref.shape, -jnp.inf, F32)
        acc_ref[...] = jnp.zeros(acc_ref.shape, F32)

    tk = k_ref.shape[0]
    for h in range(MLA_HEADS_PER_STEP):
        sl = slice(h * MLA_SLAB, (h + 1) * MLA_SLAB)
        q = q_ref[:, sl]
        m_run = m_ref[:, sl]
        acc = acc_ref[:, sl]
        for c in range(tk // MLA_KEY_CHUNK):
            rows = slice(c * MLA_KEY_CHUNK, (c + 1) * MLA_KEY_CHUNK)
            s = _dot_nt(q, k_ref[rows, sl])
            cols = [s[:, i * LANES:(i + 1) * LANES] for i in range(MLA_KEY_CHUNK // LANES)]
            m_loc = functools.reduce(jnp.maximum, cols)
            m_new = jnp.maximum(m_run, jnp.max(m_loc, axis=1, keepdims=True))
            p = jnp.concatenate([jnp.exp2(col - m_new) for col in cols], axis=1).astype(BF16)
            acc = jnp.exp2(m_run - m_new) * acc + _dot(p, v_ref[rows, sl])
            m_run = m_new
        acc_ref[:, sl] = acc
        m_ref[:, sl] = m_run

    @pl.when(j == pl.num_programs(3) - 1)
    def _():
        for h in range(MLA_HEADS_PER_STEP):
            sl = slice(h * MLA_SLAB, (h + 1) * MLA_SLAB)
            acc = acc_ref[:, sl]
            den = acc[:, MLA_ONE_LANE:MLA_ONE_LANE + 1]
            o_ref[:, sl] = (acc / den).astype(o_ref.dtype)


def _mla_attn(q, k, v, tq, tk):
    b, s, wide = q.shape
    cols = MLA_HEADS_PER_STEP * MLA_SLAB
    return pl.pallas_call(
        _mla_attn_kernel,
        grid=(b, MLA_HEADS // MLA_HEADS_PER_STEP, s // tq, s // tk),
        in_specs=[
            pl.BlockSpec((None, tq, cols), lambda bi, h, i, j: (bi, i, h)),
            pl.BlockSpec((None, tk, cols), lambda bi, h, i, j: (bi, j, h)),
            pl.BlockSpec((None, tk, cols), lambda bi, h, i, j: (bi, j, h)),
        ],
        out_specs=pl.BlockSpec((None, tq, cols), lambda bi, h, i, j: (bi, i, h)),
        out_shape=jax.ShapeDtypeStruct((b, s, wide), BF16),
        scratch_shapes=[pltpu.VMEM((tq, cols), F32), pltpu.VMEM((tq, cols), F32)],
        compiler_params=_params(("parallel", "parallel", "parallel", "arbitrary")),
        name="mla_attn",
    )(q, k, v)


def _dsa_kernel(q_ref, kp_ref, kc_ref, kn_ref, vp_ref, vc_ref, vn_ref, o_ref, lse_ref, *, cls_len):
    j = pl.program_id(2)
    tq = q_ref.shape[0]
    nk = tq + 2 * DSA_HALF
    q = q_ref[...]
    kk = jnp.concatenate([kp_ref[...], kc_ref[...], kn_ref[...]], axis=0)
    vv = jnp.concatenate([vp_ref[...], vc_ref[...], vn_ref[...]], axis=0)
    qpos = j * tq + lax.broadcasted_iota(jnp.int32, (tq, nk), 0)
    kpos = j * tq - DSA_HALF + lax.broadcasted_iota(jnp.int32, (tq, nk), 1)
    ok = (jnp.abs(kpos - qpos) <= DSA_HALF) & (kpos >= 0) & (kpos < cls_len)
    lane = lax.broadcasted_iota(jnp.int32, (tq, DSA_GROUP_COLS), 1)
    heads = range(DSA_HEADS)
    in_head = [(lane // DSA_HEAD_DIM) == h for h in heads]
    s = [jnp.where(ok, _dot_nt(jnp.where(in_head[h], q, jnp.zeros_like(q)), kk), NEG_INF) for h in heads]
    m = [jnp.max(s[h], axis=1, keepdims=True) for h in heads]
    p = [jnp.exp(s[h] - m[h]) for h in heads]
    den = [jnp.sum(p[h], axis=1, keepdims=True) for h in heads]
    oh = [_dot((p[h] / den[h]).astype(BF16), vv) for h in heads]
    o = jnp.zeros((tq, DSA_GROUP_COLS), F32)
    lse = jnp.zeros((tq, DSA_GROUP_COLS), F32)
    for h in heads:
        o = jnp.where(in_head[h], oh[h], o)
        lse = jnp.where(in_head[h], m[h] + jnp.log(den[h]), lse)
    o_ref[...] = o
    lse_ref[...] = lse


def _dsa_group(q, k, v, g, tq):
    b, dil, cls_len, _ = q.shape
    tq = min(tq, cls_len)
    per = tq // DSA_HALF
    last = cls_len // DSA_HALF - 1
    prev = lambda j: jnp.maximum(j * per - 1, 0)
    nxt = lambda j: jnp.minimum((j + 1) * per, last)
    halo = (None, None, DSA_HALF, DSA_GROUP_COLS)
    cur = (None, None, tq, DSA_GROUP_COLS)
    cur_spec = pl.BlockSpec(cur, lambda bi, r, j: (bi, r, j, 0))
    prev_spec = pl.BlockSpec(halo, lambda bi, r, j: (bi, r, prev(j), 0))
    next_spec = pl.BlockSpec(halo, lambda bi, r, j: (bi, r, nxt(j), 0))
    out_sd = jax.ShapeDtypeStruct((b, dil, cls_len, DSA_GROUP_COLS), F32)
    return pl.pallas_call(
        functools.partial(_dsa_kernel, cls_len=cls_len),
        grid=(b, dil, cls_len // tq),
        in_specs=[cur_spec, prev_spec, cur_spec, next_spec, prev_spec, cur_spec, next_spec],
        out_specs=[cur_spec, cur_spec],
        out_shape=(out_sd, out_sd),
        compiler_params=_params(("parallel", "parallel", "parallel")),
        name=f"dsa_g{g}",
    )(q, k, k, k, v, v, v)


def _merge_kernel(o0_ref, l0_ref, o1_ref, l1_ref, o2_ref, l2_ref, yb_ref, gate_ref, x_ref,
                  wpa_ref, wpb_ref, wout_ref, out_ref, join_ref):
    def position_order(ref, slot):
        dil, per_class, _ = ref.shape
        if dil == 1:
            return ref[0]
        halves = []
        for c in range(DSA_GROUP_COLS // LANES):
            for r in range(dil):
                join_ref[slot, c, pl.ds(r, per_class, stride=dil), :] = ref[r, :, c * LANES:(c + 1) * LANES]
            halves.append(join_ref[slot, c])
        return jnp.concatenate(halves, axis=1)

    o0, l0 = position_order(o0_ref, 0), position_order(l0_ref, 0)
    o1, l1 = position_order(o1_ref, 0), position_order(l1_ref, 1)
    o2, l2 = position_order(o2_ref, 2), position_order(l2_ref, 3)
    m = jnp.maximum(jnp.maximum(l0, l1), l2)
    e0, e1, e2 = jnp.exp(l0 - m), jnp.exp(l1 - m), jnp.exp(l2 - m)
    den = e0 + e1 + e2
    ya = (e0 / den) * o0 + (e1 / den) * o1 + (e2 / den) * o2
    gate = gate_ref[...]
    pa = _dot(ya.astype(BF16), wpa_ref[...])
    pb = _dot(yb_ref[...], wpb_ref[...])
    merged = gate[:, :D_MODEL] * pa + gate[:, D_MODEL:] * pb
    out_ref[...] = x_ref[...] + _dot(merged.astype(BF16), wout_ref[...])


def _merge(dsa_out, yb, gates, x2d, seq, w, tm):
    t = x2d.shape[0]
    nseq = seq // tm
    row = lambda i: (i, 0)
    flat, specs = [], []
    for pair in dsa_out:
        for a in pair:
            dil = a.shape[1]
            flat.append(a)
            specs.append(pl.BlockSpec((None, dil, tm // dil, DSA_GROUP_COLS),
                                      lambda i: (i // nseq, 0, i % nseq, 0)))
    return pl.pallas_call(
        _merge_kernel,
        grid=(t // tm,),
        in_specs=specs + [
            pl.BlockSpec((tm, MLA_HEADS * MLA_SLAB), row),
            pl.BlockSpec((tm, 2 * D_MODEL), row),
            pl.BlockSpec((tm, D_MODEL), row),
            _const_spec(w['wpa'].shape), _const_spec(w['wpb'].shape), _const_spec(w['wout'].shape),
        ],
        out_specs=pl.BlockSpec((tm, D_MODEL), row),
        out_shape=jax.ShapeDtypeStruct((t, D_MODEL), F32),
        scratch_shapes=[pltpu.VMEM((4, DSA_GROUP_COLS // LANES, tm, LANES), F32)],
        compiler_params=_params(("parallel",)),
        name="merge",
    )(*flat, yb, gates, x2d, w['wpa'], w['wpb'], w['wout'])


def _norm_mm_kernel(x_ref, g_ref, w_ref, o_ref):
    o_ref[...] = _dot(_rms(x_ref[...], g_ref[...]).astype(BF16), w_ref[...]).astype(o_ref.dtype)


def _norm_mm(x2d, g, wmat, tm):
    t, d = x2d.shape
    n = wmat.shape[1]
    return pl.pallas_call(
        _norm_mm_kernel,
        grid=(t // tm,),
        in_specs=[pl.BlockSpec((tm, d), lambda i: (i, 0)), _const_spec((1, d)), _const_spec(wmat.shape)],
        out_specs=pl.BlockSpec((tm, n), lambda i: (i, 0)),
        out_shape=jax.ShapeDtypeStruct((t, n), BF16),
        compiler_params=_params(("parallel",)),
        name="mem_kv",
    )(x2d, g, wmat)


def _xattn_kernel(x_ref, g_ref, kv_ref, wq_ref, wo_ref, out_ref):
    x = x_ref[...]
    hq = _rms(x, g_ref[...]).astype(BF16)
    q = (_dot(hq, wq_ref[...]) * (MEM_HEAD_DIM ** -0.5)).astype(BF16)
    kv = kv_ref[...]
    outs = []
    for h in range(MEM_HEADS):
        sl = slice(h * MEM_HEAD_DIM, (h + 1) * MEM_HEAD_DIM)
        vsl = slice(D_MODEL + h * MEM_HEAD_DIM, D_MODEL + (h + 1) * MEM_HEAD_DIM)
        s = _dot_nt(q[:, sl], kv[:, sl])
        m = jnp.max(s, axis=1, keepdims=True)
        p = jnp.exp(s - m)
        den = jnp.sum(p, axis=1, keepdims=True)
        outs.append(_dot((p / den).astype(BF16), kv[:, vsl]).astype(BF16))
    o = jnp.concatenate(outs, axis=1)
    out_ref[...] = x + _dot(o, wo_ref[...])


def _xattn(x3d, kv3d, w, tm):
    b, s, d = x3d.shape
    n_mem = kv3d.shape[1]
    return pl.pallas_call(
        _xattn_kernel,
        grid=(b, s // tm),
        in_specs=[
            pl.BlockSpec((None, tm, d), lambda bi, i: (bi, i, 0)),
            _const_spec((1, d)),
            pl.BlockSpec((None, n_mem, 2 * d), lambda bi, i: (bi, 0, 0)),
            _const_spec(w['wmq'].shape), _const_spec(w['wmo'].shape),
        ],
        out_specs=pl.BlockSpec((None, tm, d), lambda bi, i: (bi, i, 0)),
        out_shape=jax.ShapeDtypeStruct((b, s, d), F32),
        compiler_params=_params(("parallel", "parallel")),
        name="xattn",
    )(x3d, w['g_xq'], kv3d, w['wmq'], w['wmo'])


def _top_values(sc, count, with_rank, break_ties):
    n = sc.shape[0]
    row = lax.broadcasted_iota(jnp.int32, sc.shape, 0)
    rank = jnp.full(sc.shape, float(n - 1), F32)
    vals = []
    for k in range(count):
        m = jnp.max(sc, axis=0, keepdims=True)
        hit = sc == m
        if break_ties:
            hit = row == jnp.min(jnp.where(hit, row, n), axis=0, keepdims=True)
        vals.append(m)
        if with_rank:
            rank = jnp.where(hit, float(k), rank)
        sc = jnp.where(hit, -jnp.inf, sc)
    dropped = jnp.sum(jnp.where(sc == -jnp.inf, 1.0, 0.0), axis=0, keepdims=True)
    return jnp.concatenate(vals, axis=0), rank, dropped


def _selected_per_first_key(s1, v1, v2, tau):
    need = []
    for j in range(PEER_TOPK):
        reach = (v1 + v2[j:j + 1]) >= tau
        need.append(jnp.min(jnp.where(reach, v1, jnp.inf), axis=0, keepdims=True))
    c8 = s1 >= need[7]
    c4 = s1 >= jnp.where(c8, need[11], need[3])
    c2 = s1 >= jnp.where(c8, jnp.where(c4, need[13], need[9]), jnp.where(c4, need[5], need[1]))
    low = jnp.where(c4, jnp.where(c2, need[6], need[4]), jnp.where(c2, need[2], need[0]))
    high = jnp.where(c4, jnp.where(c2, need[14], need[12]), jnp.where(c2, need[10], need[8]))
    c1 = s1 >= jnp.where(c8, high, low)
    c16 = s1 >= need[15]
    return (jnp.where(c8, 8.0, 0.0) + jnp.where(c4, 4.0, 0.0) + jnp.where(c2, 2.0, 0.0)
            + jnp.where(c1, 1.0, 0.0) + jnp.where(c16, 1.0, 0.0))


def _pack_bf16_rows(x):
    return pltpu.bitcast(x.astype(BF16), jnp.int32)


def _unpack_bf16_rows(words):
    return pltpu.bitcast(words, BF16)


def _bf16_pair_word(x):
    bits = lax.bitcast_convert_type(x.astype(BF16).astype(F32), jnp.uint32)
    return lax.bitcast_convert_type(bits | (bits >> 16), jnp.int32)


def _peer_score_kernel(x_ref, g_ref, wpq_ref, keys_ref, hqt_ref, r2s_ref, cnts_ref, e1s_ref, e2s_ref):
    hq32 = _rms(x_ref[...], g_ref[...])

    @pl.when(pl.program_id(1) == 0)
    def _():
        hqt_ref[...] = _pack_bf16_rows(hq32.T)

    q = _dot(hq32.astype(BF16), wpq_ref[...]).astype(BF16)
    heads = keys_ref.shape[0]
    halves = [[_dot_nt(keys_ref[h, p], q[:, (2 * h + p) * PEER_HALF:(2 * h + p + 1) * PEER_HALF])
               for p in range(2)] for h in range(heads)]

    def statistics(break_ties):
        worst = [head_statistics(h, break_ties) for h in range(heads)]
        return functools.reduce(jnp.maximum, worst)

    def head_statistics(h, break_ties):
        s1, s2 = halves[h]
        r2_ref, cnt_ref, e1_ref, e2_ref = (ref.at[h] for ref in (r2s_ref, cnts_ref, e1s_ref, e2s_ref))
        v1, _, d1 = _top_values(s1, PEER_TOPK, False, break_ties)
        v2, r2, d2 = _top_values(s2, PEER_TOPK, True, break_ties)
        cand = [v1[0:1] + v2]
        cand += [v1[i:i + 1] + v2[0:8] for i in range(1, 8)]
        cand += [v1[8:16] + v2[0:1]]
        top, _, d3 = _top_values(jnp.concatenate(cand, axis=0), PEER_TOPK, False, break_ties)
        z = jnp.sum(jnp.exp(top - top[0:1]), axis=0, keepdims=True)
        tau = top[PEER_TOPK - 1:PEER_TOPK]
        cnt = _selected_per_first_key(s1, v1, v2, tau)
        r2_ref[...] = _pack_bf16_rows(r2)
        cnt_ref[...] = _bf16_pair_word(cnt)
        e1_ref[...] = _bf16_pair_word(jnp.exp(s1 - v1[0:1]) * (0.5 / z))
        e2_ref[...] = _pack_bf16_rows(jnp.exp(s2 - v2[0:1]))
        return jnp.maximum(jnp.maximum(d1, d2), d3)

    most_dropped = statistics(break_ties=False)

    @pl.when(jnp.max(most_dropped) > PEER_TOPK)
    def _():
        statistics(break_ties=True)


def _peer_scores(x2d, w):
    t = x2d.shape[0]
    tt = PEER_TOKENS
    groups = t // tt
    words = lambda rows: jax.ShapeDtypeStruct((groups, PEER_HEADS, rows, tt), jnp.int32)
    hs = PEER_SCORE_HEADS
    spec = lambda rows: pl.BlockSpec((None, hs, rows, tt), lambda i, h: (i, h, 0, 0))
    half = PEER_NKEYS // 2
    return pl.pallas_call(
        _peer_score_kernel,
        grid=(groups, PEER_HEADS // hs),
        in_specs=[
            pl.BlockSpec((tt, D_MODEL), lambda i, h: (i, 0)),
            _const_spec((1, D_MODEL)),
            pl.BlockSpec((D_MODEL, hs * 2 * PEER_HALF), lambda i, h: (0, h)),
            pl.BlockSpec((hs, 2, PEER_NKEYS, PEER_HALF), lambda i, h: (h, 0, 0, 0)),
        ],
        out_specs=[pl.BlockSpec((None, D_MODEL // 2, tt), lambda i, h: (i, 0, 0)),
                   spec(half), spec(PEER_NKEYS), spec(PEER_NKEYS), spec(half)],
        out_shape=(jax.ShapeDtypeStruct((groups, D_MODEL // 2, tt), jnp.int32),
                   words(half), words(PEER_NKEYS), words(PEER_NKEYS), words(half)),
        compiler_params=_params(("parallel", "arbitrary")),
        name="peer_scores",
    )(x2d, w['g_ffn'], w['wpq'], w['keys'])


def _peer_mix_kernel(hqt_ref, u_ref, vt_ref, r2_ref, cnt_ref, e1_ref, e2_ref, x_ref, gf_ref,
                     out_ref, acc_ref, wg_ref, *, blocks):
    j = pl.program_id(1)
    last = pl.num_programs(1) - 1
    groups, _, tt = hqt_ref.shape
    half_rows = PEER_NKEYS // 2

    @pl.when(j == 0)
    def _():
        acc_ref[...] = jnp.zeros(acc_ref.shape, F32)
        wg_ref[...] = jnp.zeros(wg_ref.shape, jnp.int32)

    def mix_previous_block(g):
        acc_ref[g] += _dot(_unpack_bf16_rows(vt_ref[...]), _unpack_bf16_rows(wg_ref[g]))

    def row_bcast(words):
        return _unpack_bf16_rows(jnp.broadcast_to(words, (half_rows, LANES)))

    def activate_and_mix(g, carry):
        mix_previous_block(g)
        hqt = _unpack_bf16_rows(hqt_ref[g])
        for c in range(blocks // 2):
            u_pair = _unpack_bf16_rows(u_ref[c * PEER_NKEYS:(c + 1) * PEER_NKEYS, :])
            pre2 = _dot(u_pair, hqt)
            for half in range(2):
                i = 2 * c + half
                a = j * blocks + i
                cnt_rows = [cnt_ref[g, h, pl.ds(a, 1), :] for h in range(PEER_HEADS)]
                e1_rows = [e1_ref[g, h, pl.ds(a, 1), :] for h in range(PEER_HEADS)]
                for col in range(tt // LANES):
                    cs = slice(col * LANES, (col + 1) * LANES)
                    wsum = jnp.zeros((PEER_NKEYS, LANES), BF16)
                    for h in range(PEER_HEADS):
                        chosen = _unpack_bf16_rows(r2_ref[g, h, :, cs]) < row_bcast(cnt_rows[h][:, cs])
                        picked = jnp.where(chosen, _unpack_bf16_rows(e2_ref[g, h, :, cs]), jnp.zeros((), BF16))
                        wsum = wsum + picked * row_bcast(e1_rows[h][:, cs])
                    pre = pre2[half * PEER_NKEYS:(half + 1) * PEER_NKEYS, cs]
                    act = pre * (1.0 + lax.erf(pre * (0.5 ** 0.5)))
                    wg_ref[g, i * half_rows:(i + 1) * half_rows, cs] = pltpu.bitcast(
                        wsum * act.astype(BF16), jnp.int32)
        return carry

    @pl.when(j < last)
    def _():
        for g in range(groups):
            activate_and_mix(g, 0)

    @pl.when(j == last)
    def _():
        for g in range(groups):
            mix_previous_block(g)
            rows = slice(g * tt, (g + 1) * tt)
            out_ref[rows, :] = _rms(x_ref[rows, :] + acc_ref[g].T, gf_ref[...])


def _peer_mix(hqt, stats, x2d, w, groups, blocks):
    t = x2d.shape[0]
    tt = PEER_TOKENS
    rows = groups * tt
    ne = blocks * PEER_NKEYS
    nblk = PEER_NKEYS // blocks
    spec = lambda r: pl.BlockSpec((groups, PEER_HEADS, r, tt), lambda i, j: (i, 0, 0, 0))
    half = PEER_NKEYS // 2
    return pl.pallas_call(
        functools.partial(_peer_mix_kernel, blocks=blocks),
        grid=(t // rows, nblk + 1),
        in_specs=[
            pl.BlockSpec((groups, D_MODEL // 2, tt), lambda i, j: (i, 0, 0)),
            pl.BlockSpec((ne // 2, D_MODEL), lambda i, j: (jnp.minimum(j, nblk - 1), 0)),
            pl.BlockSpec((D_MODEL // 2, ne), lambda i, j: (0, jnp.maximum(j - 1, 0))),
            spec(half), spec(PEER_NKEYS), spec(PEER_NKEYS), spec(half),
            pl.BlockSpec((rows, D_MODEL), lambda i, j: (i, 0)),
            _const_spec((1, D_MODEL)),
        ],
        out_specs=pl.BlockSpec((rows, D_MODEL), lambda i, j: (i, 0)),
        out_shape=jax.ShapeDtypeStruct((t, D_MODEL), F32),
        scratch_shapes=[pltpu.VMEM((groups, D_MODEL, tt), F32), pltpu.VMEM((groups, ne // 2, tt), jnp.int32)],
        compiler_params=_params(("parallel", "arbitrary")),
        name="peer_mix",
    )(hqt, w['peer_u'], w['peer_vt'], *stats, x2d, w['g_final'])


def _pack_rows_host(a):
    m2, n = a.shape
    return lax.bitcast_convert_type(jnp.swapaxes(a.reshape(m2 // 2, 2, n), 1, 2), jnp.int32)


def _prep_weights(g_mix, w_in, g_qa, g_kva, w_uq, w_ukv, w_gate, b_gate, w_pa, w_pb, w_out,
                  g_xq, g_mkv, w_mq, w_mkv, w_mo, g_ffn, w_pq, sub_keys, peer_u, peer_v, g_final):
    w_in = w_in[0]
    q_cols = DSA_QK_COLS // 2
    wqk = jnp.concatenate([w_in[:, :q_cols] * (DSA_HEAD_DIM ** -0.5), w_in[:, q_cols:DSA_QK_COLS]], axis=1)
    c0 = DSA_QK_COLS + DSA_V_COLS
    c1 = c0 + MLA_Q_LORA + MLA_KV_LORA
    wkpe = jnp.zeros((D_MODEL, MLA_SLAB), F32).at[:, MLA_NOPE:MLA_QK].set(w_in[:, c1:])
    pad_head = lambda a: jnp.pad(a, ((0, 0), (0, 0), (0, MLA_SLAB - a.shape[2])))
    wq = pad_head(w_uq[0].reshape(MLA_Q_LORA, MLA_HEADS, MLA_QK)).reshape(MLA_Q_LORA, -1)
    ukv = w_ukv[0].reshape(MLA_KV_LORA, MLA_HEADS, MLA_NOPE + MLA_V)
    wk = pad_head(ukv[:, :, :MLA_NOPE]).reshape(MLA_KV_LORA, -1)
    wv = pad_head(ukv[:, :, MLA_NOPE:]).reshape(MLA_KV_LORA, -1)
    wpb = jnp.pad(w_pb[0].reshape(MLA_HEADS, MLA_V, D_MODEL), ((0, 0), (0, MLA_SLAB - MLA_V), (0, 0)))
    bf = lambda a: a.astype(BF16)
    return {
        'g_mix': g_mix, 'wqk': bf(wqk), 'wv': bf(w_in[:, DSA_QK_COLS:c0]), 'wc': bf(w_in[:, c0:c1]),
        'wkpe': bf(wkpe), 'wgate': bf(w_gate[0]), 'b_gate': b_gate,
        'g_qa': g_qa, 'g_kva': g_kva, 'wq_mla': bf(wq), 'wk_mla': bf(wk), 'wv_mla': bf(wv),
        'wpa': bf(w_pa[0]), 'wpb': bf(wpb.reshape(-1, D_MODEL)), 'wout': bf(w_out[0]),
        'g_xq': g_xq, 'g_mkv': g_mkv, 'wmq': bf(w_mq[0]), 'wmkv': bf(w_mkv[0]), 'wmo': bf(w_mo[0]),
        'g_ffn': g_ffn, 'wpq': bf(w_pq[0]),
        'keys': bf(sub_keys[0]),
        'peer_u': _pack_rows_host(bf(peer_u[0])), 'peer_vt': _pack_rows_host(bf(peer_v[0]).T),
        'g_final': g_final.reshape(1, D_MODEL),
    }


def _rope_tables(seq):
    def table(rot, lead, period):
        inv = ROPE_THETA ** (-jnp.arange(0, rot, 2, dtype=F32) / rot)
        ang = jnp.arange(seq, dtype=F32)[:, None] * inv[None, :]
        cos, sin = jnp.cos(ang), jnp.sin(ang)
        ones = lambda n: jnp.ones((seq, n), F32)
        zeros = lambda n: jnp.zeros((seq, n), F32)
        tail = period - lead - rot
        c = jnp.concatenate([ones(lead), cos, cos, ones(tail)], axis=1)
        s = jnp.concatenate([zeros(lead), -sin, sin, zeros(tail)], axis=1)
        reps = LANES // period
        return jnp.tile(c, (1, reps)), jnp.tile(s, (1, reps))
    ca, sa = table(DSA_ROT, 0, DSA_HEAD_DIM)
    cb, sb = table(MLA_ROPE, MLA_NOPE, MLA_SLAB)
    return {'ca': ca, 'sa': sa, 'cb': cb, 'sb': sb}


def _trunk(x, mem, w):
    b, s, d = x.shape
    t = b * s
    tabs = _rope_tables(s)
    x2d = x.reshape(t, d)
    tm = ROW_TILE
    *dsa_in, cq, ckv, kpe, gates = _inproj(x2d, b, s, w, tabs, tm)
    dsa_out = [_dsa_group(*dsa_in[3 * g:3 * g + 3], g, DSA_QUERY_TILE) for g in range(len(DSA_DILATIONS))]
    q_b, k_b, v_b = _mla_proj(cq, ckv, kpe, s, w, tabs, tm)
    wide = MLA_HEADS * MLA_SLAB
    yb = _mla_attn(q_b.reshape(b, s, wide), k_b.reshape(b, s, wide), v_b.reshape(b, s, wide),
                   min(MLA_QUERY_TILE, s), min(MLA_KEY_TILE, s))
    x1 = _merge(dsa_out, yb.reshape(t, wide), gates, x2d, s, w, tm)
    n_mem = mem.shape[1]
    kv = _norm_mm(mem.reshape(b * n_mem, d), w['g_mkv'], w['wmkv'], n_mem)
    x2 = _xattn(x1.reshape(b, s, d), kv.reshape(b, n_mem, 2 * d), w, tm)
    x2 = x2.reshape(t, d)
    hqt, *stats = _peer_scores(x2, w)
    y = _peer_mix(hqt, stats, x2, w, PEER_MIX_GROUPS, PEER_MIX_BLOCKS)
    return y.reshape(b, s, d)


def kernel(x_prompt, x_sample, mem_prompt, mem_sample, g_mix, w_in, g_qa, g_kva, w_uq, w_ukv, w_gate,
           b_gate, w_pa, w_pb, w_out, g_xq, g_mkv, w_mq, w_mkv, w_mo, g_ffn, w_pq, sub_keys, peer_u,
           peer_v, g_final):
    w = _prep_weights(g_mix, w_in, g_qa, g_kva, w_uq, w_ukv, w_gate, b_gate, w_pa, w_pb, w_out,
                      g_xq, g_mkv, w_mq, w_mkv, w_mo, g_ffn, w_pq, sub_keys, peer_u, peer_v, g_final)
    return (_trunk(x_prompt, mem_prompt, w), _trunk(x_sample, mem_sample, w))
```

```python
import functools

import jax
import jax.numpy as jnp
from jax import lax
from jax.experimental import pallas as pl
from jax.experimental.pallas import tpu as pltpu

F32 = jnp.float32
BF16 = jnp.bfloat16

D_MODEL = 1024
EPS = 1e-6
NEG_INF = -1e30
ROPE_THETA = 500000.0

DSA_DILATIONS = (1, 4, 16)
DSA_HALF = 64
DSA_HEADS = 4
DSA_HEAD_DIM = 64
DSA_ROT = 16
DSA_GROUP_COLS = DSA_HEADS * DSA_HEAD_DIM
DSA_QK_COLS = 2 * 3 * DSA_GROUP_COLS
DSA_V_COLS = 3 * DSA_GROUP_COLS

MLA_HEADS = 8
MLA_Q_LORA = 768
MLA_KV_LORA = 256
MLA_NOPE = 64
MLA_ROPE = 32
MLA_V = 64
MLA_QK = MLA_NOPE + MLA_ROPE
MLA_SLAB = 128
MLA_ONE_LANE = MLA_V
MLA_HEADS_PER_STEP = 1
MLA_KEY_CHUNK = 512
LOG2_E = 1.4426950408889634

MEM_HEADS = 4
MEM_HEAD_DIM = 256

PEER_HEADS = 8
PEER_NKEYS = 128
PEER_HALF = 128
PEER_TOPK = 16
PEER_TOKENS = 256
PEER_SCORE_HEADS = 4

LANES = 128
VMEM_LIMIT_BYTES = 56 * 1024 * 1024

ROW_TILE = 512
DSA_QUERY_TILE = 256
MLA_QUERY_TILE = 4096
MLA_KEY_TILE = 2048
PEER_MIX_GROUPS = 2
PEER_MIX_BLOCKS = 16


def _params(sem):
    return pltpu.CompilerParams(dimension_semantics=sem, vmem_limit_bytes=VMEM_LIMIT_BYTES)


def _const_spec(shape):
    nd = len(shape)
    return pl.BlockSpec(shape, lambda *_: (0,) * nd, pipeline_mode=pl.Buffered(1))


def _rms(x, g):
    y = x * lax.rsqrt(jnp.mean(x * x, axis=-1, keepdims=True) + EPS)
    return y * g


def _dot(a, b):
    return jnp.dot(a, b, preferred_element_type=F32)


def _dot_nt(a, b):
    return lax.dot_general(a, b, (((1,), (1,)), ((), ())), preferred_element_type=F32)


def _rope_slab(x, cos_t, sin_t, take_up, half):
    up = pltpu.roll(x, LANES - half, 1)
    dn = pltpu.roll(x, half, 1)
    return x * cos_t + jnp.where(take_up, up, dn) * sin_t


def _inproj_kernel(x_ref, g_ref, wqk_ref, wv_ref, wc_ref, wkpe_ref, wg_ref, bg_ref,
                   ca_ref, sa_ref, cb_ref, sb_ref, *rest):
    dsa_refs = rest[:9]
    cq_ref, ckv_ref, kpe_ref, gate_ref, split_ref = rest[9:]
    h = _rms(x_ref[...], g_ref[...]).astype(BF16)
    rows = h.shape[0]
    lane = lax.broadcasted_iota(jnp.int32, (rows, LANES), 1)
    up_a = (lane % DSA_HEAD_DIM) < (DSA_ROT // 2)
    ca = ca_ref[...]
    sa = sa_ref[...]
    zqk = _dot(h, wqk_ref[...])
    zv = _dot(h, wv_ref[...])
    slot = 0
    for g, dil in enumerate(DSA_DILATIONS):
        for part in range(3):
            if part < 2:
                first = (part * DSA_QK_COLS // 2 + g * DSA_GROUP_COLS) // LANES
                val = jnp.concatenate(
                    [_rope_slab(zqk[:, s * LANES:(s + 1) * LANES], ca, sa, up_a, DSA_ROT // 2)
                     for s in range(first, first + DSA_GROUP_COLS // LANES)], axis=1)
            else:
                val = zv[:, g * DSA_GROUP_COLS:(g + 1) * DSA_GROUP_COLS]
            out_ref = dsa_refs[3 * g + part]
            if dil == 1:
                out_ref[0] = val.astype(BF16)
            else:
                for c in range(DSA_GROUP_COLS // LANES):
                    cs = slice(c * LANES, (c + 1) * LANES)
                    split_ref[slot, c] = val[:, cs]
                    for r in range(dil):
                        out_ref[r, :, cs] = split_ref[slot, c, pl.ds(r, rows // dil, stride=dil), :].astype(BF16)
                slot += 1
    c = _dot(h, wc_ref[...])
    cq_ref[...] = c[:, :MLA_Q_LORA]
    ckv_ref[...] = c[:, MLA_Q_LORA:]
    up_b = (lane >= MLA_NOPE) & (lane < MLA_NOPE + MLA_ROPE // 2)
    kpe_ref[...] = _rope_slab(_dot(h, wkpe_ref[...]), cb_ref[...], sb_ref[...], up_b, MLA_ROPE // 2)
    gate_ref[...] = jax.nn.sigmoid(_dot(h, wg_ref[...]) + bg_ref[...])


def _inproj(x2d, batch, seq, w, tabs, tm):
    t = x2d.shape[0]
    nseq = seq // tm
    row = lambda i: (i, 0)
    tab = lambda i: (i % nseq, 0)
    dsa_shapes, dsa_specs = [], []
    for dil in DSA_DILATIONS:
        for _ in range(3):
            dsa_shapes.append(jax.ShapeDtypeStruct((batch, dil, seq // dil, DSA_GROUP_COLS), BF16))
            dsa_specs.append(pl.BlockSpec((None, dil, tm // dil, DSA_GROUP_COLS),
                                          lambda i: (i // nseq, 0, i % nseq, 0)))
    flat_shapes = (
        jax.ShapeDtypeStruct((t, MLA_Q_LORA), F32),
        jax.ShapeDtypeStruct((t, MLA_KV_LORA), F32),
        jax.ShapeDtypeStruct((t, MLA_SLAB), F32),
        jax.ShapeDtypeStruct((t, 2 * D_MODEL), F32),
    )
    split_slots = 3 * sum(1 for dil in DSA_DILATIONS if dil > 1)
    return pl.pallas_call(
        _inproj_kernel,
        grid=(t // tm,),
        in_specs=[
            pl.BlockSpec((tm, D_MODEL), row),
            _const_spec((1, D_MODEL)),
            _const_spec(w['wqk'].shape), _const_spec(w['wv'].shape), _const_spec(w['wc'].shape),
            _const_spec(w['wkpe'].shape), _const_spec(w['wgate'].shape), _const_spec((1, 2 * D_MODEL)),
            pl.BlockSpec((tm, LANES), tab), pl.BlockSpec((tm, LANES), tab),
            pl.BlockSpec((tm, LANES), tab), pl.BlockSpec((tm, LANES), tab),
        ],
        out_specs=dsa_specs + [pl.BlockSpec((tm, s.shape[1]), row) for s in flat_shapes],
        out_shape=tuple(dsa_shapes) + flat_shapes,
        scratch_shapes=[pltpu.VMEM((split_slots, DSA_GROUP_COLS // LANES, tm, LANES), F32)],
        compiler_params=_params(("parallel",)),
        name="inproj",
    )(x2d, w['g_mix'], w['wqk'], w['wv'], w['wc'], w['wkpe'], w['wgate'], w['b_gate'],
      tabs['ca'], tabs['sa'], tabs['cb'], tabs['sb'])


def _mla_proj_kernel(cq_ref, ckv_ref, kpe_ref, gqa_ref, gkva_ref, wq_ref, wk_ref, wv_ref,
                     cb_ref, sb_ref, q_ref, k_ref, v_ref):
    qn = _rms(cq_ref[...], gqa_ref[...]).astype(BF16)
    rows = qn.shape[0]
    lane = lax.broadcasted_iota(jnp.int32, (rows, LANES), 1)
    up_b = (lane >= MLA_NOPE) & (lane < MLA_NOPE + MLA_ROPE // 2)
    cb = cb_ref[...]
    sb = sb_ref[...]
    q = _dot(qn, wq_ref[...])
    kvn = _rms(ckv_ref[...], gkva_ref[...]).astype(BF16)
    k = _dot(kvn, wk_ref[...])
    kpe = kpe_ref[...]
    scale = MLA_QK ** -0.5 * LOG2_E
    for s in range(MLA_HEADS):
        sl = slice(s * LANES, (s + 1) * LANES)
        q_ref[:, sl] = (_rope_slab(q[:, sl], cb, sb, up_b, MLA_ROPE // 2) * scale).astype(BF16)
        k_ref[:, sl] = (k[:, sl] + kpe).astype(BF16)
    wide_lane = lax.broadcasted_iota(jnp.int32, (1, MLA_HEADS * MLA_SLAB), 1)
    one_lane = jnp.where(wide_lane % MLA_SLAB == MLA_ONE_LANE, 1.0, 0.0)
    v_ref[...] = (_dot(kvn, wv_ref[...]) + one_lane).astype(BF16)


def _mla_proj(cq, ckv, kpe, seq, w, tabs, tm):
    t = cq.shape[0]
    nseq = seq // tm
    row = lambda i: (i, 0)
    tab = lambda i: (i % nseq, 0)
    wide = MLA_HEADS * MLA_SLAB
    out_shape = tuple(jax.ShapeDtypeStruct((t, wide), BF16) for _ in range(3))
    return pl.pallas_call(
        _mla_proj_kernel,
        grid=(t // tm,),
        in_specs=[
            pl.BlockSpec((tm, MLA_Q_LORA), row), pl.BlockSpec((tm, MLA_KV_LORA), row),
            pl.BlockSpec((tm, MLA_SLAB), row),
            _const_spec((1, MLA_Q_LORA)), _const_spec((1, MLA_KV_LORA)),
            _const_spec(w['wq_mla'].shape), _const_spec(w['wk_mla'].shape), _const_spec(w['wv_mla'].shape),
            pl.BlockSpec((tm, LANES), tab), pl.BlockSpec((tm, LANES), tab),
        ],
        out_specs=[pl.BlockSpec((tm, wide), row) for _ in range(3)],
        out_shape=out_shape,
        compiler_params=_params(("parallel",)),
        name="mla_proj",
    )(cq, ckv, kpe, w['g_qa'], w['g_kva'], w['wq_mla'], w['wk_mla'], w['wv_mla'], tabs['cb'], tabs['sb'])


def _mla_attn_kernel(q_ref, k_ref, v_ref, o_ref, m_ref, acc_ref):
    j = pl.program_id(3)

    @pl.when(j == 0)
    def _():
        m_ref[...] = jnp.full(m_ref.shape, -jnp.inf, F32)
        acc_ref[...] = jnp.zeros(acc_ref.shape, F32)

    tk = k_ref.shape[0]
    for h in range(MLA_HEADS_PER_STEP):
        sl = slice(h * MLA_SLAB, (h + 1) * MLA_SLAB)
        q = q_ref[:, sl]
        m_run = m_ref[:, sl]
        acc = acc_ref[:, sl]
        for c in range(tk // MLA_KEY_CHUNK):
            rows = slice(c * MLA_KEY_CHUNK, (c + 1) * MLA_KEY_CHUNK)
            s = _dot_nt(q, k_ref[rows, sl])
            cols = [s[:, i * LANES:(i + 1) * LANES] for i in range(MLA_KEY_CHUNK // LANES)]
            m_loc = functools.reduce(jnp.maximum, cols)
            m_new = jnp.maximum(m_run, jnp.max(m_loc, axis=1, keepdims=True))
            p = jnp.concatenate([jnp.exp2(col - m_new) for col in cols], axis=1).astype(BF16)
            acc = jnp.exp2(m_run - m_new) * acc + _dot(p, v_ref[rows, sl])
            m_run = m_new
        acc_ref[:, sl] = acc
        m_ref[:, sl] = m_run

    @pl.when(j == pl.num_programs(3) - 1)
    def _():
        for h in range(MLA_HEADS_PER_STEP):
            sl = slice(h * MLA_SLAB, (h + 1) * MLA_SLAB)
            acc = acc_ref[:, sl]
            den = acc[:, MLA_ONE_LANE:MLA_ONE_LANE + 1]
            o_ref[:, sl] = (acc / den).astype(o_ref.dtype)


def _mla_attn(q, k, v, tq, tk):
    b, s, wide = q.shape
    cols = MLA_HEADS_PER_STEP * MLA_SLAB
    return pl.pallas_call(
        _mla_attn_kernel,
        grid=(b, MLA_HEADS // MLA_HEADS_PER_STEP, s // tq, s // tk),
        in_specs=[
            pl.BlockSpec((None, tq, cols), lambda bi, h, i, j: (bi, i, h)),
            pl.BlockSpec((None, tk, cols), lambda bi, h, i, j: (bi, j, h)),
            pl.BlockSpec((None, tk, cols), lambda bi, h, i, j: (bi, j, h)),
        ],
        out_specs=pl.BlockSpec((None, tq, cols), lambda bi, h, i, j: (bi, i, h)),
        out_shape=jax.ShapeDtypeStruct((b, s, wide), BF16),
        scratch_shapes=[pltpu.VMEM((tq, cols), F32), pltpu.VMEM((tq, cols), F32)],
        compiler_params=_params(("parallel", "parallel", "parallel", "arbitrary")),
        name="mla_attn",
    )(q, k, v)


def _dsa_kernel(q_ref, kp_ref, kc_ref, kn_ref, vp_ref, vc_ref, vn_ref, o_ref, lse_ref, *, cls_len):
    j = pl.program_id(2)
    tq = q_ref.shape[0]
    nk = tq + 2 * DSA_HALF
    q = q_ref[...]
    kk = jnp.concatenate([kp_ref[...], kc_ref[...], kn_ref[...]], axis=0)
    vv = jnp.concatenate([vp_ref[...], vc_ref[...], vn_ref[...]], axis=0)
    qpos = j * tq + lax.broadcasted_iota(jnp.int32, (tq, nk), 0)
    kpos = j * tq - DSA_HALF + lax.broadcasted_iota(jnp.int32, (tq, nk), 1)
    ok = (jnp.abs(kpos - qpos) <= DSA_HALF) & (kpos >= 0) & (kpos < cls_len)
    lane = lax.broadcasted_iota(jnp.int32, (tq, DSA_GROUP_COLS), 1)
    heads = range(DSA_HEADS)
    in_head = [(lane // DSA_HEAD_DIM) == h for h in heads]
    s = [jnp.where(ok, _dot_nt(jnp.where(in_head[h], q, jnp.zeros_like(q)), kk), NEG_INF) for h in heads]
    m = [jnp.max(s[h], axis=1, keepdims=True) for h in heads]
    p = [jnp.exp(s[h] - m[h]) for h in heads]
    den = [jnp.sum(p[h], axis=1, keepdims=True) for h in heads]
    oh = [_dot((p[h] / den[h]).astype(BF16), vv) for h in heads]
    o = jnp.zeros((tq, DSA_GROUP_COLS), F32)
    lse = jnp.zeros((tq, DSA_GROUP_COLS), F32)
    for h in heads:
        o = jnp.where(in_head[h], oh[h], o)
        lse = jnp.where(in_head[h], m[h] + jnp.log(den[h]), lse)
    o_ref[...] = o
    lse_ref[...] = lse


def _dsa_group(q, k, v, g, tq):
    b, dil, cls_len, _ = q.shape
    tq = min(tq, cls_len)
    per = tq // DSA_HALF
    last = cls_len // DSA_HALF - 1
    prev = lambda j: jnp.maximum(j * per - 1, 0)
    nxt = lambda j: jnp.minimum((j + 1) * per, last)
    halo = (None, None, DSA_HALF, DSA_GROUP_COLS)
    cur = (None, None, tq, DSA_GROUP_COLS)
    cur_spec = pl.BlockSpec(cur, lambda bi, r, j: (bi, r, j, 0))
    prev_spec = pl.BlockSpec(halo, lambda bi, r, j: (bi, r, prev(j), 0))
    next_spec = pl.BlockSpec(halo, lambda bi, r, j: (bi, r, nxt(j), 0))
    out_sd = jax.ShapeDtypeStruct((b, dil, cls_len, DSA_GROUP_COLS), F32)
    return pl.pallas_call(
        functools.partial(_dsa_kernel, cls_len=cls_len),
        grid=(b, dil, cls_len // tq),
        in_specs=[cur_spec, prev_spec, cur_spec, next_spec, prev_spec, cur_spec, next_spec],
        out_specs=[cur_spec, cur_spec],
        out_shape=(out_sd, out_sd),
        compiler_params=_params(("parallel", "parallel", "parallel")),
        name=f"dsa_g{g}",
    )(q, k, k, k, v, v, v)


def _merge_kernel(o0_ref, l0_ref, o1_ref, l1_ref, o2_ref, l2_ref, yb_ref, gate_ref, x_ref,
                  wpa_ref, wpb_ref, wout_ref, out_ref, join_ref):
    def position_order(ref, slot):
        dil, per_class, _ = ref.shape
        if dil == 1:
            return ref[0]
        halves = []
        for c in range(DSA_GROUP_COLS // LANES):
            for r in range(dil):
                join_ref[slot, c, pl.ds(r, per_class, stride=dil), :] = ref[r, :, c * LANES:(c + 1) * LANES]
            halves.append(join_ref[slot, c])
        return jnp.concatenate(halves, axis=1)

    o0, l0 = position_order(o0_ref, 0), position_order(l0_ref, 0)
    o1, l1 = position_order(o1_ref, 0), position_order(l1_ref, 1)
    o2, l2 = position_order(o2_ref, 2), position_order(l2_ref, 3)
    m = jnp.maximum(jnp.maximum(l0, l1), l2)
    e0, e1, e2 = jnp.exp(l0 - m), jnp.exp(l1 - m), jnp.exp(l2 - m)
    den = e0 + e1 + e2
    ya = (e0 / den) * o0 + (e1 / den) * o1 + (e2 / den) * o2
    gate = gate_ref[...]
    pa = _dot(ya.astype(BF16), wpa_ref[...])
    pb = _dot(yb_ref[...], wpb_ref[...])
    merged = gate[:, :D_MODEL] * pa + gate[:, D_MODEL:] * pb
    out_ref[...] = x_ref[...] + _dot(merged.astype(BF16), wout_ref[...])


def _merge(dsa_out, yb, gates, x2d, seq, w, tm):
    t = x2d.shape[0]
    nseq = seq // tm
    row = lambda i: (i, 0)
    flat, specs = [], []
    for pair in dsa_out:
        for a in pair:
            dil = a.shape[1]
            flat.append(a)
            specs.append(pl.BlockSpec((None, dil, tm // dil, DSA_GROUP_COLS),
                                      lambda i: (i // nseq, 0, i % nseq, 0)))
    return pl.pallas_call(
        _merge_kernel,
        grid=(t // tm,),
        in_specs=specs + [
            pl.BlockSpec((tm, MLA_HEADS * MLA_SLAB), row),
            pl.BlockSpec((tm, 2 * D_MODEL), row),
            pl.BlockSpec((tm, D_MODEL), row),
            _const_spec(w['wpa'].shape), _const_spec(w['wpb'].shape), _const_spec(w['wout'].shape),
        ],
        out_specs=pl.BlockSpec((tm, D_MODEL), row),
        out_shape=jax.ShapeDtypeStruct((t, D_MODEL), F32),
        scratch_shapes=[pltpu.VMEM((4, DSA_GROUP_COLS // LANES, tm, LANES), F32)],
        compiler_params=_params(("parallel",)),
        name="merge",
    )(*flat, yb, gates, x2d, w['wpa'], w['wpb'], w['wout'])


def _norm_mm_kernel(x_ref, g_ref, w_ref, o_ref):
    o_ref[...] = _dot(_rms(x_ref[...], g_ref[...]).astype(BF16), w_ref[...]).astype(o_ref.dtype)


def _norm_mm(x2d, g, wmat, tm):
    t, d = x2d.shape
    n = wmat.shape[1]
    return pl.pallas_call(
        _norm_mm_kernel,
        grid=(t // tm,),
        in_specs=[pl.BlockSpec((tm, d), lambda i: (i, 0)), _const_spec((1, d)), _const_spec(wmat.shape)],
        out_specs=pl.BlockSpec((tm, n), lambda i: (i, 0)),
        out_shape=jax.ShapeDtypeStruct((t, n), BF16),
        compiler_params=_params(("parallel",)),
        name="mem_kv",
    )(x2d, g, wmat)


def _xattn_kernel(x_ref, g_ref, kv_ref, wq_ref, wo_ref, out_ref):
    x = x_ref[...]
    hq = _rms(x, g_ref[...]).astype(BF16)
    q = (_dot(hq, wq_ref[...]) * (MEM_HEAD_DIM ** -0.5)).astype(BF16)
    kv = kv_ref[...]
    outs = []
    for h in range(MEM_HEADS):
        sl = slice(h * MEM_HEAD_DIM, (h + 1) * MEM_HEAD_DIM)
        vsl = slice(D_MODEL + h * MEM_HEAD_DIM, D_MODEL + (h + 1) * MEM_HEAD_DIM)
        s = _dot_nt(q[:, sl], kv[:, sl])
        m = jnp.max(s, axis=1, keepdims=True)
        p = jnp.exp(s - m)
        den = jnp.sum(p, axis=1, keepdims=True)
        outs.append(_dot((p / den).astype(BF16), kv[:, vsl]).astype(BF16))
    o = jnp.concatenate(outs, axis=1)
    out_ref[...] = x + _dot(o, wo_ref[...])


def _xattn(x3d, kv3d, w, tm):
    b, s, d = x3d.shape
    n_mem = kv3d.shape[1]
    return pl.pallas_call(
        _xattn_kernel,
        grid=(b, s // tm),
        in_specs=[
            pl.BlockSpec((None, tm, d), lambda bi, i: (bi, i, 0)),
            _const_spec((1, d)),
            pl.BlockSpec((None, n_mem, 2 * d), lambda bi, i: (bi, 0, 0)),
            _const_spec(w['wmq'].shape), _const_spec(w['wmo'].shape),
        ],
        out_specs=pl.BlockSpec((None, tm, d), lambda bi, i: (bi, i, 0)),
        out_shape=jax.ShapeDtypeStruct((b, s, d), F32),
        compiler_params=_params(("parallel", "parallel")),
        name="xattn",
    )(x3d, w['g_xq'], kv3d, w['wmq'], w['wmo'])


def _top_values(sc, count, with_rank, break_ties):
    n = sc.shape[0]
    row = lax.broadcasted_iota(jnp.int32, sc.shape, 0)
    rank = jnp.full(sc.shape, float(n - 1), F32)
    vals = []
    for k in range(count):
        m = jnp.max(sc, axis=0, keepdims=True)
        hit = sc == m
        if break_ties:
            hit = row == jnp.min(jnp.where(hit, row, n), axis=0, keepdims=True)
        vals.append(m)
        if with_rank:
            rank = jnp.where(hit, float(k), rank)
        sc = jnp.where(hit, -jnp.inf, sc)
    dropped = jnp.sum(jnp.where(sc == -jnp.inf, 1.0, 0.0), axis=0, keepdims=True)
    return jnp.concatenate(vals, axis=0), rank, dropped


def _selected_per_first_key(s1, v1, v2, tau):
    need = []
    for j in range(PEER_TOPK):
        reach = (v1 + v2[j:j + 1]) >= tau
        need.append(jnp.min(jnp.where(reach, v1, jnp.inf), axis=0, keepdims=True))
    c8 = s1 >= need[7]
    c4 = s1 >= jnp.where(c8, need[11], need[3])
    c2 = s1 >= jnp.where(c8, jnp.where(c4, need[13], need[9]), jnp.where(c4, need[5], need[1]))
    low = jnp.where(c4, jnp.where(c2, need[6], need[4]), jnp.where(c2, need[2], need[0]))
    high = jnp.where(c4, jnp.where(c2, need[14], need[12]), jnp.where(c2, need[10], need[8]))
    c1 = s1 >= jnp.where(c8, high, low)
    c16 = s1 >= need[15]
    return (jnp.where(c8, 8.0, 0.0) + jnp.where(c4, 4.0, 0.0) + jnp.where(c2, 2.0, 0.0)
            + jnp.where(c1, 1.0, 0.0) + jnp.where(c16, 1.0, 0.0))


def _pack_bf16_rows(x):
    return pltpu.bitcast(x.astype(BF16), jnp.int32)


def _unpack_bf16_rows(words):
    return pltpu.bitcast(words, BF16)


def _bf16_pair_word(x):
    bits = lax.bitcast_convert_type(x.astype(BF16).astype(F32), jnp.uint32)
    return lax.bitcast_convert_type(bits | (bits >> 16), jnp.int32)


def _peer_score_kernel(x_ref, g_ref, wpq_ref, keys_ref, hqt_ref, r2s_ref, cnts_ref, e1s_ref, e2s_ref):
    hq32 = _rms(x_ref[...], g_ref[...])

    @pl.when(pl.program_id(1) == 0)
    def _():
        hqt_ref[...] = _pack_bf16_rows(hq32.T)

    q = _dot(hq32.astype(BF16), wpq_ref[...]).astype(BF16)
    heads = keys_ref.shape[0]
    halves = [[_dot_nt(keys_ref[h, p], q[:, (2 * h + p) * PEER_HALF:(2 * h + p + 1) * PEER_HALF])
               for p in range(2)] for h in range(heads)]

    def statistics(break_ties):
        worst = [head_statistics(h, break_ties) for h in range(heads)]
        return functools.reduce(jnp.maximum, worst)

    def head_statistics(h, break_ties):
        s1, s2 = halves[h]
        r2_ref, cnt_ref, e1_ref, e2_ref = (ref.at[h] for ref in (r2s_ref, cnts_ref, e1s_ref, e2s_ref))
        v1, _, d1 = _top_values(s1, PEER_TOPK, False, break_ties)
        v2, r2, d2 = _top_values(s2, PEER_TOPK, True, break_ties)
        cand = [v1[0:1] + v2]
        cand += [v1[i:i + 1] + v2[0:8] for i in range(1, 8)]
        cand += [v1[8:16] + v2[0:1]]
        top, _, d3 = _top_values(jnp.concatenate(cand, axis=0), PEER_TOPK, False, break_ties)
        z = jnp.sum(jnp.exp(top - top[0:1]), axis=0, keepdims=True)
        tau = top[PEER_TOPK - 1:PEER_TOPK]
        cnt = _selected_per_first_key(s1, v1, v2, tau)
        r2_ref[...] = _pack_bf16_rows(r2)
        cnt_ref[...] = _bf16_pair_word(cnt)
        e1_ref[...] = _bf16_pair_word(jnp.exp(s1 - v1[0:1]) * (0.5 / z))
        e2_ref[...] = _pack_bf16_rows(jnp.exp(s2 - v2[0:1]))
        return jnp.maximum(jnp.maximum(d1, d2), d3)

    most_dropped = statistics(break_ties=False)

    @pl.when(jnp.max(most_dropped) > PEER_TOPK)
    def _():
        statistics(break_ties=True)


def _peer_scores(x2d, w):
    t = x2d.shape[0]
    tt = PEER_TOKENS
    groups = t // tt
    words = lambda rows: jax.ShapeDtypeStruct((groups, PEER_HEADS, rows, tt), jnp.int32)
    hs = PEER_SCORE_HEADS
    spec = lambda rows: pl.BlockSpec((None, hs, rows, tt), lambda i, h: (i, h, 0, 0))
    half = PEER_NKEYS // 2
    return pl.pallas_call(
        _peer_score_kernel,
        grid=(groups, PEER_HEADS // hs),
        in_specs=[
            pl.BlockSpec((tt, D_MODEL), lambda i, h: (i, 0)),
            _const_spec((1, D_MODEL)),
            pl.BlockSpec((D_MODEL, hs * 2 * PEER_HALF), lambda i, h: (0, h)),
            pl.BlockSpec((hs, 2, PEER_NKEYS, PEER_HALF), lambda i, h: (h, 0, 0, 0)),
        ],
        out_specs=[pl.BlockSpec((None, D_MODEL // 2, tt), lambda i, h: (i, 0, 0)),
                   spec(half), spec(PEER_NKEYS), spec(PEER_NKEYS), spec(half)],
        out_shape=(jax.ShapeDtypeStruct((groups, D_MODEL // 2, tt), jnp.int32),
                   words(half), words(PEER_NKEYS), words(PEER_NKEYS), words(half)),
        compiler_params=_params(("parallel", "arbitrary")),
        name="peer_scores",
    )(x2d, w['g_ffn'], w['wpq'], w['keys'])


def _peer_mix_kernel(hqt_ref, u_ref, vt_ref, r2_ref, cnt_ref, e1_ref, e2_ref, x_ref, gf_ref,
                     out_ref, acc_ref, wg_ref, *, blocks):
    j = pl.program_id(1)
    last = pl.num_programs(1) - 1
    groups, _, tt = hqt_ref.shape
    half_rows = PEER_NKEYS // 2

    @pl.when(j == 0)
    def _():
        acc_ref[...] = jnp.zeros(acc_ref.shape, F32)
        wg_ref[...] = jnp.zeros(wg_ref.shape, jnp.int32)

    def mix_previous_block(g):
        acc_ref[g] += _dot(_unpack_bf16_rows(vt_ref[...]), _unpack_bf16_rows(wg_ref[g]))

    def row_bcast(words):
        return _unpack_bf16_rows(jnp.broadcast_to(words, (half_rows, LANES)))

    def activate_and_mix(g, carry):
        mix_previous_block(g)
        hqt = _unpack_bf16_rows(hqt_ref[g])
        for c in range(blocks // 2):
            u_pair = _unpack_bf16_rows(u_ref[c * PEER_NKEYS:(c + 1) * PEER_NKEYS, :])
            pre2 = _dot(u_pair, hqt)
            for half in range(2):
                i = 2 * c + half
                a = j * blocks + i
                cnt_rows = [cnt_ref[g, h, pl.ds(a, 1), :] for h in range(PEER_HEADS)]
                e1_rows = [e1_ref[g, h, pl.ds(a, 1), :] for h in range(PEER_HEADS)]
                for col in range(tt // LANES):
                    cs = slice(col * LANES, (col + 1) * LANES)
                    wsum = jnp.zeros((PEER_NKEYS, LANES), BF16)
                    for h in range(PEER_HEADS):
                        chosen = _unpack_bf16_rows(r2_ref[g, h, :, cs]) < row_bcast(cnt_rows[h][:, cs])
                        picked = jnp.where(chosen, _unpack_bf16_rows(e2_ref[g, h, :, cs]), jnp.zeros((), BF16))
                        wsum = wsum + picked * row_bcast(e1_rows[h][:, cs])
                    pre = pre2[half * PEER_NKEYS:(half + 1) * PEER_NKEYS, cs]
                    act = pre * (1.0 + lax.erf(pre * (0.5 ** 0.5)))
                    wg_ref[g, i * half_rows:(i + 1) * half_rows, cs] = pltpu.bitcast(
                        wsum * act.astype(BF16), jnp.int32)
        return carry

    @pl.when(j < last)
    def _():
        for g in range(groups):
            activate_and_mix(g, 0)

    @pl.when(j == last)
    def _():
        for g in range(groups):
            mix_previous_block(g)
            rows = slice(g * tt, (g + 1) * tt)
            out_ref[rows, :] = _rms(x_ref[rows, :] + acc_ref[g].T, gf_ref[...])


def _peer_mix(hqt, stats, x2d, w, groups, blocks):
    t = x2d.shape[0]
    tt = PEER_TOKENS
    rows = groups * tt
    ne = blocks * PEER_NKEYS
    nblk = PEER_NKEYS // blocks
    spec = lambda r: pl.BlockSpec((groups, PEER_HEADS, r, tt), lambda i, j: (i, 0, 0, 0))
    half = PEER_NKEYS // 2
    return pl.pallas_call(
        functools.partial(_peer_mix_kernel, blocks=blocks),
        grid=(t // rows, nblk + 1),
        in_specs=[
            pl.BlockSpec((groups, D_MODEL // 2, tt), lambda i, j: (i, 0, 0)),
            pl.BlockSpec((ne // 2, D_MODEL), lambda i, j: (jnp.minimum(j, nblk - 1), 0)),
            pl.BlockSpec((D_MODEL // 2, ne), lambda i, j: (0, jnp.maximum(j - 1, 0))),
            spec(half), spec(PEER_NKEYS), spec(PEER_NKEYS), spec(half),
            pl.BlockSpec((rows, D_MODEL), lambda i, j: (i, 0)),
            _const_spec((1, D_MODEL)),
        ],
        out_specs=pl.BlockSpec((rows, D_MODEL), lambda i, j: (i, 0)),
        out_shape=jax.ShapeDtypeStruct((t, D_MODEL), F32),
        scratch_shapes=[pltpu.VMEM((groups, D_MODEL, tt), F32), pltpu.VMEM((groups, ne // 2, tt), jnp.int32)],
        compiler_params=_params(("parallel", "arbitrary")),
        name="peer_mix",
    )(hqt, w['peer_u'], w['peer_vt'], *stats, x2d, w['g_final'])


def _pack_rows_host(a):
    m2, n = a.shape
    return lax.bitcast_convert_type(jnp.swapaxes(a.reshape(m2 // 2, 2, n), 1, 2), jnp.int32)


def _prep_weights(g_mix, w_in, g_qa, g_kva, w_uq, w_ukv, w_gate, b_gate, w_pa, w_pb, w_out,
                  g_xq, g_mkv, w_mq, w_mkv, w_mo, g_ffn, w_pq, sub_keys, peer_u, peer_v, g_final):
    w_in = w_in[0]
    q_cols = DSA_QK_COLS // 2
    wqk = jnp.concatenate([w_in[:, :q_cols] * (DSA_HEAD_DIM ** -0.5), w_in[:, q_cols:DSA_QK_COLS]], axis=1)
    c0 = DSA_QK_COLS + DSA_V_COLS
    c1 = c0 + MLA_Q_LORA + MLA_KV_LORA
    wkpe = jnp.zeros((D_MODEL, MLA_SLAB), F32).at[:, MLA_NOPE:MLA_QK].set(w_in[:, c1:])
    pad_head = lambda a: jnp.pad(a, ((0, 0), (0, 0), (0, MLA_SLAB - a.shape[2])))
    wq = pad_head(w_uq[0].reshape(MLA_Q_LORA, MLA_HEADS, MLA_QK)).reshape(MLA_Q_LORA, -1)
    ukv = w_ukv[0].reshape(MLA_KV_LORA, MLA_HEADS, MLA_NOPE + MLA_V)
    wk = pad_head(ukv[:, :, :MLA_NOPE]).reshape(MLA_KV_LORA, -1)
    wv = pad_head(ukv[:, :, MLA_NOPE:]).reshape(MLA_KV_LORA, -1)
    wpb = jnp.pad(w_pb[0].reshape(MLA_HEADS, MLA_V, D_MODEL), ((0, 0), (0, MLA_SLAB - MLA_V), (0, 0)))
    bf = lambda a: a.astype(BF16)
    return {
        'g_mix': g_mix, 'wqk': bf(wqk), 'wv': bf(w_in[:, DSA_QK_COLS:c0]), 'wc': bf(w_in[:, c0:c1]),
        'wkpe': bf(wkpe), 'wgate': bf(w_gate[0]), 'b_gate': b_gate,
        'g_qa': g_qa, 'g_kva': g_kva, 'wq_mla': bf(wq), 'wk_mla': bf(wk), 'wv_mla': bf(wv),
        'wpa': bf(w_pa[0]), 'wpb': bf(wpb.reshape(-1, D_MODEL)), 'wout': bf(w_out[0]),
        'g_xq': g_xq, 'g_mkv': g_mkv, 'wmq': bf(w_mq[0]), 'wmkv': bf(w_mkv[0]), 'wmo': bf(w_mo[0]),
        'g_ffn': g_ffn, 'wpq': bf(w_pq[0]),
        'keys': bf(sub_keys[0]),
        'peer_u': _pack_rows_host(bf(peer_u[0])), 'peer_vt': _pack_rows_host(bf(peer_v[0]).T),
        'g_final': g_final.reshape(1, D_MODEL),
    }


def _rope_tables(seq):
    def table(rot, lead, period):
        inv = ROPE_THETA ** (-jnp.arange(0, rot, 2, dtype=F32) / rot)
        ang = jnp.arange(seq, dtype=F32)[:, None] * inv[None, :]
        cos, sin = jnp.cos(ang), jnp.sin(ang)
        ones = lambda n: jnp.ones((seq, n), F32)
        zeros = lambda n: jnp.zeros((seq, n), F32)
        tail = period - lead - rot
        c = jnp.concatenate([ones(lead), cos, cos, ones(tail)], axis=1)
        s = jnp.concatenate([zeros(lead), -sin, sin, zeros(tail)], axis=1)
        reps = LANES // period
        return jnp.tile(c, (1, reps)), jnp.tile(s, (1, reps))
    ca, sa = table(DSA_ROT, 0, DSA_HEAD_DIM)
    cb, sb = table(MLA_ROPE, MLA_NOPE, MLA_SLAB)
    return {'ca': ca, 'sa': sa, 'cb': cb, 'sb': sb}


def _trunk(x, mem, w):
    b, s, d = x.shape
    t = b * s
    tabs = _rope_tables(s)
    x2d = x.reshape(t, d)
    tm = ROW_TILE
    *dsa_in, cq, ckv, kpe, gates = _inproj(x2d, b, s, w, tabs, tm)
    dsa_out = [_dsa_group(*dsa_in[3 * g:3 * g + 3], g, DSA_QUERY_TILE) for g in range(len(DSA_DILATIONS))]
    q_b, k_b, v_b = _mla_proj(cq, ckv, kpe, s, w, tabs, tm)
    wide = MLA_HEADS * MLA_SLAB
    yb = _mla_attn(q_b.reshape(b, s, wide), k_b.reshape(b, s, wide), v_b.reshape(b, s, wide),
                   min(MLA_QUERY_TILE, s), min(MLA_KEY_TILE, s))
    x1 = _merge(dsa_out, yb.reshape(t, wide), gates, x2d, s, w, tm)
    n_mem = mem.shape[1]
    kv = _norm_mm(mem.reshape(b * n_mem, d), w['g_mkv'], w['wmkv'], n_mem)
    x2 = _xattn(x1.reshape(b, s, d), kv.reshape(b, n_mem, 2 * d), w, tm)
    x2 = x2.reshape(t, d)
    hqt, *stats = _peer_scores(x2, w)
    y = _peer_mix(hqt, stats, x2, w, PEER_MIX_GROUPS, PEER_MIX_BLOCKS)
    return y.reshape(b, s, d)


def kernel(x_prompt, x_sample, mem_prompt, mem_sample, g_mix, w_in, g_qa, g_kva, w_uq, w_ukv, w_gate,
           b_gate, w_pa, w_pb, w_out, g_xq, g_mkv, w_mq, w_mkv, w_mo, g_ffn, w_pq, sub_keys, peer_u,
           peer_v, g_final):
    w = _prep_weights(g_mix, w_in, g_qa, g_kva, w_uq, w_ukv, w_gate, b_gate, w_pa, w_pb, w_out,
                      g_xq, g_mkv, w_mq, w_mkv, w_mo, g_ffn, w_pq, sub_keys, peer_u, peer_v, g_final)
    return (_trunk(x_prompt, mem_prompt, w), _trunk(x_sample, mem_sample, w))
```

```python
import functools

import jax
import jax.numpy as jnp
from jax import lax
from jax.experimental import pallas as pl
from jax.experimental.pallas import tpu as pltpu

F32 = jnp.float32
BF16 = jnp.bfloat16

D_MODEL = 1024
EPS = 1e-6
NEG_INF = -1e30
ROPE_THETA = 500000.0

DSA_DILATIONS = (1, 4, 16)
DSA_HALF = 64
DSA_HEADS = 4
DSA_HEAD_DIM = 64
DSA_ROT = 16
DSA_GROUP_COLS = DSA_HEADS * DSA_HEAD_DIM
DSA_QK_COLS = 2 * 3 * DSA_GROUP_COLS
DSA_V_COLS = 3 * DSA_GROUP_COLS

MLA_HEADS = 8
MLA_Q_LORA = 768
MLA_KV_LORA = 256
MLA_NOPE = 64
MLA_ROPE = 32
MLA_V = 64
MLA_QK = MLA_NOPE + MLA_ROPE
MLA_SLAB = 128
MLA_ONE_LANE = MLA_V
MLA_HEADS_PER_STEP = 1
MLA_KEY_CHUNK = 512
LOG2_E = 1.4426950408889634

MEM_HEADS = 4
MEM_HEAD_DIM = 256

PEER_HEADS = 8
PEER_NKEYS = 128
PEER_HALF = 128
PEER_TOPK = 16
PEER_TOKENS = 256
PEER_SCORE_HEADS = 4

LANES = 128
VMEM_LIMIT_BYTES = 56 * 1024 * 1024

ROW_TILE = 512
DSA_QUERY_TILE = 256
MLA_QUERY_TILE = 4096
MLA_KEY_TILE = 2048
PEER_MIX_GROUPS = 2
PEER_MIX_BLOCKS = 16


def _params(sem):
    return pltpu.CompilerParams(dimension_semantics=sem, vmem_limit_bytes=VMEM_LIMIT_BYTES)


def _const_spec(shape):
    nd = len(shape)
    return pl.BlockSpec(shape, lambda *_: (0,) * nd, pipeline_mode=pl.Buffered(1))


def _rms(x, g):
    y = x * lax.rsqrt(jnp.mean(x * x, axis=-1, keepdims=True) + EPS)
    return y * g


def _dot(a, b):
    return jnp.dot(a, b, preferred_element_type=F32)


def _dot_nt(a, b):
    return lax.dot_general(a, b, (((1,), (1,)), ((), ())), preferred_element_type=F32)


def _rope_slab(x, cos_t, sin_t, take_up, half):
    up = pltpu.roll(x, LANES - half, 1)
    dn = pltpu.roll(x, half, 1)
    return x * cos_t + jnp.where(take_up, up, dn) * sin_t


def _inproj_kernel(x_ref, g_ref, wqk_ref, wv_ref, wc_ref, wkpe_ref, wg_ref, bg_ref,
                   ca_ref, sa_ref, cb_ref, sb_ref, *rest):
    dsa_refs = rest[:9]
    cq_ref, ckv_ref, kpe_ref, gate_ref, split_ref = rest[9:]
    h = _rms(x_ref[...], g_ref[...]).astype(BF16)
    rows = h.shape[0]
    lane = lax.broadcasted_iota(jnp.int32, (rows, LANES), 1)
    up_a = (lane % DSA_HEAD_DIM) < (DSA_ROT // 2)
    ca = ca_ref[...]
    sa = sa_ref[...]
    zqk = _dot(h, wqk_ref[...])
    zv = _dot(h, wv_ref[...])
    slot = 0
    for g, dil in enumerate(DSA_DILATIONS):
        for part in range(3):
            if part < 2:
                first = (part * DSA_QK_COLS // 2 + g * DSA_GROUP_COLS) // LANES
                val = jnp.concatenate(
                    [_rope_slab(zqk[:, s * LANES:(s + 1) * LANES], ca, sa, up_a, DSA_ROT // 2)
                     for s in range(first, first + DSA_GROUP_COLS // LANES)], axis=1)
            else:
                val = zv[:, g * DSA_GROUP_COLS:(g + 1) * DSA_GROUP_COLS]
            out_ref = dsa_refs[3 * g + part]
            if dil == 1:
                out_ref[0] = val.astype(BF16)
            else:
                for c in range(DSA_GROUP_COLS // LANES):
                    cs = slice(c * LANES, (c + 1) * LANES)
                    split_ref[slot, c] = val[:, cs]
                    for r in range(dil):
                        out_ref[r, :, cs] = split_ref[slot, c, pl.ds(r, rows // dil, stride=dil), :].astype(BF16)
                slot += 1
    c = _dot(h, wc_ref[...])
    cq_ref[...] = c[:, :MLA_Q_LORA]
    ckv_ref[...] = c[:, MLA_Q_LORA:]
    up_b = (lane >= MLA_NOPE) & (lane < MLA_NOPE + MLA_ROPE // 2)
    kpe_ref[...] = _rope_slab(_dot(h, wkpe_ref[...]), cb_ref[...], sb_ref[...], up_b, MLA_ROPE // 2)
    gate_ref[...] = jax.nn.sigmoid(_dot(h, wg_ref[...]) + bg_ref[...])


def _inproj(x2d, batch, seq, w, tabs, tm):
    t = x2d.shape[0]
    nseq = seq // tm
    row = lambda i: (i, 0)
    tab = lambda i: (i % nseq, 0)
    dsa_shapes, dsa_specs = [], []
    for dil in DSA_DILATIONS:
        for _ in range(3):
            dsa_shapes.append(jax.ShapeDtypeStruct((batch, dil, seq // dil, DSA_GROUP_COLS), BF16))
            dsa_specs.append(pl.BlockSpec((None, dil, tm // dil, DSA_GROUP_COLS),
                                          lambda i: (i // nseq, 0, i % nseq, 0)))
    flat_shapes = (
        jax.ShapeDtypeStruct((t, MLA_Q_LORA), F32),
        jax.ShapeDtypeStruct((t, MLA_KV_LORA), F32),
        jax.ShapeDtypeStruct((t, MLA_SLAB), F32),
        jax.ShapeDtypeStruct((t, 2 * D_MODEL), F32),
    )
    split_slots = 3 * sum(1 for dil in DSA_DILATIONS if dil > 1)
    return pl.pallas_call(
        _inproj_kernel,
        grid=(t // tm,),
        in_specs=[
            pl.BlockSpec((tm, D_MODEL), row),
            _const_spec((1, D_MODEL)),
            _const_spec(w['wqk'].shape), _const_spec(w['wv'].shape), _const_spec(w['wc'].shape),
            _const_spec(w['wkpe'].shape), _const_spec(w['wgate'].shape), _const_spec((1, 2 * D_MODEL)),
            pl.BlockSpec((tm, LANES), tab), pl.BlockSpec((tm, LANES), tab),
            pl.BlockSpec((tm, LANES), tab), pl.BlockSpec((tm, LANES), tab),
        ],
        out_specs=dsa_specs + [pl.BlockSpec((tm, s.shape[1]), row) for s in flat_shapes],
        out_shape=tuple(dsa_shapes) + flat_shapes,
        scratch_shapes=[pltpu.VMEM((split_slots, DSA_GROUP_COLS // LANES, tm, LANES), F32)],
        compiler_params=_params(("parallel",)),
        name="inproj",
    )(x2d, w['g_mix'], w['wqk'], w['wv'], w['wc'], w['wkpe'], w['wgate'], w['b_gate'],
      tabs['ca'], tabs['sa'], tabs['cb'], tabs['sb'])


def _mla_proj_kernel(cq_ref, ckv_ref, kpe_ref, gqa_ref, gkva_ref, wq_ref, wk_ref, wv_ref,
                     cb_ref, sb_ref, q_ref, k_ref, v_ref):
    qn = _rms(cq_ref[...], gqa_ref[...]).astype(BF16)
    rows = qn.shape[0]
    lane = lax.broadcasted_iota(jnp.int32, (rows, LANES), 1)
    up_b = (lane >= MLA_NOPE) & (lane < MLA_NOPE + MLA_ROPE // 2)
    cb = cb_ref[...]
    sb = sb_ref[...]
    q = _dot(qn, wq_ref[...])
    kvn = _rms(ckv_ref[...], gkva_ref[...]).astype(BF16)
    k = _dot(kvn, wk_ref[...])
    kpe = kpe_ref[...]
    scale = MLA_QK ** -0.5 * LOG2_E
    for s in range(MLA_HEADS):
        sl = slice(s * LANES, (s + 1) * LANES)
        q_ref[:, sl] = (_rope_slab(q[:, sl], cb, sb, up_b, MLA_ROPE // 2) * scale).astype(BF16)
        k_ref[:, sl] = (k[:, sl] + kpe).astype(BF16)
    wide_lane = lax.broadcasted_iota(jnp.int32, (1, MLA_HEADS * MLA_SLAB), 1)
    one_lane = jnp.where(wide_lane % MLA_SLAB == MLA_ONE_LANE, 1.0, 0.0)
    v_ref[...] = (_dot(kvn, wv_ref[...]) + one_lane).astype(BF16)


def _mla_proj(cq, ckv, kpe, seq, w, tabs, tm):
    t = cq.shape[0]
    nseq = seq // tm
    row = lambda i: (i, 0)
    tab = lambda i: (i % nseq, 0)
    wide = MLA_HEADS * MLA_SLAB
    out_shape = tuple(jax.ShapeDtypeStruct((t, wide), BF16) for _ in range(3))
    return pl.pallas_call(
        _mla_proj_kernel,
        grid=(t // tm,),
        in_specs=[
            pl.BlockSpec((tm, MLA_Q_LORA), row), pl.BlockSpec((tm, MLA_KV_LORA), row),
            pl.BlockSpec((tm, MLA_SLAB), row),
            _const_spec((1, MLA_Q_LORA)), _const_spec((1, MLA_KV_LORA)),
            _const_spec(w['wq_mla'].shape), _const_spec(w['wk_mla'].shape), _const_spec(w['wv_mla'].shape),
            pl.BlockSpec((tm, LANES), tab), pl.BlockSpec((tm, LANES), tab),
        ],
        out_specs=[pl.BlockSpec((tm, wide), row) for _ in range(3)],
        out_shape=out_shape,
        compiler_params=_params(("parallel",)),
        name="mla_proj",
    )(cq, ckv, kpe, w['g_qa'], w['g_kva'], w['wq_mla'], w['wk_mla'], w['wv_mla'], tabs['cb'], tabs['sb'])


def _mla_attn_kernel(q_ref, k_ref, v_ref, o_ref, m_ref, acc_ref):
    j = pl.program_id(3)

    @pl.when(j == 0)
    def _():
        m_ref[...] = jnp.full(m_ref.shape, -jnp.inf, F32)
        acc_ref[...] = jnp.zeros(acc_ref.shape, F32)

    tk = k_ref.shape[0]
    for h in range(MLA_HEADS_PER_STEP):
        sl = slice(h * MLA_SLAB, (h + 1) * MLA_SLAB)
        q = q_ref[:, sl]
        m_run = m_ref[:, sl]
        acc = acc_ref[:, sl]
        for c in range(tk // MLA_KEY_CHUNK):
            rows = slice(c * MLA_KEY_CHUNK, (c + 1) * MLA_KEY_CHUNK)
            s = _dot_nt(q, k_ref[rows, sl])
            cols = [s[:, i * LANES:(i + 1) * LANES] for i in range(MLA_KEY_CHUNK // LANES)]
            m_loc = functools.reduce(jnp.maximum, cols)
            m_new = jnp.maximum(m_run, jnp.max(m_loc, axis=1, keepdims=True))
            p = jnp.concatenate([jnp.exp2(col - m_new) for col in cols], axis=1).astype(BF16)
            acc = jnp.exp2(m_run - m_new) * acc + _dot(p, v_ref[rows, sl])
            m_run = m_new
        acc_ref[:, sl] = acc
        m_ref[:, sl] = m_run

    @pl.when(j == pl.num_programs(3) - 1)
    def _():
        for h in range(MLA_HEADS_PER_STEP):
            sl = slice(h * MLA_SLAB, (h + 1) * MLA_SLAB)
            acc = acc_ref[:, sl]
            den = acc[:, MLA_ONE_LANE:MLA_ONE_LANE + 1]
            o_ref[:, sl] = (acc / den).astype(o_ref.dtype)


def _mla_attn(q, k, v, tq, tk):
    b, s, wide = q.shape
    cols = MLA_HEADS_PER_STEP * MLA_SLAB
    return pl.pallas_call(
        _mla_attn_kernel,
        grid=(b, MLA_HEADS // MLA_HEADS_PER_STEP, s // tq, s // tk),
        in_specs=[
            pl.BlockSpec((None, tq, cols), lambda bi, h, i, j: (bi, i, h)),
            pl.BlockSpec((None, tk, cols), lambda bi, h, i, j: (bi, j, h)),
            pl.BlockSpec((None, tk, cols), lambda bi, h, i, j: (bi, j, h)),
        ],
        out_specs=pl.BlockSpec((None, tq, cols), lambda bi, h, i, j: (bi, i, h)),
        out_shape=jax.ShapeDtypeStruct((b, s, wide), BF16),
        scratch_shapes=[pltpu.VMEM((tq, cols), F32), pltpu.VMEM((tq, cols), F32)],
        compiler_params=_params(("parallel", "parallel", "parallel", "arbitrary")),
        name="mla_attn",
    )(q, k, v)


def _dsa_kernel(q_ref, kp_ref, kc_ref, kn_ref, vp_ref, vc_ref, vn_ref, o_ref, lse_ref, *, cls_len):
    j = pl.program_id(2)
    tq = q_ref.shape[0]
    nk = tq + 2 * DSA_HALF
    q = q_ref[...]
    kk = jnp.concatenate([kp_ref[...], kc_ref[...], kn_ref[...]], axis=0)
    vv = jnp.concatenate([vp_ref[...], vc_ref[...], vn_ref[...]], axis=0)
    qpos = j * tq + lax.broadcasted_iota(jnp.int32, (tq, nk), 0)
    kpos = j * tq - DSA_HALF + lax.broadcasted_iota(jnp.int32, (tq, nk), 1)
    ok = (jnp.abs(kpos - qpos) <= DSA_HALF) & (kpos >= 0) & (kpos < cls_len)
    lane = lax.broadcasted_iota(jnp.int32, (tq, DSA_GROUP_COLS), 1)
    heads = range(DSA_HEADS)
    in_head = [(lane // DSA_HEAD_DIM) == h for h in heads]
    s = [jnp.where(ok, _dot_nt(jnp.where(in_head[h], q, jnp.zeros_like(q)), kk), NEG_INF) for h in heads]
    m = [jnp.max(s[h], axis=1, keepdims=True) for h in heads]
    p = [jnp.exp(s[h] - m[h]) for h in heads]
    den = [jnp.sum(p[h], axis=1, keepdims=True) for h in heads]
    oh = [_dot((p[h] / den[h]).astype(BF16), vv) for h in heads]
    o = jnp.zeros((tq, DSA_GROUP_COLS), F32)
    lse = jnp.zeros((tq, DSA_GROUP_COLS), F32)
    for h in heads:
        o = jnp.where(in_head[h], oh[h], o)
        lse = jnp.where(in_head[h], m[h] + jnp.log(den[h]), lse)
    o_ref[...] = o
    lse_ref[...] = lse


def _dsa_group(q, k, v, g, tq):
    b, dil, cls_len, _ = q.shape
    tq = min(tq, cls_len)
    per = tq // DSA_HALF
    last = cls_len // DSA_HALF - 1
    prev = lambda j: jnp.maximum(j * per - 1, 0)
    nxt = lambda j: jnp.minimum((j + 1) * per, last)
    halo = (None, None, DSA_HALF, DSA_GROUP_COLS)
    cur = (None, None, tq, DSA_GROUP_COLS)
    cur_spec = pl.BlockSpec(cur, lambda bi, r, j: (bi, r, j, 0))
    prev_spec = pl.BlockSpec(halo, lambda bi, r, j: (bi, r, prev(j), 0))
    next_spec = pl.BlockSpec(halo, lambda bi, r, j: (bi, r, nxt(j), 0))
    out_sd = jax.ShapeDtypeStruct((b, dil, cls_len, DSA_GROUP_COLS), F32)
    return pl.pallas_call(
        functools.partial(_dsa_kernel, cls_len=cls_len),
        grid=(b, dil, cls_len // tq),
        in_specs=[cur_spec, prev_spec, cur_spec, next_spec, prev_spec, cur_spec, next_spec],
        out_specs=[cur_spec, cur_spec],
        out_shape=(out_sd, out_sd),
        compiler_params=_params(("parallel", "parallel", "parallel")),
        name=f"dsa_g{g}",
    )(q, k, k, k, v, v, v)


def _merge_kernel(o0_ref, l0_ref, o1_ref, l1_ref, o2_ref, l2_ref, yb_ref, gate_ref, x_ref,
                  wpa_ref, wpb_ref, wout_ref, out_ref, join_ref):
    def position_order(ref, slot):
        dil, per_class, _ = ref.shape
        if dil == 1:
            return ref[0]
        halves = []
        for c in range(DSA_GROUP_COLS // LANES):
            for r in range(dil):
                join_ref[slot, c, pl.ds(r, per_class, stride=dil), :] = ref[r, :, c * LANES:(c + 1) * LANES]
            halves.append(join_ref[slot, c])
        return jnp.concatenate(halves, axis=1)

    o0, l0 = position_order(o0_ref, 0), position_order(l0_ref, 0)
    o1, l1 = position_order(o1_ref, 0), position_order(l1_ref, 1)
    o2, l2 = position_order(o2_ref, 2), position_order(l2_ref, 3)
    m = jnp.maximum(jnp.maximum(l0, l1), l2)
    e0, e1, e2 = jnp.exp(l0 - m), jnp.exp(l1 - m), jnp.exp(l2 - m)
    den = e0 + e1 + e2
    ya = (e0 / den) * o0 + (e1 / den) * o1 + (e2 / den) * o2
    gate = gate_ref[...]
    pa = _dot(ya.astype(BF16), wpa_ref[...])
    pb = _dot(yb_ref[...], wpb_ref[...])
    merged = gate[:, :D_MODEL] * pa + gate[:, D_MODEL:] * pb
    out_ref[...] = x_ref[...] + _dot(merged.astype(BF16), wout_ref[...])


def _merge(dsa_out, yb, gates, x2d, seq, w, tm):
    t = x2d.shape[0]
    nseq = seq // tm
    row = lambda i: (i, 0)
    flat, specs = [], []
    for pair in dsa_out:
        for a in pair:
            dil = a.shape[1]
            flat.append(a)
            specs.append(pl.BlockSpec((None, dil, tm // dil, DSA_GROUP_COLS),
                                      lambda i: (i // nseq, 0, i % nseq, 0)))
    return pl.pallas_call(
        _merge_kernel,
        grid=(t // tm,),
        in_specs=specs + [
            pl.BlockSpec((tm, MLA_HEADS * MLA_SLAB), row),
            pl.BlockSpec((tm, 2 * D_MODEL), row),
            pl.BlockSpec((tm, D_MODEL), row),
            _const_spec(w['wpa'].shape), _const_spec(w['wpb'].shape), _const_spec(w['wout'].shape),
        ],
        out_specs=pl.BlockSpec((tm, D_MODEL), row),
        out_shape=jax.ShapeDtypeStruct((t, D_MODEL), F32),
        scratch_shapes=[pltpu.VMEM((4, DSA_GROUP_COLS // LANES, tm, LANES), F32)],
        compiler_params=_params(("parallel",)),
        name="merge",
    )(*flat, yb, gates, x2d, w['wpa'], w['wpb'], w['wout'])


def _norm_mm_kernel(x_ref, g_ref, w_ref, o_ref):
    o_ref[...] = _dot(_rms(x_ref[...], g_ref[...]).astype(BF16), w_ref[...]).astype(o_ref.dtype)


def _norm_mm(x2d, g, wmat, tm):
    t, d = x2d.shape
    n = wmat.shape[1]
    return pl.pallas_call(
        _norm_mm_kernel,
        grid=(t // tm,),
        in_specs=[pl.BlockSpec((tm, d), lambda i: (i, 0)), _const_spec((1, d)), _const_spec(wmat.shape)],
        out_specs=pl.BlockSpec((tm, n), lambda i: (i, 0)),
        out_shape=jax.ShapeDtypeStruct((t, n), BF16),
        compiler_params=_params(("parallel",)),
        name="mem_kv",
    )(x2d, g, wmat)


def _xattn_kernel(x_ref, g_ref, kv_ref, wq_ref, wo_ref, out_ref):
    x = x_ref[...]
    hq = _rms(x, g_ref[...]).astype(BF16)
    q = (_dot(hq, wq_ref[...]) * (MEM_HEAD_DIM ** -0.5)).astype(BF16)
    kv = kv_ref[...]
    outs = []
    for h in range(MEM_HEADS):
        sl = slice(h * MEM_HEAD_DIM, (h + 1) * MEM_HEAD_DIM)
        vsl = slice(D_MODEL + h * MEM_HEAD_DIM, D_MODEL + (h + 1) * MEM_HEAD_DIM)
        s = _dot_nt(q[:, sl], kv[:, sl])
        m = jnp.max(s, axis=1, keepdims=True)
        p = jnp.exp(s - m)
        den = jnp.sum(p, axis=1, keepdims=True)
        outs.append(_dot((p / den).astype(BF16), kv[:, vsl]).astype(BF16))
    o = jnp.concatenate(outs, axis=1)
    out_ref[...] = x + _dot(o, wo_ref[...])


def _xattn(x3d, kv3d, w, tm):
    b, s, d = x3d.shape
    n_mem = kv3d.shape[1]
    return pl.pallas_call(
        _xattn_kernel,
        grid=(b, s // tm),
        in_specs=[
            pl.BlockSpec((None, tm, d), lambda bi, i: (bi, i, 0)),
            _const_spec((1, d)),
            pl.BlockSpec((None, n_mem, 2 * d), lambda bi, i: (bi, 0, 0)),
            _const_spec(w['wmq'].shape), _const_spec(w['wmo'].shape),
        ],
        out_specs=pl.BlockSpec((None, tm, d), lambda bi, i: (bi, i, 0)),
        out_shape=jax.ShapeDtypeStruct((b, s, d), F32),
        compiler_params=_params(("parallel", "parallel")),
        name="xattn",
    )(x3d, w['g_xq'], kv3d, w['wmq'], w['wmo'])


def _top_values(sc, count, with_rank, break_ties):
    n = sc.shape[0]
    row = lax.broadcasted_iota(jnp.int32, sc.shape, 0)
    rank = jnp.full(sc.shape, float(n - 1), F32)
    vals = []
    for k in range(count):
        m = jnp.max(sc, axis=0, keepdims=True)
        hit = sc == m
        if break_ties:
            hit = row == jnp.min(jnp.where(hit, row, n), axis=0, keepdims=True)
        vals.append(m)
        if with_rank:
            rank = jnp.where(hit, float(k), rank)
        sc = jnp.where(hit, -jnp.inf, sc)
    dropped = jnp.sum(jnp.where(sc == -jnp.inf, 1.0, 0.0), axis=0, keepdims=True)
    return jnp.concatenate(vals, axis=0), rank, dropped


def _selected_per_first_key(s1, v1, v2, tau):
    need = []
    for j in range(PEER_TOPK):
        reach = (v1 + v2[j:j + 1]) >= tau
        need.append(jnp.min(jnp.where(reach, v1, jnp.inf), axis=0, keepdims=True))
    c8 = s1 >= need[7]
    c4 = s1 >= jnp.where(c8, need[11], need[3])
    c2 = s1 >= jnp.where(c8, jnp.where(c4, need[13], need[9]), jnp.where(c4, need[5], need[1]))
    low = jnp.where(c4, jnp.where(c2, need[6], need[4]), jnp.where(c2, need[2], need[0]))
    high = jnp.where(c4, jnp.where(c2, need[14], need[12]), jnp.where(c2, need[10], need[8]))
    c1 = s1 >= jnp.where(c8, high, low)
    c16 = s1 >= need[15]
    return (jnp.where(c8, 8.0, 0.0) + jnp.where(c4, 4.0, 0.0) + jnp.where(c2, 2.0, 0.0)
            + jnp.where(c1, 1.0, 0.0) + jnp.where(c16, 1.0, 0.0))


def _pack_bf16_rows(x):
    return pltpu.bitcast(x.astype(BF16), jnp.int32)


def _unpack_bf16_rows(words):
    return pltpu.bitcast(words, BF16)


def _bf16_pair_word(x):
    bits = lax.bitcast_convert_type(x.astype(BF16).astype(F32), jnp.uint32)
    return lax.bitcast_convert_type(bits | (bits >> 16), jnp.int32)


def _peer_score_kernel(x_ref, g_ref, wpq_ref, keys_ref, hqt_ref, r2s_ref, cnts_ref, e1s_ref, e2s_ref):
    hq32 = _rms(x_ref[...], g_ref[...])

    @pl.when(pl.program_id(1) == 0)
    def _():
        hqt_ref[...] = _pack_bf16_rows(hq32.T)

    q = _dot(hq32.astype(BF16), wpq_ref[...]).astype(BF16)
    heads = keys_ref.shape[0]
    halves = [[_dot_nt(keys_ref[h, p], q[:, (2 * h + p) * PEER_HALF:(2 * h + p + 1) * PEER_HALF])
               for p in range(2)] for h in range(heads)]

    def statistics(break_ties):
        worst = [head_statistics(h, break_ties) for h in range(heads)]
        return functools.reduce(jnp.maximum, worst)

    def head_statistics(h, break_ties):
        s1, s2 = halves[h]
        r2_ref, cnt_ref, e1_ref, e2_ref = (ref.at[h] for ref in (r2s_ref, cnts_ref, e1s_ref, e2s_ref))
        v1, _, d1 = _top_values(s1, PEER_TOPK, False, break_ties)
        v2, r2, d2 = _top_values(s2, PEER_TOPK, True, break_ties)
        cand = [v1[0:1] + v2]
        cand += [v1[i:i + 1] + v2[0:8] for i in range(1, 8)]
        cand += [v1[8:16] + v2[0:1]]
        top, _, d3 = _top_values(jnp.concatenate(cand, axis=0), PEER_TOPK, False, break_ties)
        z = jnp.sum(jnp.exp(top - top[0:1]), axis=0, keepdims=True)
        tau = top[PEER_TOPK - 1:PEER_TOPK]
        cnt = _selected_per_first_key(s1, v1, v2, tau)
        r2_ref[...] = _pack_bf16_rows(r2)
        cnt_ref[...] = _bf16_pair_word(cnt)
        e1_ref[...] = _bf16_pair_word(jnp.exp(s1 - v1[0:1]) * (0.5 / z))
        e2_ref[...] = _pack_bf16_rows(jnp.exp(s2 - v2[0:1]))
        return jnp.maximum(jnp.maximum(d1, d2), d3)

    most_dropped = statistics(break_ties=False)

    @pl.when(jnp.max(most_dropped) > PEER_TOPK)
    def _():
        statistics(break_ties=True)


def _peer_scores(x2d, w):
    t = x2d.shape[0]
    tt = PEER_TOKENS
    groups = t // tt
    words = lambda rows: jax.ShapeDtypeStruct((groups, PEER_HEADS, rows, tt), jnp.int32)
    hs = PEER_SCORE_HEADS
    spec = lambda rows: pl.BlockSpec((None, hs, rows, tt), lambda i, h: (i, h, 0, 0))
    half = PEER_NKEYS // 2
    return pl.pallas_call(
        _peer_score_kernel,
        grid=(groups, PEER_HEADS // hs),
        in_specs=[
            pl.BlockSpec((tt, D_MODEL), lambda i, h: (i, 0)),
            _const_spec((1, D_MODEL)),
            pl.BlockSpec((D_MODEL, hs * 2 * PEER_HALF), lambda i, h: (0, h)),
            pl.BlockSpec((hs, 2, PEER_NKEYS, PEER_HALF), lambda i, h: (h, 0, 0, 0)),
        ],
        out_specs=[pl.BlockSpec((None, D_MODEL // 2, tt), lambda i, h: (i, 0, 0)),
                   spec(half), spec(PEER_NKEYS), spec(PEER_NKEYS), spec(half)],
        out_shape=(jax.ShapeDtypeStruct((groups, D_MODEL // 2, tt), jnp.int32),
                   words(half), words(PEER_NKEYS), words(PEER_NKEYS), words(half)),
        compiler_params=_params(("parallel", "arbitrary")),
        name="peer_scores",
    )(x2d, w['g_ffn'], w['wpq'], w['keys'])


def _peer_mix_kernel(hqt_ref, u_ref, vt_ref, r2_ref, cnt_ref, e1_ref, e2_ref, x_ref, gf_ref,
                     out_ref, acc_ref, wg_ref, *, blocks):
    j = pl.program_id(1)
    last = pl.num_programs(1) - 1
    groups, _, tt = hqt_ref.shape
    half_rows = PEER_NKEYS // 2

    @pl.when(j == 0)
    def _():
        acc_ref[...] = jnp.zeros(acc_ref.shape, F32)
        wg_ref[...] = jnp.zeros(wg_ref.shape, jnp.int32)

    def mix_previous_block(g):
        acc_ref[g] += _dot(_unpack_bf16_rows(vt_ref[...]), _unpack_bf16_rows(wg_ref[g]))

    def row_bcast(words):
        return _unpack_bf16_rows(jnp.broadcast_to(words, (half_rows, LANES)))

    def activate_and_mix(g, carry):
        mix_previous_block(g)
        hqt = _unpack_bf16_rows(hqt_ref[g])
        for c in range(blocks // 2):
            u_pair = _unpack_bf16_rows(u_ref[c * PEER_NKEYS:(c + 1) * PEER_NKEYS, :])
            pre2 = _dot(u_pair, hqt)
            for half in range(2):
                i = 2 * c + half
                a = j * blocks + i
                cnt_rows = [cnt_ref[g, h, pl.ds(a, 1), :] for h in range(PEER_HEADS)]
                e1_rows = [e1_ref[g, h, pl.ds(a, 1), :] for h in range(PEER_HEADS)]
                for col in range(tt // LANES):
                    cs = slice(col * LANES, (col + 1) * LANES)
                    wsum = jnp.zeros((PEER_NKEYS, LANES), BF16)
                    for h in range(PEER_HEADS):
                        chosen = _unpack_bf16_rows(r2_ref[g, h, :, cs]) < row_bcast(cnt_rows[h][:, cs])
                        picked = jnp.where(chosen, _unpack_bf16_rows(e2_ref[g, h, :, cs]), jnp.zeros((), BF16))
                        wsum = wsum + picked * row_bcast(e1_rows[h][:, cs])
                    pre = pre2[half * PEER_NKEYS:(half + 1) * PEER_NKEYS, cs]
                    act = pre * (1.0 + lax.erf(pre * (0.5 ** 0.5)))
                    wg_ref[g, i * half_rows:(i + 1) * half_rows, cs] = pltpu.bitcast(
                        wsum * act.astype(BF16), jnp.int32)
        return carry

    @pl.when(j < last)
    def _():
        for g in range(groups):
            activate_and_mix(g, 0)

    @pl.when(j == last)
    def _():
        for g in range(groups):
            mix_previous_block(g)
            rows = slice(g * tt, (g + 1) * tt)
            out_ref[rows, :] = _rms(x_ref[rows, :] + acc_ref[g].T, gf_ref[...])


def _peer_mix(hqt, stats, x2d, w, groups, blocks):
    t = x2d.shape[0]
    tt = PEER_TOKENS
    rows = groups * tt
    ne = blocks * PEER_NKEYS
    nblk = PEER_NKEYS // blocks
    spec = lambda r: pl.BlockSpec((groups, PEER_HEADS, r, tt), lambda i, j: (i, 0, 0, 0))
    half = PEER_NKEYS // 2
    return pl.pallas_call(
        functools.partial(_peer_mix_kernel, blocks=blocks),
        grid=(t // rows, nblk + 1),
        in_specs=[
            pl.BlockSpec((groups, D_MODEL // 2, tt), lambda i, j: (i, 0, 0)),
            pl.BlockSpec((ne // 2, D_MODEL), lambda i, j: (jnp.minimum(j, nblk - 1), 0)),
            pl.BlockSpec((D_MODEL // 2, ne), lambda i, j: (0, jnp.maximum(j - 1, 0))),
            spec(half), spec(PEER_NKEYS), spec(PEER_NKEYS), spec(half),
            pl.BlockSpec((rows, D_MODEL), lambda i, j: (i, 0)),
            _const_spec((1, D_MODEL)),
        ],
        out_specs=pl.BlockSpec((rows, D_MODEL), lambda i, j: (i, 0)),
        out_shape=jax.ShapeDtypeStruct((t, D_MODEL), F32),
        scratch_shapes=[pltpu.VMEM((groups, D_MODEL, tt), F32), pltpu.VMEM((groups, ne // 2, tt), jnp.int32)],
        compiler_params=_params(("parallel", "arbitrary")),
        name="peer_mix",
    )(hqt, w['peer_u'], w['peer_vt'], *stats, x2d, w['g_final'])


def _pack_tables_kernel(u_ref, v_ref, uw_ref, vtw_ref):
    uw_ref[...] = _pack_bf16_rows(u_ref[...])
    vtw_ref[...] = _pack_bf16_rows(v_ref[...].T)


def _pack_expert_tables(u, v):
    n, d = u.shape
    tile = 512
    return pl.pallas_call(
        _pack_tables_kernel,
        grid=(n // tile,),
        in_specs=[pl.BlockSpec((tile, d), lambda i: (i, 0)), pl.BlockSpec((tile, d), lambda i: (i, 0))],
        out_specs=[pl.BlockSpec((tile // 2, d), lambda i: (i, 0)), pl.BlockSpec((d // 2, tile), lambda i: (0, i))],
        out_shape=(jax.ShapeDtypeStruct((n // 2, d), jnp.int32), jax.ShapeDtypeStruct((d // 2, n), jnp.int32)),
        compiler_params=_params(("parallel",)),
        name="pack_tables",
    )(u, v)


def _prep_weights(g_mix, w_in, g_qa, g_kva, w_uq, w_ukv, w_gate, b_gate, w_pa, w_pb, w_out,
                  g_xq, g_mkv, w_mq, w_mkv, w_mo, g_ffn, w_pq, sub_keys, peer_u, peer_v, g_final):
    w_in = w_in[0]
    q_cols = DSA_QK_COLS // 2
    wqk = jnp.concatenate([w_in[:, :q_cols] * (DSA_HEAD_DIM ** -0.5), w_in[:, q_cols:DSA_QK_COLS]], axis=1)
    c0 = DSA_QK_COLS + DSA_V_COLS
    c1 = c0 + MLA_Q_LORA + MLA_KV_LORA
    wkpe = jnp.zeros((D_MODEL, MLA_SLAB), F32).at[:, MLA_NOPE:MLA_QK].set(w_in[:, c1:])
    pad_head = lambda a: jnp.pad(a, ((0, 0), (0, 0), (0, MLA_SLAB - a.shape[2])))
    wq = pad_head(w_uq[0].reshape(MLA_Q_LORA, MLA_HEADS, MLA_QK)).reshape(MLA_Q_LORA, -1)
    ukv = w_ukv[0].reshape(MLA_KV_LORA, MLA_HEADS, MLA_NOPE + MLA_V)
    wk = pad_head(ukv[:, :, :MLA_NOPE]).reshape(MLA_KV_LORA, -1)
    wv = pad_head(ukv[:, :, MLA_NOPE:]).reshape(MLA_KV_LORA, -1)
    wpb = jnp.pad(w_pb[0].reshape(MLA_HEADS, MLA_V, D_MODEL), ((0, 0), (0, MLA_SLAB - MLA_V), (0, 0)))
    bf = lambda a: a.astype(BF16)
    u_words, vt_words = _pack_expert_tables(peer_u[0], peer_v[0])
    return {
        'g_mix': g_mix, 'wqk': bf(wqk), 'wv': bf(w_in[:, DSA_QK_COLS:c0]), 'wc': bf(w_in[:, c0:c1]),
        'wkpe': bf(wkpe), 'wgate': bf(w_gate[0]), 'b_gate': b_gate,
        'g_qa': g_qa, 'g_kva': g_kva, 'wq_mla': bf(wq), 'wk_mla': bf(wk), 'wv_mla': bf(wv),
        'wpa': bf(w_pa[0]), 'wpb': bf(wpb.reshape(-1, D_MODEL)), 'wout': bf(w_out[0]),
        'g_xq': g_xq, 'g_mkv': g_mkv, 'wmq': bf(w_mq[0]), 'wmkv': bf(w_mkv[0]), 'wmo': bf(w_mo[0]),
        'g_ffn': g_ffn, 'wpq': bf(w_pq[0]),
        'keys': bf(sub_keys[0]),
        'peer_u': u_words, 'peer_vt': vt_words,
        'g_final': g_final.reshape(1, D_MODEL),
    }


def _rope_tables(seq):
    def table(rot, lead, period):
        inv = ROPE_THETA ** (-jnp.arange(0, rot, 2, dtype=F32) / rot)
        ang = jnp.arange(seq, dtype=F32)[:, None] * inv[None, :]
        cos, sin = jnp.cos(ang), jnp.sin(ang)
        ones = lambda n: jnp.ones((seq, n), F32)
        zeros = lambda n: jnp.zeros((seq, n), F32)
        tail = period - lead - rot
        c = jnp.concatenate([ones(lead), cos, cos, ones(tail)], axis=1)
        s = jnp.concatenate([zeros(lead), -sin, sin, zeros(tail)], axis=1)
        reps = LANES // period
        return jnp.tile(c, (1, reps)), jnp.tile(s, (1, reps))
    ca, sa = table(DSA_ROT, 0, DSA_HEAD_DIM)
    cb, sb = table(MLA_ROPE, MLA_NOPE, MLA_SLAB)
    return {'ca': ca, 'sa': sa, 'cb': cb, 'sb': sb}


def _trunk(x, mem, w):
    b, s, d = x.shape
    t = b * s
    tabs = _rope_tables(s)
    x2d = x.reshape(t, d)
    tm = ROW_TILE
    *dsa_in, cq, ckv, kpe, gates = _inproj(x2d, b, s, w, tabs, tm)
    dsa_out = [_dsa_group(*dsa_in[3 * g:3 * g + 3], g, DSA_QUERY_TILE) for g in range(len(DSA_DILATIONS))]
    q_b, k_b, v_b = _mla_proj(cq, ckv, kpe, s, w, tabs, tm)
    wide = MLA_HEADS * MLA_SLAB
    yb = _mla_attn(q_b.reshape(b, s, wide), k_b.reshape(b, s, wide), v_b.reshape(b, s, wide),
                   min(MLA_QUERY_TILE, s), min(MLA_KEY_TILE, s))
    x1 = _merge(dsa_out, yb.reshape(t, wide), gates, x2d, s, w, tm)
    n_mem = mem.shape[1]
    kv = _norm_mm(mem.reshape(b * n_mem, d), w['g_mkv'], w['wmkv'], n_mem)
    x2 = _xattn(x1.reshape(b, s, d), kv.reshape(b, n_mem, 2 * d), w, tm)
    x2 = x2.reshape(t, d)
    hqt, *stats = _peer_scores(x2, w)
    y = _peer_mix(hqt, stats, x2, w, PEER_MIX_GROUPS, PEER_MIX_BLOCKS)
    return y.reshape(b, s, d)


def kernel(x_prompt, x_sample, mem_prompt, mem_sample, g_mix, w_in, g_qa, g_kva, w_uq, w_ukv, w_gate,
           b_gate, w_pa, w_pb, w_out, g_xq, g_mkv, w_mq, w_mkv, w_mo, g_ffn, w_pq, sub_keys, peer_u,
           peer_v, g_final):
    w = _prep_weights(g_mix, w_in, g_qa, g_kva, w_uq, w_ukv, w_gate, b_gate, w_pa, w_pb, w_out,
                      g_xq, g_mkv, w_mq, w_mkv, w_mo, g_ffn, w_pq, sub_keys, peer_u, peer_v, g_final)
    return (_trunk(x_prompt, mem_prompt, w), _trunk(x_sample, mem_sample, w))
```
